```python
import math
import jax
import jax.numpy as jnp
from jax import lax
import numpy as np

D_MODEL = 1024
BATCH = 16
SEQ = 2048
DEPTH = 2

HEAD_DIM = 64
ATTN_Q_HEADS = 8
ATTN_KV_HEADS = 2
ATTN_GROUP = ATTN_Q_HEADS // ATTN_KV_HEADS
WINDOW = 128
BLOCK = 128
SPAN = BLOCK + 2 * WINDOW
N_BUCKETS = 32
MAX_DISTANCE = 128
DN_HEADS = 8
DN_DK = 64
DN_DV = 64
DN_CONV = 5
DN_CHUNK = 64
POOL_WINDOWS = (2, 4, 8, 16)
N_POOL_GROUPS = 4
POOL_GROUP = 128
ATTN_W = ATTN_Q_HEADS * HEAD_DIM
KV_W = ATTN_KV_HEADS * HEAD_DIM
DN_QK_W = DN_HEADS * DN_DK
DN_V_W = DN_HEADS * DN_DV
DN_CONV_CH = 2 * DN_QK_W + DN_V_W
POOL_W = N_POOL_GROUPS * POOL_GROUP
BRANCH_W = 512
N_BRANCH = 3
SPLIT_SIZES = (ATTN_W, KV_W, KV_W, DN_QK_W, DN_QK_W, DN_V_W, DN_V_W, 4 * DN_HEADS, POOL_W, N_BRANCH * D_MODEL)
IN_COLS = sum(SPLIT_SIZES)
D_FF = 2816
N_EXPERTS = 8
TOP_K = 2
D_FF_EXPERT = 3584
N_DENSE = (DEPTH + 1) // 2
N_MOE = DEPTH // 2
DEEPNORM_ALPHA = (2.0 * DEPTH) ** 0.25
DEEPNORM_BETA = (8.0 * DEPTH) ** -0.25
LN_EPS = 1e-5
RMS_EPS = 1e-6
NEG_INF = -1e30

kernel_name = 'hybrid_gated_attn_deltanet_pool_moe_encoder'

f32 = jnp.float32


def layer_norm(x, g, b):
    xf = x.astype(f32)
    mu = jnp.mean(xf, -1, keepdims=True)
    var = jnp.mean(jnp.square(xf - mu), -1, keepdims=True)
    return ((xf - mu) * lax.rsqrt(var + LN_EPS) * g + b).astype(x.dtype)


def _band_rel():
    q_loc = np.arange(BLOCK)[:, None]
    k_loc = np.arange(SPAN)[None, :]
    return k_loc - WINDOW - q_loc


def _t5_bucket(rel):
    nb = N_BUCKETS // 2
    max_exact = nb // 2
    n = np.abs(rel)
    large = max_exact + (np.log(np.maximum(n, 1) / max_exact) / np.log(MAX_DISTANCE / max_exact)
                         * (nb - max_exact)).astype(np.int32)
    large = np.minimum(large, nb - 1)
    return (np.where(rel > 0, nb, 0) + np.where(n < max_exact, n, large)).astype(np.int32)


def windowed_gqa(q, k, v, sink, bias):
    B, S = q.shape[0], q.shape[1]
    n_blocks = S // BLOCK
    in_window = jnp.asarray(np.abs(_band_rel()) <= WINDOW)
    qg = q.reshape(B, S, ATTN_KV_HEADS, ATTN_GROUP, HEAD_DIM)
    kp = jnp.pad(k, ((0, 0), (WINDOW, WINDOW), (0, 0), (0, 0)))
    vp = jnp.pad(v, ((0, 0), (WINDOW, WINDOW), (0, 0), (0, 0)))
    bias_g = bias.reshape(ATTN_KV_HEADS, ATTN_GROUP, BLOCK, SPAN)
    sink_g = sink.astype(f32).reshape(ATTN_KV_HEADS, ATTN_GROUP, 1, 1)
    scale = HEAD_DIM ** -0.5

    def one_block(i):
        start = i * BLOCK
        qb = lax.dynamic_slice_in_dim(qg, start, BLOCK, axis=1).astype(f32)
        kb = lax.dynamic_slice_in_dim(kp, start, SPAN, axis=1).astype(f32)
        vb = lax.dynamic_slice_in_dim(vp, start, SPAN, axis=1).astype(f32)
        kpos = start - WINDOW + jnp.arange(SPAN)
        valid = in_window & ((kpos >= 0) & (kpos < S))[None, :]
        s = jnp.einsum('bqkgd,bjkd->bkgqj', qb, kb) * scale + bias_g
        s = jnp.where(valid, s, NEG_INF)
        m = jnp.maximum(jnp.max(s, -1, keepdims=True), sink_g)
        p = jnp.exp(s - m)
        denom = jnp.sum(p, -1, keepdims=True) + jnp.exp(sink_g - m)
        o = jnp.einsum('bkgqj,bjkd->bqkgd', p / denom, vb)
        return o.astype(q.dtype)

    o = lax.map(one_block, jnp.arange(n_blocks))
    return jnp.moveaxis(o, 0, 1).reshape(B, S, ATTN_W)


def short_conv(u, w):
    pad = (DN_CONV - 1) // 2
    return lax.conv_general_dilated(u, w[:, None, :].astype(u.dtype), window_strides=(1,),
                                    padding=[(pad, pad)], dimension_numbers=('NWC', 'WIO', 'NWC'),
                                    feature_group_count=u.shape[-1])


def l2norm(t):
    return t * lax.rsqrt(jnp.sum(t * t, -1, keepdims=True) + 1e-6)


def gated_delta_chunked(q, k, v, g, beta):
    R, B, S, H, dk = q.shape
    dv = v.shape[-1]
    C = DN_CHUNK
    N = S // C

    def to_chunks(t):
        t = t.reshape(t.shape[:2] + (N, C) + t.shape[3:])
        return jnp.transpose(t, (2, 0, 1, 4, 3) + tuple(range(5, t.ndim)))

    q = l2norm(q) * (dk ** -0.5)
    k = l2norm(k)
    qc, kc, vc = to_chunks(q), to_chunks(k), to_chunks(v)
    gc, bc = to_chunks(g), to_chunks(beta)
    G = jnp.cumsum(gc, axis=-1)
    lower = jnp.tril(jnp.ones((C, C), bool))
    strict = jnp.tril(jnp.ones((C, C), bool), -1)
    diff = G[..., :, None] - G[..., None, :]
    decay = jnp.where(lower, jnp.exp(jnp.where(lower, diff, 0.0)), 0.0)
    kb = kc * bc[..., None]
    A = jnp.where(strict, jnp.einsum('...id,...jd->...ij', kb, kc) * decay, 0.0)
    rhs = jnp.concatenate([vc * bc[..., None], kb * jnp.exp(G)[..., None]], axis=-1)
    sol = lax.linalg.triangular_solve(A, rhs, left_side=True, lower=True, unit_diagonal=True)
    u_c, w_c = sol[..., :dv], sol[..., dv:]
    qk = jnp.where(lower, jnp.einsum('...id,...jd->...ij', qc, kc) * decay, 0.0)
    q_dec = qc * jnp.exp(G)[..., None]
    G_last = G[..., -1:]
    k_dec = kc * jnp.exp(G_last - G)[..., None]
    chunk_decay = jnp.exp(G_last)[..., None]

    def step(state, xs):
        q_i, k_i, qk_i, u_i, w_i, cd_i = xs
        v_new = u_i - jnp.einsum('rbhcd,rbhde->rbhce', w_i, state)
        o = jnp.einsum('rbhcd,rbhde->rbhce', q_i, state) + jnp.einsum('rbhij,rbhje->rbhie', qk_i, v_new)
        state = state * cd_i + jnp.einsum('rbhcd,rbhce->rbhde', k_i, v_new)
        return state, o

    s0 = jnp.zeros((R, B, H, dk, dv), f32)
    _, o = lax.scan(step, s0, (q_dec, k_dec, qk, u_c, w_c, chunk_decay))
    return jnp.transpose(o, (1, 2, 0, 4, 3, 5)).reshape(R, B, S, H, dv)


def delta_branch(q, k, v, z, a, b, conv_w, a_log, dt_bias, norm_w):
    B, S = q.shape[0], q.shape[1]
    dtype = q.dtype
    qkv = jax.nn.silu(short_conv(jnp.concatenate([q, k, v], -1), conv_w)).astype(f32)
    qh = qkv[..., :DN_QK_W].reshape(B, S, DN_HEADS, DN_DK)
    kh = qkv[..., DN_QK_W:2 * DN_QK_W].reshape(B, S, DN_HEADS, DN_DK)
    vh = qkv[..., 2 * DN_QK_W:].reshape(B, S, DN_HEADS, DN_DV)
    g = -jnp.exp(a_log.astype(f32)) * jax.nn.softplus(a.astype(f32) + dt_bias.astype(f32))
    beta = jax.nn.sigmoid(b.astype(f32))

    def both(t):
        return jnp.stack([t, jnp.flip(t, axis=1)])

    g2 = jnp.stack([g[:, :, 0], jnp.flip(g[:, :, 1], axis=1)])
    beta2 = jnp.stack([beta[:, :, 0], jnp.flip(beta[:, :, 1], axis=1)])
    o = gated_delta_chunked(both(qh), both(kh), both(vh), g2, beta2)
    o = o[0] + jnp.flip(o[1], axis=1)
    zh = z.astype(f32).reshape(B, S, DN_HEADS, DN_DV)
    o = o * lax.rsqrt(jnp.mean(o * o, -1, keepdims=True) + RMS_EPS) * norm_w.astype(f32) * jax.nn.silu(zh)
    return o.reshape(B, S, DN_V_W).astype(dtype)


def pool_branch(u, pool_w, pool_scale):
    B, S = u.shape[0], u.shape[1]
    groups = u.reshape(B, S, N_POOL_GROUPS, POOL_GROUP).astype(f32)
    t = np.arange(S)
    outs = []
    for gi, w in enumerate(POOL_WINDOWS):
        r = w // 2
        ug = groups[:, :, gi]
        cs = jnp.cumsum(jnp.pad(ug, ((0, 0), (r + 1, r), (0, 0))), axis=1)
        win_sum = cs[:, 2 * r + 1:] - cs[:, :S]
        count = (np.minimum(t + r, S - 1) - np.maximum(t - r, 0) + 1).astype(np.float32)
        outs.append(win_sum / jnp.asarray(count)[None, :, None] - ug)
    mixed = jnp.stack(outs, axis=2)
    y = jnp.einsum('bsgc,gcd->bsgd', mixed, pool_w.astype(f32))
    return (y.reshape(B, S, POOL_W) * pool_scale.astype(f32)).astype(u.dtype)


def hybrid_mixer(x, w_in, attn_sink, bias, conv_w, a_log, dt_bias, dn_norm_w, pool_w, pool_scale, w_branch, w_out):
    B, S = x.shape[0], x.shape[1]
    proj = x @ w_in
    points = np.cumsum(SPLIT_SIZES)[:-1].tolist()
    aq, ak, av, dq, dk, dvv, dz, dab, pu, gates = jnp.split(proj, points, axis=-1)
    o_attn = windowed_gqa(aq.reshape(B, S, ATTN_Q_HEADS, HEAD_DIM),
                          ak.reshape(B, S, ATTN_KV_HEADS, HEAD_DIM),
                          av.reshape(B, S, ATTN_KV_HEADS, HEAD_DIM), attn_sink, bias)
    dab = dab.reshape(B, S, 4, DN_HEADS)
    o_delta = delta_branch(dq, dk, dvv, dz, dab[:, :, :2], dab[:, :, 2:], conv_w, a_log, dt_bias, dn_norm_w)
    o_pool = pool_branch(pu, pool_w, pool_scale)
    branches = jnp.stack([o_attn, o_delta, o_pool], axis=2)
    up = jnp.einsum('bsnc,ncd->bsnd', branches, w_branch)
    gate = jax.nn.sigmoid(gates.reshape(B, S, N_BRANCH, D_MODEL))
    merged = jnp.sum(gate * up, axis=2)
    return merged @ w_out


def swiglu(x, w_gate, w_up, w_down):
    return (jax.nn.silu(x @ w_gate) * (x @ w_up)) @ w_down


def moe_swiglu(x, router_w, wg, wu, wd):
    logits = (x @ router_w).astype(f32)
    top_logit, top_idx = lax.top_k(logits, TOP_K)
    top_p = jax.nn.softmax(top_logit, axis=-1)
    combine = jnp.sum(jax.nn.one_hot(top_idx, N_EXPERTS, dtype=f32) * top_p[..., None], axis=-2)
    out = jnp.zeros(x.shape, f32)
    for e in range(N_EXPERTS):
        out = out + combine[..., e:e + 1] * swiglu(x, wg[e], wu[e], wd[e]).astype(f32)
    return out.astype(x.dtype)


def setup_inputs(seed: int = 0) -> dict:
    key = jax.random.key(seed)
    ks = jax.random.split(key, 24)
    L = DEPTH
    D = D_MODEL

    def nrm(k, shape, scale):
        return jax.random.normal(k, shape, f32) * scale

    x = nrm(ks[0], (BATCH, SEQ, D), 1.0)
    w_in = nrm(ks[1], (L, D, IN_COLS), D ** -0.5)
    attn_sink = nrm(ks[2], (L, ATTN_Q_HEADS), 0.5)
    rel_bias = nrm(ks[3], (N_BUCKETS, ATTN_Q_HEADS), 0.5)
    conv_w = nrm(ks[4], (L, DN_CONV, DN_CONV_CH), DN_CONV ** -0.5)
    a_log = jnp.log(jax.random.uniform(ks[5], (L, 2, DN_HEADS), f32, 1.0, 16.0))
    dt = jnp.exp(jax.random.uniform(ks[6], (L, 2, DN_HEADS), f32, math.log(1e-3), math.log(1e-1)))
    dt_bias = dt + jnp.log(-jnp.expm1(-dt))
    dn_norm_w = 1.0 + nrm(ks[7], (L, DN_DV), 0.1)
    pool_w = nrm(ks[8], (L, N_POOL_GROUPS, POOL_GROUP, POOL_GROUP), POOL_GROUP ** -0.5)
    pool_scale = 1.0 + nrm(ks[9], (L, POOL_W), 0.1)
    w_branch = nrm(ks[10], (L, N_BRANCH, BRANCH_W, D), BRANCH_W ** -0.5)
    w_out = nrm(ks[11], (L, D, D), DEEPNORM_BETA * D ** -0.5)
    ln1_g = 1.0 + nrm(ks[12], (L, D), 0.1)
    ln1_b = nrm(ks[13], (L, D), 0.02)
    ln2_g = 1.0 + nrm(ks[14], (L, D), 0.1)
    ln2_b = nrm(ks[15], (L, D), 0.02)
    ffn_w1 = nrm(ks[16], (N_DENSE, D, D_FF), D ** -0.5)
    ffn_w3 = nrm(ks[17], (N_DENSE, D, D_FF), D ** -0.5)
    ffn_w2 = nrm(ks[18], (N_DENSE, D_FF, D), DEEPNORM_BETA * D_FF ** -0.5)
    router_w = nrm(ks[19], (N_MOE, D, N_EXPERTS), D ** -0.5)
    moe_wg = nrm(ks[20], (N_MOE, N_EXPERTS, D, D_FF_EXPERT), D ** -0.5)
    moe_wu = nrm(ks[21], (N_MOE, N_EXPERTS, D, D_FF_EXPERT), D ** -0.5)
    moe_wd = nrm(ks[22], (N_MOE, N_EXPERTS, D_FF_EXPERT, D), DEEPNORM_BETA * D_FF_EXPERT ** -0.5)
    return {'x': x, 'w_in': w_in, 'attn_sink': attn_sink, 'rel_bias': rel_bias, 'conv_w': conv_w,
            'a_log': a_log, 'dt_bias': dt_bias, 'dn_norm_w': dn_norm_w, 'pool_w': pool_w,
            'pool_scale': pool_scale, 'w_branch': w_branch, 'w_out': w_out, 'ln1_g': ln1_g,
            'ln1_b': ln1_b, 'ln2_g': ln2_g, 'ln2_b': ln2_b, 'ffn_w1': ffn_w1, 'ffn_w3': ffn_w3,
            'ffn_w2': ffn_w2, 'router_w': router_w, 'moe_wg': moe_wg, 'moe_wu': moe_wu, 'moe_wd': moe_wd}


def reference(x, w_in, attn_sink, rel_bias, conv_w, a_log, dt_bias, dn_norm_w, pool_w, pool_scale,
              w_branch, w_out, ln1_g, ln1_b, ln2_g, ln2_b, ffn_w1, ffn_w3, ffn_w2,
              router_w, moe_wg, moe_wu, moe_wd):
    bias = jnp.transpose(rel_bias[_t5_bucket(_band_rel())], (2, 0, 1)).astype(f32)
    h = x
    for layer in range(DEPTH):
        y = hybrid_mixer(h, w_in[layer], attn_sink[layer], bias, conv_w[layer], a_log[layer], dt_bias[layer],
                         dn_norm_w[layer], pool_w[layer], pool_scale[layer], w_branch[layer], w_out[layer])
        h = layer_norm(DEEPNORM_ALPHA * h + y, ln1_g[layer], ln1_b[layer])
        i = layer // 2
        if layer % 2 == 0:
            y = swiglu(h, ffn_w1[i], ffn_w3[i], ffn_w2[i])
        else:
            y = moe_swiglu(h, router_w[i], moe_wg[i], moe_wu[i], moe_wd[i])
        h = layer_norm(DEEPNORM_ALPHA * h + y, ln2_g[layer], ln2_b[layer])
    return h
```

```python
import functools

import numpy as np
import jax
import jax.numpy as jnp
from jax import lax
from jax.experimental import pallas as pl
from jax.experimental.pallas import tpu as pltpu

f32 = jnp.float32
bf16 = jnp.bfloat16

D_MODEL = 1024
DEPTH = 2
HEAD_DIM = 64
ATTN_Q_HEADS = 8
ATTN_KV_HEADS = 2
ATTN_GROUP = ATTN_Q_HEADS // ATTN_KV_HEADS
WINDOW = 128
BLOCK = 128
SPAN = BLOCK + 2 * WINDOW
N_BUCKETS = 32
MAX_DISTANCE = 128
DN_HEADS = 8
DN_DK = 64
DN_DV = 64
DN_CONV = 5
DN_CHUNK = 64
POOL_WINDOWS = (2, 4, 8, 16)
N_POOL_GROUPS = 4
POOL_GROUP = 128
ATTN_W = ATTN_Q_HEADS * HEAD_DIM
KV_W = ATTN_KV_HEADS * HEAD_DIM
DN_QK_W = DN_HEADS * DN_DK
DN_V_W = DN_HEADS * DN_DV
POOL_W = N_POOL_GROUPS * POOL_GROUP
N_BRANCH = 3
N_EXPERTS = 8
TOP_K = 2
DEEPNORM_ALPHA = (2.0 * DEPTH) ** 0.25
LN_EPS = 1e-5
RMS_EPS = 1e-6
NEG_INF = -1e30

VMEM_LIMIT_BYTES = 56 * 1024 * 1024
POOL_TILE = 256
POOL_SPAN = 512
CONV_HALO = 8


def _cparams(sem):
    return pltpu.CompilerParams(dimension_semantics=sem, vmem_limit_bytes=VMEM_LIMIT_BYTES)


def _full(shape):
    n = len(shape)
    return pl.BlockSpec(shape, lambda *_: (0,) * n)


def _sigmoid(x):
    return 1.0 / (1.0 + jnp.exp(-x))


def _silu(x):
    return x * _sigmoid(x)


def _layer_norm(x, g, b):
    mu = jnp.mean(x, -1, keepdims=True)
    xc = x - mu
    var = jnp.mean(xc * xc, -1, keepdims=True)
    return xc * lax.rsqrt(var + LN_EPS) * g + b


def _dot(a, b):
    return jnp.dot(a, b, preferred_element_type=f32)


def _dot_nt(a, b):
    return lax.dot_general(a, b, (((1,), (1,)), ((), ())), preferred_element_type=f32)


def _dot_tn(a, b):
    return lax.dot_general(a, b, (((0,), (0,)), ((), ())), preferred_element_type=f32)


def _split3(x):
    hi = x.astype(bf16)
    r = x - hi.astype(f32)
    mid = r.astype(bf16)
    lo = (r - mid.astype(f32)).astype(bf16)
    return hi, mid, lo


def _dot_exact_lhs(a_bf, x):
    hi, mid, lo = _split3(x)
    return _dot(a_bf, hi) + _dot(a_bf, mid) + _dot(a_bf, lo)


def _dot_exact_rhs(x, b_bf):
    hi, mid, lo = _split3(x)
    return _dot(hi, b_bf) + _dot(mid, b_bf) + _dot(lo, b_bf)


def _dot_f32(a, b):
    return jnp.dot(a, b, preferred_element_type=f32, precision=lax.Precision.HIGHEST)


def _inproj_kernel(x_ref, *refs):
    n = len(refs) // 2
    x = x_ref[...]
    for w_ref, o_ref in zip(refs[:n], refs[n:]):
        o_ref[...] = _dot(x, w_ref[...]).astype(o_ref.dtype)


def _inproj(xb, ws, dtypes, tm=256):
    m = xb.shape[0]
    return pl.pallas_call(
        _inproj_kernel,
        grid=(m // tm,),
        in_specs=[pl.BlockSpec((tm, D_MODEL), lambda i: (i, 0))]
        + [pl.BlockSpec(w.shape, lambda i: (0, 0), pipeline_mode=pl.Buffered(1)) for w in ws],
        out_specs=[pl.BlockSpec((tm, w.shape[1]), lambda i: (i, 0)) for w in ws],
        out_shape=[jax.ShapeDtypeStruct((m, w.shape[1]), dt) for w, dt in zip(ws, dtypes)],
        compiler_params=_cparams(("parallel",)),
        name="inproj",
    )(xb, *ws)


def _t5_bucket(rel):
    nb = N_BUCKETS // 2
    max_exact = nb // 2
    n = np.abs(rel)
    large = max_exact + (np.log(np.maximum(n, 1) / max_exact) / np.log(MAX_DISTANCE / max_exact)
                         * (nb - max_exact)).astype(np.int32)
    large = np.minimum(large, nb - 1)
    return (np.where(rel > 0, nb, 0) + np.where(n < max_exact, n, large)).astype(np.int32)


def _attn_bias_tables(rel_bias):
    r = np.arange(BLOCK)[:, None]
    j = np.arange(SPAN)[None, :]
    tabs = []
    for shift in (0, WINDOW, 2 * WINDOW):
        rel = j - shift - r
        b = jnp.transpose(rel_bias[_t5_bucket(rel)], (2, 0, 1)).astype(f32)
        tabs.append(jnp.where(jnp.asarray(np.abs(rel) <= WINDOW)[None], b, NEG_INF))
    return jnp.stack(tabs)


def _attn_kernel(q_ref, kv_ref, bias_ref, sink_ref, o_ref, *, seq):
    i = pl.program_id(1)
    ks = pl.multiple_of(jnp.clip(i * BLOCK - WINDOW, 0, seq - SPAN), BLOCK)
    kv = kv_ref[0, pl.ds(ks, SPAN), :]
    q = q_ref[0]
    scale = HEAD_DIM ** -0.5
    outs = []
    for h in range(ATTN_Q_HEADS):
        g = h // ATTN_GROUP
        qh = q[:, HEAD_DIM * h:HEAD_DIM * (h + 1)]
        kh = kv[:, HEAD_DIM * g:HEAD_DIM * (g + 1)]
        vh = kv[:, KV_W + HEAD_DIM * g:KV_W + HEAD_DIM * (g + 1)]
        s = _dot_nt(qh, kh) * scale + bias_ref[0, h]
        sk = sink_ref[0:1, h:h + 1]
        m = jnp.maximum(jnp.max(s, -1, keepdims=True), sk)
        p = jnp.exp(s - m)
        denom = jnp.sum(p, -1, keepdims=True) + jnp.exp(sk - m)
        outs.append(_dot(p.astype(bf16), vh) / denom)
    o_ref[0] = jnp.concatenate(outs, -1).astype(o_ref.dtype)


def _attention(aq, akv, bias_tabs, sink):
    b, s, _ = aq.shape
    nb = s // BLOCK
    assert s % BLOCK == 0 and s >= SPAN

    def variant(bi, i):
        return (jnp.where(i == 0, 0, jnp.where(i == nb - 1, 2, 1)), 0, 0, 0)

    return pl.pallas_call(
        functools.partial(_attn_kernel, seq=s),
        grid=(b, nb),
        in_specs=[pl.BlockSpec((1, BLOCK, ATTN_W), lambda bi, i: (bi, i, 0)),
                  pl.BlockSpec((1, s, 2 * KV_W), lambda bi, i: (bi, 0, 0)),
                  pl.BlockSpec((1, ATTN_Q_HEADS, BLOCK, SPAN), variant),
                  _full((1, ATTN_Q_HEADS))],
        out_specs=pl.BlockSpec((1, BLOCK, ATTN_W), lambda bi, i: (bi, i, 0)),
        out_shape=jax.ShapeDtypeStruct((b, s, ATTN_W), bf16),
        compiler_params=_cparams(("parallel", "arbitrary")),
        name="attn",
    )(aq, akv, bias_tabs, sink.reshape(1, ATTN_Q_HEADS).astype(f32))


def _dn_prep_kernel(u_ref, w_ref, ones_ref, o_ref, pad_ref, *, seq):
    sec = pl.program_id(1)
    zeros = jnp.zeros((CONV_HALO, DN_QK_W), f32)
    pad_ref[0:CONV_HALO, :] = zeros
    pad_ref[seq + CONV_HALO:seq + 2 * CONV_HALO, :] = zeros
    pad_ref[CONV_HALO:seq + CONV_HALO, :] = u_ref[0].astype(f32)
    reach = (DN_CONV - 1) // 2
    acc = None
    for k in range(DN_CONV):
        term = pad_ref[CONV_HALO - reach + k:CONV_HALO - reach + k + seq, :] * w_ref[0, k:k + 1, :]
        acc = term if acc is None else acc + term
    y = _silu(acc)

    @pl.when(sec < 2)
    def _():
        ss = _dot_exact_rhs(y * y, ones_ref[...])
        scale = jnp.where(sec == 0, DN_DK ** -0.5, 1.0)
        o_ref[0] = y * lax.rsqrt(ss + 1e-6) * scale

    @pl.when(sec == 2)
    def _():
        o_ref[0] = y


def _head_ones():
    blk = np.arange(DN_QK_W) // DN_DK
    return jnp.asarray(blk[:, None] == blk[None, :], dtype=bf16)


def _dn_prep(dqkv, conv_w):
    b, s, _ = dqkv.shape
    w3 = jnp.transpose(conv_w.reshape(DN_CONV, 3, DN_QK_W), (1, 0, 2)).astype(f32)
    return pl.pallas_call(
        functools.partial(_dn_prep_kernel, seq=s),
        grid=(b, 3),
        in_specs=[pl.BlockSpec((1, s, DN_QK_W), lambda bi, c: (bi, 0, c)),
                  pl.BlockSpec((1, DN_CONV, DN_QK_W), lambda bi, c: (c, 0, 0)),
                  _full((DN_QK_W, DN_QK_W))],
        out_specs=pl.BlockSpec((1, s, DN_QK_W), lambda bi, c: (bi, 0, c)),
        out_shape=jax.ShapeDtypeStruct(dqkv.shape, f32),
        scratch_shapes=[pltpu.VMEM((s + 2 * CONV_HALO, DN_QK_W), f32)],
        compiler_params=_cparams(("parallel", "arbitrary")),
        name="dn_prep",
    )(dqkv, w3, _head_ones())


def _softplus(x):
    return jnp.maximum(x, 0.0) + jnp.log1p(jnp.exp(-jnp.abs(x)))


def _delta_kernel(q_ref, k_ref, v_ref, ab_ref, abt_ref, alog_ref, dtb_ref, alogt_ref, dtbt_ref, o_ref, state_ref):
    d = pl.program_id(1)
    c = pl.program_id(2)
    C = DN_CHUNK
    H = DN_HEADS

    @pl.when(c == 0)
    def _():
        state_ref[...] = jnp.zeros(state_ref.shape, f32)

    ii = lax.broadcasted_iota(jnp.int32, (C, C), 0)
    jj = lax.broadcasted_iota(jnp.int32, (C, C), 1)
    fwd = d == 0
    order = (ii - jj) * (1 - 2 * d)
    incl = order >= 0
    strict = order > 0
    tri = jnp.where(incl, 1.0, 0.0).astype(bf16)
    tri_t = jnp.where(order <= 0, 1.0, 0.0).astype(bf16)
    eye = jnp.where(ii == jj, 1.0, 0.0)

    ab = ab_ref[0, 0, 0]
    abt = abt_ref[0, 0, 0]
    g_col = -jnp.exp(alog_ref[0]) * _softplus(ab[:, :H] + dtb_ref[0])
    beta = _sigmoid(ab[:, H:])
    g_row = -jnp.exp(alogt_ref[0]) * _softplus(abt[:H, :] + dtbt_ref[0])
    G_col = _dot_exact_lhs(tri, g_col)
    G_row = _dot_exact_rhs(g_row, tri_t)
    last = jnp.where(fwd, C - 1, 0)
    lane = lax.broadcasted_iota(jnp.int32, (H, C), 1)
    G_last = jnp.sum(jnp.where(lane == last, G_row, 0.0), -1, keepdims=True)

    q = q_ref[0]
    k = k_ref[0]
    v = v_ref[0]
    outs = []
    for h in range(H):
        sl = slice(DN_DK * h, DN_DK * (h + 1))
        qh, kh, vh = q[:, sl], k[:, sl], v[:, sl]
        gc = G_col[:, h:h + 1]
        gr = G_row[h:h + 1, :]
        gl = G_last[h:h + 1, :]
        bcol = beta[:, h:h + 1]
        decay = jnp.where(incl, jnp.exp(jnp.where(incl, gc - gr, 0.0)), 0.0)
        kb = kh * bcol
        khb = kh.astype(bf16)
        a_mat = jnp.where(strict, _dot_nt(kb.astype(bf16), khb) * decay, 0.0)
        eg = jnp.exp(gc)
        rhs = jnp.concatenate([vh * bcol, kb * eg], -1)
        p = -a_mat
        t = eye + p
        for _ in range(5):
            p = _dot_f32(p, p)
            t = t + _dot_f32(t, p)
        sol = _dot_f32(t, rhs)
        u_c, w_c = sol[:, :DN_DV], sol[:, DN_DV:]
        qk = jnp.where(incl, _dot_nt(qh.astype(bf16), khb) * decay, 0.0)
        q_dec = qh * eg
        k_dec = kh * jnp.exp(gl - gc)
        st = state_ref[h]
        stb = st.astype(bf16)
        v_new = u_c - _dot(w_c.astype(bf16), stb)
        outs.append(_dot(q_dec.astype(bf16), stb) + _dot(qk.astype(bf16), v_new.astype(bf16)))
        state_ref[h] = st * jnp.exp(gl) + _dot_tn(k_dec.astype(bf16), v_new.astype(bf16))
    o_ref[0, 0] = jnp.concatenate(outs, -1)


def _delta(qkvn, dab, a_log, dt_bias):
    b, s, _ = qkvn.shape
    C, H = DN_CHUNK, DN_HEADS
    nc = s // C
    d4 = dab.reshape(b, s, 4, H)
    ab = jnp.stack([jnp.concatenate([d4[:, :, r], d4[:, :, 2 + r]], -1) for r in range(2)])
    ab = ab.reshape(2, b, nc, C, 2 * H)
    abt = jnp.swapaxes(ab, -1, -2)
    alog = a_log.astype(f32).reshape(2, 1, H)
    dtb = dt_bias.astype(f32).reshape(2, 1, H)
    alogt = a_log.astype(f32).reshape(2, H, 1)
    dtbt = dt_bias.astype(f32).reshape(2, H, 1)

    def chunk(d, c):
        return c + d * (nc - 1 - 2 * c)

    def sec(n):
        return pl.BlockSpec((1, C, DN_QK_W), lambda bi, d, c: (bi, chunk(d, c), n))

    return pl.pallas_call(
        _delta_kernel,
        grid=(b, 2, nc),
        in_specs=[sec(0), sec(1), sec(2),
                  pl.BlockSpec((1, 1, 1, C, 2 * H), lambda bi, d, c: (d, bi, chunk(d, c), 0, 0)),
                  pl.BlockSpec((1, 1, 1, 2 * H, C), lambda bi, d, c: (d, bi, chunk(d, c), 0, 0)),
                  pl.BlockSpec((1, 1, H), lambda bi, d, c: (d, 0, 0)),
                  pl.BlockSpec((1, 1, H), lambda bi, d, c: (d, 0, 0)),
                  pl.BlockSpec((1, H, 1), lambda bi, d, c: (d, 0, 0)),
                  pl.BlockSpec((1, H, 1), lambda bi, d, c: (d, 0, 0))],
        out_specs=pl.BlockSpec((1, 1, C, DN_V_W), lambda bi, d, c: (d, bi, chunk(d, c), 0)),
        out_shape=jax.ShapeDtypeStruct((2, b, s, DN_V_W), f32),
        scratch_shapes=[pltpu.VMEM((H, DN_DK, DN_DV), f32)],
        compiler_params=_cparams(("parallel", "arbitrary", "arbitrary")),
        name="delta",
    )(qkvn, qkvn, qkvn, ab, abt, alog, dtb, alogt, dtbt)


def _pool_band_tables():
    r = np.arange(POOL_TILE)[:, None]
    j = np.arange(POOL_SPAN)[None, :]
    tabs = np.zeros((3, N_POOL_GROUPS, POOL_TILE, POOL_SPAN), np.float32)
    for vi, shift in enumerate((0, 128, 256)):
        rel = j - shift - r
        for gi, w in enumerate(POOL_WINDOWS):
            tabs[vi, gi] = np.abs(rel) <= w // 2
    return jnp.asarray(tabs, dtype=bf16)


def _pool_kernel(u_ref, band_ref, w_ref, scale_ref, o_ref, *, seq):
    t = pl.program_id(1)
    t0 = pl.multiple_of(t * POOL_TILE, POOL_TILE)
    ks = pl.multiple_of(jnp.clip(t0 - 128, 0, seq - POOL_SPAN), 128)
    win = u_ref[0, pl.ds(ks, POOL_SPAN), :]
    own = u_ref[0, pl.ds(t0, POOL_TILE), :].astype(f32)
    pos = t0 + lax.broadcasted_iota(jnp.int32, (POOL_TILE, POOL_GROUP), 0)
    outs = []
    for gi, w in enumerate(POOL_WINDOWS):
        r = w // 2
        sl = slice(POOL_GROUP * gi, POOL_GROUP * (gi + 1))
        wsum = _dot(band_ref[0, gi], win[:, sl])
        count = (jnp.minimum(pos + r, seq - 1) - jnp.maximum(pos - r, 0) + 1).astype(f32)
        mixed = wsum / count - own[:, sl]
        outs.append(_dot(mixed.astype(bf16), w_ref[gi]))
    o_ref[0] = (jnp.concatenate(outs, -1) * scale_ref[...]).astype(o_ref.dtype)


def _pool(pu, pool_w, pool_scale):
    b, s, _ = pu.shape
    nt = s // POOL_TILE
    assert s % POOL_TILE == 0 and s >= POOL_SPAN

    def variant(bi, t):
        return (jnp.where(t == 0, 0, jnp.where(t == nt - 1, 2, 1)), 0, 0, 0)

    return pl.pallas_call(
        functools.partial(_pool_kernel, seq=s),
        grid=(b, nt),
        in_specs=[pl.BlockSpec((1, s, POOL_W), lambda bi, t: (bi, 0, 0)),
                  pl.BlockSpec((1, N_POOL_GROUPS, POOL_TILE, POOL_SPAN), variant),
                  _full((N_POOL_GROUPS, POOL_GROUP, POOL_GROUP)),
                  _full((1, POOL_W))],
        out_specs=pl.BlockSpec((1, POOL_TILE, POOL_W), lambda bi, t: (bi, t, 0)),
        out_shape=jax.ShapeDtypeStruct((b, s, POOL_W), bf16),
        compiler_params=_cparams(("parallel", "arbitrary")),
        name="pool",
    )(pu, _pool_band_tables(), pool_w.astype(bf16), pool_scale.reshape(1, POOL_W).astype(f32))


def _merge_kernel(oa_ref, od_ref, z_ref, op_ref, gate_ref, h_ref, ones_ref, nw_ref, wb_ref, wo_ref,
                  g_ref, b_ref, hf_ref, hb_ref):
    od = od_ref[0] + od_ref[1]
    ms = _dot_exact_rhs(od * od, ones_ref[...]) * (1.0 / DN_DV)
    od = od * lax.rsqrt(ms + RMS_EPS) * nw_ref[...] * _silu(z_ref[...].astype(f32))
    branches = (oa_ref[...], od.astype(bf16), op_ref[...])
    merged = None
    for n in range(N_BRANCH):
        up = _dot(branches[n], wb_ref[n])
        gate = _sigmoid(gate_ref[:, D_MODEL * n:D_MODEL * (n + 1)].astype(f32))
        merged = gate * up if merged is None else merged + gate * up
    y = _dot(merged.astype(bf16), wo_ref[...])
    hn = _layer_norm(DEEPNORM_ALPHA * h_ref[...] + y, g_ref[...], b_ref[...])
    hf_ref[...] = hn
    hb_ref[...] = hn.astype(bf16)


def _merge(oa, od, z, op, gates, h, dn_norm_w, w_branch, w_out, ln_g, ln_b, tm=256):
    m = h.shape[0]
    row = lambda w: pl.BlockSpec((tm, w), lambda i: (i, 0))
    nw = jnp.tile(dn_norm_w.astype(f32), DN_HEADS).reshape(1, DN_V_W)
    return pl.pallas_call(
        _merge_kernel,
        grid=(m // tm,),
        in_specs=[row(ATTN_W),
                  pl.BlockSpec((2, tm, DN_V_W), lambda i: (0, i, 0)),
                  row(DN_V_W), row(POOL_W), row(N_BRANCH * D_MODEL), row(D_MODEL),
                  _full((DN_V_W, DN_V_W)), _full((1, DN_V_W)),
                  _full((N_BRANCH, ATTN_W, D_MODEL)), _full((D_MODEL, D_MODEL)),
                  _full((1, D_MODEL)), _full((1, D_MODEL))],
        out_specs=[row(D_MODEL), row(D_MODEL)],
        out_shape=[jax.ShapeDtypeStruct((m, D_MODEL), f32), jax.ShapeDtypeStruct((m, D_MODEL), bf16)],
        compiler_params=_cparams(("parallel",)),
        name="merge",
    )(oa, od, z, op, gates, h, _head_ones(), nw, w_branch.astype(bf16), w_out.astype(bf16),
      ln_g.reshape(1, D_MODEL).astype(f32), ln_b.reshape(1, D_MODEL).astype(f32))


def _swiglu_step(x_ref, wg_ref, wu_ref, wd_ref, acc_ref):
    x = x_ref[...]
    g = _dot(x, wg_ref[0])
    u = _dot(x, wu_ref[0])
    acc_ref[...] += _dot((_silu(g) * u).astype(bf16), wd_ref[0])


def _ffn_kernel(x_ref, h_ref, wg_ref, wu_ref, wd_ref, g_ref, b_ref, hf_ref, hb_ref, acc_ref):
    j = pl.program_id(1)

    @pl.when(j == 0)
    def _():
        acc_ref[...] = jnp.zeros(acc_ref.shape, f32)

    _swiglu_step(x_ref, wg_ref, wu_ref, wd_ref, acc_ref)

    @pl.when(j == pl.num_programs(1) - 1)
    def _():
        hn = _layer_norm(DEEPNORM_ALPHA * h_ref[...] + acc_ref[...], g_ref[...], b_ref[...])
        hf_ref[...] = hn
        hb_ref[...] = hn.astype(bf16)


def _ffn(hb, hf, w1, w3, w2, ln_g, ln_b, tm=1024, tf=256):
    m = hb.shape[0]
    tm = min(tm, m)
    ff = w1.shape[-1]
    row = pl.BlockSpec((tm, D_MODEL), lambda i, j: (i, 0))
    return pl.pallas_call(
        _ffn_kernel,
        grid=(m // tm, ff // tf),
        in_specs=[row, row,
                  pl.BlockSpec((1, D_MODEL, tf), lambda i, j: (0, 0, j)),
                  pl.BlockSpec((1, D_MODEL, tf), lambda i, j: (0, 0, j)),
                  pl.BlockSpec((1, tf, D_MODEL), lambda i, j: (0, j, 0)),
                  pl.BlockSpec((1, D_MODEL), lambda i, j: (0, 0)),
                  pl.BlockSpec((1, D_MODEL), lambda i, j: (0, 0))],
        out_specs=[row, row],
        out_shape=[jax.ShapeDtypeStruct((m, D_MODEL), f32), jax.ShapeDtypeStruct((m, D_MODEL), bf16)],
        scratch_shapes=[pltpu.VMEM((tm, D_MODEL), f32)],
        compiler_params=_cparams(("parallel", "arbitrary")),
        name="ffn",
    )(hb, hf, w1.astype(bf16)[None], w3.astype(bf16)[None], w2.astype(bf16)[None],
      ln_g.reshape(1, D_MODEL).astype(f32), ln_b.reshape(1, D_MODEL).astype(f32))


def _gmm_kernel(te_ref, x_ref, wg_ref, wu_ref, wd_ref, o_ref, acc_ref):
    j = pl.program_id(1)

    @pl.when(j == 0)
    def _():
        acc_ref[...] = jnp.zeros(acc_ref.shape, f32)

    _swiglu_step(x_ref, wg_ref, wu_ref, wd_ref, acc_ref)

    @pl.when(j == pl.num_programs(1) - 1)
    def _():
        o_ref[...] = acc_ref[...].astype(o_ref.dtype)


def _gmm(xs, tile_expert, wg, wu, wd, tm, tf=512):
    n = xs.shape[0]
    ff = wg.shape[-1]
    grid_spec = pltpu.PrefetchScalarGridSpec(
        num_scalar_prefetch=1,
        grid=(n // tm, ff // tf),
        in_specs=[pl.BlockSpec((tm, D_MODEL), lambda i, j, te: (i, 0)),
                  pl.BlockSpec((1, D_MODEL, tf), lambda i, j, te: (te[i], 0, j)),
                  pl.BlockSpec((1, D_MODEL, tf), lambda i, j, te: (te[i], 0, j)),
                  pl.BlockSpec((1, tf, D_MODEL), lambda i, j, te: (te[i], j, 0))],
        out_specs=pl.BlockSpec((tm, D_MODEL), lambda i, j, te: (i, 0)),
        scratch_shapes=[pltpu.VMEM((tm, D_MODEL), f32)])
    return pl.pallas_call(
        _gmm_kernel,
        grid_spec=grid_spec,
        out_shape=jax.ShapeDtypeStruct((n, D_MODEL), f32),
        compiler_params=_cparams(("parallel", "arbitrary")),
        name="moe_gmm",
    )(tile_expert, xs, wg, wu, wd)


def _router_kernel(x_ref, w_ref, comb_ref):
    logits = _dot_f32(x_ref[...], w_ref[...])
    lane = lax.broadcasted_iota(jnp.int32, logits.shape, 1)
    m1 = jnp.max(logits, -1, keepdims=True)
    i1 = jnp.min(jnp.where(logits == m1, lane, N_EXPERTS), -1, keepdims=True)
    rest = jnp.where(lane == i1, -jnp.inf, logits)
    m2 = jnp.max(rest, -1, keepdims=True)
    i2 = jnp.min(jnp.where(rest == m2, lane, N_EXPERTS), -1, keepdims=True)
    e2 = jnp.exp(m2 - m1)
    p1 = 1.0 / (1.0 + e2)
    p2 = e2 / (1.0 + e2)
    comb_ref[...] = jnp.where(lane == i1, p1, 0.0) + jnp.where(lane == i2, p2, 0.0)


def _router(hf, router_w, tm=512):
    m = hf.shape[0]
    tm = min(tm, m)
    return pl.pallas_call(
        _router_kernel,
        grid=(m // tm,),
        in_specs=[pl.BlockSpec((tm, D_MODEL), lambda i: (i, 0)), _full((D_MODEL, N_EXPERTS))],
        out_specs=pl.BlockSpec((tm, N_EXPERTS), lambda i: (i, 0)),
        out_shape=jax.ShapeDtypeStruct((m, N_EXPERTS), f32),
        compiler_params=_cparams(("parallel",)),
        name="router",
    )(hf, router_w.astype(f32))


def _combine_kernel(ys_ref, comb_ref, h_ref, g_ref, b_ref, o_ref, acc_ref):
    e = pl.program_id(1)

    @pl.when(e == 0)
    def _():
        acc_ref[...] = jnp.zeros(acc_ref.shape, f32)

    lane = lax.broadcasted_iota(jnp.int32, comb_ref.shape, 1)
    wcol = jnp.sum(jnp.where(lane == e, comb_ref[...], 0.0), -1, keepdims=True)
    acc_ref[...] += wcol * ys_ref[0]

    @pl.when(e == pl.num_programs(1) - 1)
    def _():
        o_ref[...] = _layer_norm(DEEPNORM_ALPHA * h_ref[...] + acc_ref[...], g_ref[...], b_ref[...])


def _combine(ys, comb, hf, ln_g, ln_b, tm=512):
    m = hf.shape[0]
    tm = min(tm, m)
    return pl.pallas_call(
        _combine_kernel,
        grid=(m // tm, N_EXPERTS),
        in_specs=[pl.BlockSpec((1, tm, D_MODEL), lambda i, e: (e, i, 0)),
                  pl.BlockSpec((tm, N_EXPERTS), lambda i, e: (i, 0)),
                  pl.BlockSpec((tm, D_MODEL), lambda i, e: (i, 0)),
                  pl.BlockSpec((1, D_MODEL), lambda i, e: (0, 0)),
                  pl.BlockSpec((1, D_MODEL), lambda i, e: (0, 0))],
        out_specs=pl.BlockSpec((tm, D_MODEL), lambda i, e: (i, 0)),
        out_shape=jax.ShapeDtypeStruct((m, D_MODEL), f32),
        scratch_shapes=[pltpu.VMEM((tm, D_MODEL), f32)],
        compiler_params=_cparams(("parallel", "arbitrary")),
        name="moe_combine",
    )(ys, comb, hf, ln_g.reshape(1, D_MODEL).astype(f32), ln_b.reshape(1, D_MODEL).astype(f32))


def _moe(hb, hf, router_w, wg, wu, wd, ln_g, ln_b):
    m = hb.shape[0]
    comb = _router(hf, router_w)
    tm = min(1024, m)
    tiles = m // tm
    xs = jnp.tile(hb, (N_EXPERTS, 1))
    tile_expert = jnp.repeat(jnp.arange(N_EXPERTS, dtype=jnp.int32), tiles)
    ys = _gmm(xs, tile_expert, wg.astype(bf16), wu.astype(bf16), wd.astype(bf16), tm)
    return _combine(ys.reshape(N_EXPERTS, m, D_MODEL), comb, hf, ln_g, ln_b)


_SPLITS = (ATTN_W, 2 * KV_W, 3 * DN_QK_W, DN_V_W, 4 * DN_HEADS, POOL_W, N_BRANCH * D_MODEL)
_SPLIT_DTYPES = (bf16, bf16, f32, bf16, f32, bf16, bf16)


def _mixer(hf, hb, bsz, seq, bias_tabs, w_in, sink, conv_w, a_log, dt_bias, dn_norm_w, pool_w, pool_scale,
           w_branch, w_out, ln_g, ln_b):
    points = np.cumsum(_SPLITS)[:-1].tolist()
    ws = jnp.split(w_in.astype(bf16), points, axis=-1)
    aq, akv, dqkv, dz, dab, pu, gates = _inproj(hb, ws, _SPLIT_DTYPES)
    shp = lambda t: t.reshape(bsz, seq, t.shape[-1])
    oa = _attention(shp(aq), shp(akv), bias_tabs, sink)
    qkvn = _dn_prep(shp(dqkv), conv_w)
    od = _delta(qkvn, shp(dab), a_log, dt_bias)
    op = _pool(shp(pu), pool_w, pool_scale)
    m = bsz * seq
    return _merge(oa.reshape(m, ATTN_W), od.reshape(2, m, DN_V_W), dz, op.reshape(m, POOL_W), gates, hf,
                  dn_norm_w, w_branch, w_out, ln_g, ln_b)


def kernel(x, w_in, attn_sink, rel_bias, conv_w, a_log, dt_bias, dn_norm_w, pool_w, pool_scale, w_branch, w_out,
           ln1_g, ln1_b, ln2_g, ln2_b, ffn_w1, ffn_w3, ffn_w2, router_w, moe_wg, moe_wu, moe_wd):
    bsz, seq, _ = x.shape
    m = bsz * seq
    bias_tabs = _attn_bias_tables(rel_bias)
    hf = x.reshape(m, D_MODEL).astype(f32)
    hb = hf.astype(bf16)
    for layer in range(DEPTH):
        hf, hb = _mixer(hf, hb, bsz, seq, bias_tabs, w_in[layer], attn_sink[layer], conv_w[layer], a_log[layer],
                        dt_bias[layer], dn_norm_w[layer], pool_w[layer], pool_scale[layer], w_branch[layer],
                        w_out[layer], ln1_g[layer], ln1_b[layer])
        i = layer // 2
        if layer % 2 == 0:
            hf, hb = _ffn(hb, hf, ffn_w1[i], ffn_w3[i], ffn_w2[i], ln2_g[layer], ln2_b[layer])
        else:
            hf = _moe(hb, hf, router_w[i], moe_wg[i], moe_wu[i], moe_wd[i], ln2_g[layer], ln2_b[layer])
    return hf.reshape(bsz, seq, D_MODEL).astype(x.dtype)
```

```python
import functools

import numpy as np
import jax
import jax.numpy as jnp
from jax import lax
from jax.experimental import pallas as pl
from jax.experimental.pallas import tpu as pltpu

f32 = jnp.float32
bf16 = jnp.bfloat16

D_MODEL = 1024
DEPTH = 2
HEAD_DIM = 64
ATTN_Q_HEADS = 8
ATTN_KV_HEADS = 2
ATTN_GROUP = ATTN_Q_HEADS // ATTN_KV_HEADS
WINDOW = 128
BLOCK = 128
SPAN = BLOCK + 2 * WINDOW
N_BUCKETS = 32
MAX_DISTANCE = 128
DN_HEADS = 8
DN_DK = 64
DN_DV = 64
DN_CONV = 5
DN_CHUNK = 64
POOL_WINDOWS = (2, 4, 8, 16)
N_POOL_GROUPS = 4
POOL_GROUP = 128
ATTN_W = ATTN_Q_HEADS * HEAD_DIM
KV_W = ATTN_KV_HEADS * HEAD_DIM
DN_QK_W = DN_HEADS * DN_DK
DN_V_W = DN_HEADS * DN_DV
POOL_W = N_POOL_GROUPS * POOL_GROUP
N_BRANCH = 3
N_EXPERTS = 8
TOP_K = 2
DEEPNORM_ALPHA = (2.0 * DEPTH) ** 0.25
LN_EPS = 1e-5
RMS_EPS = 1e-6
NEG_INF = -1e30

VMEM_LIMIT_BYTES = 56 * 1024 * 1024
POOL_TILE = 256
POOL_SPAN = 512
CONV_HALO = 8


def _cparams(sem):
    return pltpu.CompilerParams(dimension_semantics=sem, vmem_limit_bytes=VMEM_LIMIT_BYTES)


def _full(shape):
    n = len(shape)
    return pl.BlockSpec(shape, lambda *_: (0,) * n)


def _sigmoid(x):
    return 1.0 / (1.0 + jnp.exp(-x))


def _silu(x):
    return x * _sigmoid(x)


def _layer_norm(x, g, b):
    mu = jnp.mean(x, -1, keepdims=True)
    xc = x - mu
    var = jnp.mean(xc * xc, -1, keepdims=True)
    return xc * lax.rsqrt(var + LN_EPS) * g + b


def _dot(a, b):
    return jnp.dot(a, b, preferred_element_type=f32)


def _dot_nt(a, b):
    return lax.dot_general(a, b, (((1,), (1,)), ((), ())), preferred_element_type=f32)


def _dot_tn(a, b):
    return lax.dot_general(a, b, (((0,), (0,)), ((), ())), preferred_element_type=f32)


def _split3(x):
    hi = x.astype(bf16)
    r = x - hi.astype(f32)
    mid = r.astype(bf16)
    lo = (r - mid.astype(f32)).astype(bf16)
    return hi, mid, lo


def _dot_exact_lhs(a_bf, x):
    hi, mid, lo = _split3(x)
    return _dot(a_bf, hi) + _dot(a_bf, mid) + _dot(a_bf, lo)


def _dot_exact_rhs(x, b_bf):
    hi, mid, lo = _split3(x)
    return _dot(hi, b_bf) + _dot(mid, b_bf) + _dot(lo, b_bf)


def _dot_f32(a, b):
    return jnp.dot(a, b, preferred_element_type=f32, precision=lax.Precision.HIGHEST)


def _inproj_kernel(x_ref, *refs):
    n = len(refs) // 2
    x = x_ref[...]
    for w_ref, o_ref in zip(refs[:n], refs[n:]):
        o_ref[...] = _dot(x, w_ref[...]).astype(o_ref.dtype)


def _inproj(xb, ws, dtypes, tm=256):
    m = xb.shape[0]
    return pl.pallas_call(
        _inproj_kernel,
        grid=(m // tm,),
        in_specs=[pl.BlockSpec((tm, D_MODEL), lambda i: (i, 0))]
        + [pl.BlockSpec(w.shape, lambda i: (0, 0), pipeline_mode=pl.Buffered(1)) for w in ws],
        out_specs=[pl.BlockSpec((tm, w.shape[1]), lambda i: (i, 0)) for w in ws],
        out_shape=[jax.ShapeDtypeStruct((m, w.shape[1]), dt) for w, dt in zip(ws, dtypes)],
        compiler_params=_cparams(("parallel",)),
        name="inproj",
    )(xb, *ws)


def _t5_bucket(rel):
    nb = N_BUCKETS // 2
    max_exact = nb // 2
    n = np.abs(rel)
    large = max_exact + (np.log(np.maximum(n, 1) / max_exact) / np.log(MAX_DISTANCE / max_exact)
                         * (nb - max_exact)).astype(np.int32)
    large = np.minimum(large, nb - 1)
    return (np.where(rel > 0, nb, 0) + np.where(n < max_exact, n, large)).astype(np.int32)


def _attn_bias_tables(rel_bias):
    r = np.arange(BLOCK)[:, None]
    j = np.arange(SPAN)[None, :]
    tabs = []
    for shift in (0, WINDOW, 2 * WINDOW):
        rel = j - shift - r
        b = jnp.transpose(rel_bias[_t5_bucket(rel)], (2, 0, 1)).astype(f32)
        tabs.append(jnp.where(jnp.asarray(np.abs(rel) <= WINDOW)[None], b, NEG_INF))
    return jnp.stack(tabs)


def _attn_kernel(q_ref, kv_ref, bias_ref, sink_ref, o_ref, *, seq):
    i = pl.program_id(1)
    ks = pl.multiple_of(jnp.clip(i * BLOCK - WINDOW, 0, seq - SPAN), BLOCK)
    kv = kv_ref[0, pl.ds(ks, SPAN), :]
    q = q_ref[0]
    scale = HEAD_DIM ** -0.5
    outs = []
    for h in range(ATTN_Q_HEADS):
        g = h // ATTN_GROUP
        qh = q[:, HEAD_DIM * h:HEAD_DIM * (h + 1)]
        kh = kv[:, HEAD_DIM * g:HEAD_DIM * (g + 1)]
        vh = kv[:, KV_W + HEAD_DIM * g:KV_W + HEAD_DIM * (g + 1)]
        s = _dot_nt(qh, kh) * scale + bias_ref[0, h]
        sk = sink_ref[0:1, h:h + 1]
        m = jnp.maximum(jnp.max(s, -1, keepdims=True), sk)
        p = jnp.exp(s - m)
        denom = jnp.sum(p, -1, keepdims=True) + jnp.exp(sk - m)
        outs.append(_dot(p.astype(bf16), vh) / denom)
    o_ref[0] = jnp.concatenate(outs, -1).astype(o_ref.dtype)


def _attention(aq, akv, bias_tabs, sink):
    b, s, _ = aq.shape
    nb = s // BLOCK
    assert s % BLOCK == 0 and s >= SPAN

    def variant(bi, i):
        return (jnp.where(i == 0, 0, jnp.where(i == nb - 1, 2, 1)), 0, 0, 0)

    return pl.pallas_call(
        functools.partial(_attn_kernel, seq=s),
        grid=(b, nb),
        in_specs=[pl.BlockSpec((1, BLOCK, ATTN_W), lambda bi, i: (bi, i, 0)),
                  pl.BlockSpec((1, s, 2 * KV_W), lambda bi, i: (bi, 0, 0)),
                  pl.BlockSpec((1, ATTN_Q_HEADS, BLOCK, SPAN), variant),
                  _full((1, ATTN_Q_HEADS))],
        out_specs=pl.BlockSpec((1, BLOCK, ATTN_W), lambda bi, i: (bi, i, 0)),
        out_shape=jax.ShapeDtypeStruct((b, s, ATTN_W), bf16),
        compiler_params=_cparams(("parallel", "arbitrary")),
        name="attn",
    )(aq, akv, bias_tabs, sink.reshape(1, ATTN_Q_HEADS).astype(f32))


def _dn_prep_kernel(u_ref, w_ref, ones_ref, o_ref, pad_ref, *, seq):
    sec = pl.program_id(1)
    zeros = jnp.zeros((CONV_HALO, DN_QK_W), f32)
    pad_ref[0:CONV_HALO, :] = zeros
    pad_ref[seq + CONV_HALO:seq + 2 * CONV_HALO, :] = zeros
    pad_ref[CONV_HALO:seq + CONV_HALO, :] = u_ref[0].astype(f32)
    reach = (DN_CONV - 1) // 2
    acc = None
    for k in range(DN_CONV):
        term = pad_ref[CONV_HALO - reach + k:CONV_HALO - reach + k + seq, :] * w_ref[0, k:k + 1, :]
        acc = term if acc is None else acc + term
    y = _silu(acc)

    @pl.when(sec < 2)
    def _():
        ss = _dot_exact_rhs(y * y, ones_ref[...])
        scale = jnp.where(sec == 0, DN_DK ** -0.5, 1.0)
        o_ref[0] = y * lax.rsqrt(ss + 1e-6) * scale

    @pl.when(sec == 2)
    def _():
        o_ref[0] = y


def _head_ones():
    blk = np.arange(DN_QK_W) // DN_DK
    return jnp.asarray(blk[:, None] == blk[None, :], dtype=bf16)


def _dn_prep(dqkv, conv_w):
    b, s, _ = dqkv.shape
    w3 = jnp.transpose(conv_w.reshape(DN_CONV, 3, DN_QK_W), (1, 0, 2)).astype(f32)
    return pl.pallas_call(
        functools.partial(_dn_prep_kernel, seq=s),
        grid=(b, 3),
        in_specs=[pl.BlockSpec((1, s, DN_QK_W), lambda bi, c: (bi, 0, c)),
                  pl.BlockSpec((1, DN_CONV, DN_QK_W), lambda bi, c: (c, 0, 0)),
                  _full((DN_QK_W, DN_QK_W))],
        out_specs=pl.BlockSpec((1, s, DN_QK_W), lambda bi, c: (bi, 0, c)),
        out_shape=jax.ShapeDtypeStruct(dqkv.shape, f32),
        scratch_shapes=[pltpu.VMEM((s + 2 * CONV_HALO, DN_QK_W), f32)],
        compiler_params=_cparams(("parallel", "arbitrary")),
        name="dn_prep",
    )(dqkv, w3, _head_ones())


def _softplus(x):
    return jnp.maximum(x, 0.0) + jnp.log1p(jnp.exp(-jnp.abs(x)))


QUAD = 4
QUAD_W = QUAD * DN_DK
N_QUADS = DN_HEADS // QUAD
DELTA_BLOCK_CHUNKS = 4


def _block_diag(xq):
    lane_head = lax.broadcasted_iota(jnp.int32, xq.shape, 1) // DN_DK
    return jnp.concatenate([jnp.where(lane_head == h, xq, 0.0).astype(bf16) for h in range(QUAD)], axis=0)


def _delta_chunk(q, k, v, gx, bx, d):
    C = DN_CHUNK
    W = DN_QK_W
    sign = 1 - 2 * d
    ii = lax.broadcasted_iota(jnp.int32, (C, W), 0)
    jl = lax.broadcasted_iota(jnp.int32, (C, W), 1) % DN_DK
    order = (ii - jl) * sign
    incl = order >= 0
    strict = order > 0
    rr = lax.broadcasted_iota(jnp.int32, (2 * C, 2 * C), 0)
    cc = lax.broadcasted_iota(jnp.int32, (2 * C, 2 * C), 1)
    tri = jnp.where(((rr % C) - cc) * sign >= 0, 1.0, 0.0)
    lmat = jnp.where(cc < C, tri, jnp.where(rr < C, -1.0, 0.0)).astype(bf16)
    rmat = jnp.concatenate([gx, jnp.where(order <= 0, gx, 0.0)], axis=0)
    dg = None
    for part in _split3(rmat):
        term = _dot(lmat, part)
        dg = term if dg is None else dg + term
    dmat, gc = dg[:C], dg[C:]
    decay = jnp.where(incl, jnp.exp(jnp.where(incl, dmat, 0.0)), 0.0)
    eg = jnp.exp(gc)
    last = C - 1 if d == 0 else 0
    gl = gc[last:last + 1, :]
    kdec = k * jnp.exp(gl - gc)
    cd = jnp.exp(gl)
    kb = k * bx
    eye = jnp.where(ii == jl, 1.0, 0.0)
    return dict(q=q, k=k, bx=bx, decay=decay, strict=strict, eye=eye, kdec=kdec, cd=cd,
                vb=v * bx, kbe=kb * eg, qdec=q * eg)


def _delta_kernel(qf_ref, kf_ref, vf_ref, qb_ref, kb_ref, vb_ref, abf_ref, abb_ref, alog_ref, dtb_ref,
                  of_ref, ob_ref, state_ref):
    C, H = DN_CHUNK, DN_HEADS

    @pl.when(pl.program_id(1) == 0)
    def _():
        state_ref[...] = jnp.zeros(state_ref.shape, f32)

    def expand(x, parts):
        rows = lax.broadcasted_iota(jnp.int32, (H * parts, DN_QK_W), 0) % H
        cols = lax.broadcasted_iota(jnp.int32, (H * parts, DN_QK_W), 1) // DN_DK
        emat = jnp.where(rows == cols, 1.0, 0.0).astype(bf16)
        pieces = [p.astype(f32) for p in _split3(x)[:parts]]
        return _dot(jnp.concatenate(pieces, axis=-1).astype(bf16), emat)

    chains = []
    dirs = ((0, qf_ref, kf_ref, vf_ref, abf_ref), (1, qb_ref, kb_ref, vb_ref, abb_ref))
    for d, q_ref, k_ref, v_ref, ab_ref in dirs:
        ab = ab_ref[0, 0]
        g = -jnp.exp(alog_ref[d]) * _softplus(ab[:, :H] + dtb_ref[d])
        gx = expand(g, 3)
        bx = expand(_sigmoid(ab[:, H:]), 2)
        for c in range(DELTA_BLOCK_CHUNKS):
            rows = slice(C * c, C * (c + 1))
            pre = _delta_chunk(q_ref[0, rows, :], k_ref[0, rows, :], v_ref[0, rows, :], gx[rows], bx[rows], d)
            for qd in range(N_QUADS):
                sl = slice(QUAD_W * qd, QUAD_W * (qd + 1))
                chains.append(dict(d=d, c=c, qd=qd, **{n: x[:, sl] for n, x in pre.items()}))

    for ch in chains:
        kq = _dot_nt(jnp.concatenate([ch["k"], ch["q"]], axis=0).astype(bf16), _block_diag(ch["k"]))
        a_mat = jnp.where(ch["strict"], kq[:C] * ch["bx"] * ch["decay"], 0.0)
        ch["qk"] = kq[C:] * ch["decay"]
        ch["p"] = -a_mat
        ch["t"] = ch["eye"] - a_mat
    for ch in chains:
        ch["p"] = _dot(ch["p"].astype(bf16), _block_diag(ch["p"]))
    for _ in range(4):
        for ch in chains:
            pt = _dot(jnp.concatenate([ch["p"], ch["t"]], axis=0).astype(bf16), _block_diag(ch["p"]))
            ch["p"], ch["t"] = pt[:C], ch["t"] + pt[C:]
    for ch in chains:
        ch["tb"] = (ch["t"] + _dot(ch["t"].astype(bf16), _block_diag(ch["p"]))).astype(bf16)
    for ch in chains:
        ch["u"] = _dot(ch["tb"], _block_diag(ch["vb"]))
        ch["w"] = _dot(ch["tb"], _block_diag(ch["kbe"]))

    row_head = lax.broadcasted_iota(jnp.int32, (QUAD_W, QUAD_W), 0) // DN_DK
    col_head = lax.broadcasted_iota(jnp.int32, (QUAD_W, QUAD_W), 1) // DN_DK
    o_refs = (of_ref, ob_ref)
    for step in range(DELTA_BLOCK_CHUNKS):
        for ch in chains:
            d, c, qd = ch["d"], ch["c"], ch["qd"]
            if c != (step if d == 0 else DELTA_BLOCK_CHUNKS - 1 - step):
                continue
            st = state_ref[d, qd]
            wq = _dot(jnp.concatenate([ch["w"], ch["qdec"]], axis=0).astype(bf16), st.astype(bf16))
            v_new = ch["u"] - wq[:C]
            o = wq[C:] + _dot(ch["qk"].astype(bf16), _block_diag(v_new))
            o_refs[d][0, C * c:C * (c + 1), QUAD_W * qd:QUAD_W * (qd + 1)] = o
            s_new = _dot_tn(ch["kdec"].astype(bf16), v_new.astype(bf16))
            state_ref[d, qd] = st * ch["cd"] + jnp.where(row_head == col_head, s_new, 0.0)


def _delta(qkvn, dab, a_log, dt_bias):
    b, s, _ = qkvn.shape
    H = DN_HEADS
    tb = DN_CHUNK * DELTA_BLOCK_CHUNKS
    nblk = s // tb
    assert s % tb == 0
    d4 = dab.reshape(b, s, 4, H)
    ab = jnp.stack([jnp.concatenate([d4[:, :, r], d4[:, :, 2 + r]], -1) for r in range(2)])
    alog = a_log.astype(f32).reshape(2, 1, H)
    dtb = dt_bias.astype(f32).reshape(2, 1, H)

    def sec(n, rev):
        if rev:
            return pl.BlockSpec((1, tb, DN_QK_W), lambda bi, i: (bi, nblk - 1 - i, n))
        return pl.BlockSpec((1, tb, DN_QK_W), lambda bi, i: (bi, i, n))

    return pl.pallas_call(
        _delta_kernel,
        grid=(b, nblk),
        in_specs=[sec(0, False), sec(1, False), sec(2, False), sec(0, True), sec(1, True), sec(2, True),
                  pl.BlockSpec((1, 1, tb, 2 * H), lambda bi, i: (0, bi, i, 0)),
                  pl.BlockSpec((1, 1, tb, 2 * H), lambda bi, i: (1, bi, nblk - 1 - i, 0)),
                  _full((2, 1, H)), _full((2, 1, H))],
        out_specs=[sec(0, False), sec(0, True)],
        out_shape=[jax.ShapeDtypeStruct((b, s, DN_V_W), f32)] * 2,
        scratch_shapes=[pltpu.VMEM((2, N_QUADS, QUAD_W, QUAD_W), f32)],
        compiler_params=_cparams(("parallel", "arbitrary")),
        name="delta",
    )(qkvn, qkvn, qkvn, qkvn, qkvn, qkvn, ab, ab, alog, dtb)


def _pool_band_tables():
    r = np.arange(POOL_TILE)[:, None]
    j = np.arange(POOL_SPAN)[None, :]
    tabs = np.zeros((3, N_POOL_GROUPS, POOL_TILE, POOL_SPAN), np.float32)
    for vi, shift in enumerate((0, 128, 256)):
        rel = j - shift - r
        for gi, w in enumerate(POOL_WINDOWS):
            tabs[vi, gi] = np.abs(rel) <= w // 2
    return jnp.asarray(tabs, dtype=bf16)


def _pool_kernel(u_ref, band_ref, w_ref, scale_ref, o_ref, *, seq):
    t = pl.program_id(1)
    t0 = pl.multiple_of(t * POOL_TILE, POOL_TILE)
    ks = pl.multiple_of(jnp.clip(t0 - 128, 0, seq - POOL_SPAN), 128)
    win = u_ref[0, pl.ds(ks, POOL_SPAN), :]
    own = u_ref[0, pl.ds(t0, POOL_TILE), :].astype(f32)
    pos = t0 + lax.broadcasted_iota(jnp.int32, (POOL_TILE, POOL_GROUP), 0)
    outs = []
    for gi, w in enumerate(POOL_WINDOWS):
        r = w // 2
        sl = slice(POOL_GROUP * gi, POOL_GROUP * (gi + 1))
        wsum = _dot(band_ref[0, gi], win[:, sl])
        count = (jnp.minimum(pos + r, seq - 1) - jnp.maximum(pos - r, 0) + 1).astype(f32)
        mixed = wsum / count - own[:, sl]
        outs.append(_dot(mixed.astype(bf16), w_ref[gi]))
    o_ref[0] = (jnp.concatenate(outs, -1) * scale_ref[...]).astype(o_ref.dtype)


def _pool(pu, pool_w, pool_scale):
    b, s, _ = pu.shape
    nt = s // POOL_TILE
    assert s % POOL_TILE == 0 and s >= POOL_SPAN

    def variant(bi, t):
        return (jnp.where(t == 0, 0, jnp.where(t == nt - 1, 2, 1)), 0, 0, 0)

    return pl.pallas_call(
        functools.partial(_pool_kernel, seq=s),
        grid=(b, nt),
        in_specs=[pl.BlockSpec((1, s, POOL_W), lambda bi, t: (bi, 0, 0)),
                  pl.BlockSpec((1, N_POOL_GROUPS, POOL_TILE, POOL_SPAN), variant),
                  _full((N_POOL_GROUPS, POOL_GROUP, POOL_GROUP)),
                  _full((1, POOL_W))],
        out_specs=pl.BlockSpec((1, POOL_TILE, POOL_W), lambda bi, t: (bi, t, 0)),
        out_shape=jax.ShapeDtypeStruct((b, s, POOL_W), bf16),
        compiler_params=_cparams(("parallel", "arbitrary")),
        name="pool",
    )(pu, _pool_band_tables(), pool_w.astype(bf16), pool_scale.reshape(1, POOL_W).astype(f32))


def _merge_kernel(oa_ref, odf_ref, odb_ref, z_ref, op_ref, gate_ref, h_ref, ones_ref, nw_ref, wb_ref, wo_ref,
                  g_ref, b_ref, hf_ref, hb_ref):
    od = odf_ref[...] + odb_ref[...]
    ms = _dot_exact_rhs(od * od, ones_ref[...]) * (1.0 / DN_DV)
    od = od * lax.rsqrt(ms + RMS_EPS) * nw_ref[...] * _silu(z_ref[...].astype(f32))
    branches = (oa_ref[...], od.astype(bf16), op_ref[...])
    merged = None
    for n in range(N_BRANCH):
        up = _dot(branches[n], wb_ref[n])
        gate = _sigmoid(gate_ref[:, D_MODEL * n:D_MODEL * (n + 1)].astype(f32))
        merged = gate * up if merged is None else merged + gate * up
    y = _dot(merged.astype(bf16), wo_ref[...])
    hn = _layer_norm(DEEPNORM_ALPHA * h_ref[...] + y, g_ref[...], b_ref[...])
    hf_ref[...] = hn
    hb_ref[...] = hn.astype(bf16)


def _merge(oa, odf, odb, z, op, gates, h, dn_norm_w, w_branch, w_out, ln_g, ln_b, tm=256):
    m = h.shape[0]
    row = lambda w: pl.BlockSpec((tm, w), lambda i: (i, 0))
    nw = jnp.tile(dn_norm_w.astype(f32), DN_HEADS).reshape(1, DN_V_W)
    return pl.pallas_call(
        _merge_kernel,
        grid=(m // tm,),
        in_specs=[row(ATTN_W), row(DN_V_W), row(DN_V_W), row(DN_V_W), row(POOL_W),
                  row(N_BRANCH * D_MODEL), row(D_MODEL),
                  _full((DN_V_W, DN_V_W)), _full((1, DN_V_W)),
                  _full((N_BRANCH, ATTN_W, D_MODEL)), _full((D_MODEL, D_MODEL)),
                  _full((1, D_MODEL)), _full((1, D_MODEL))],
        out_specs=[row(D_MODEL), row(D_MODEL)],
        out_shape=[jax.ShapeDtypeStruct((m, D_MODEL), f32), jax.ShapeDtypeStruct((m, D_MODEL), bf16)],
        compiler_params=_cparams(("parallel",)),
        name="merge",
    )(oa, odf, odb, z, op, gates, h, _head_ones(), nw, w_branch.astype(bf16), w_out.astype(bf16),
      ln_g.reshape(1, D_MODEL).astype(f32), ln_b.reshape(1, D_MODEL).astype(f32))


def _swiglu_step(x_ref, wg_ref, wu_ref, wd_ref, acc_ref):
    x = x_ref[...]
    g = _dot(x, wg_ref[0])
    u = _dot(x, wu_ref[0])
    acc_ref[...] += _dot((_silu(g) * u).astype(bf16), wd_ref[0])


def _ffn_kernel(x_ref, h_ref, wg_ref, wu_ref, wd_ref, g_ref, b_ref, hf_ref, hb_ref, acc_ref):
    j = pl.program_id(1)

    @pl.when(j == 0)
    def _():
        acc_ref[...] = jnp.zeros(acc_ref.shape, f32)

    _swiglu_step(x_ref, wg_ref, wu_ref, wd_ref, acc_ref)

    @pl.when(j == pl.num_programs(1) - 1)
    def _():
        hn = _layer_norm(DEEPNORM_ALPHA * h_ref[...] + acc_ref[...], g_ref[...], b_ref[...])
        hf_ref[...] = hn
        hb_ref[...] = hn.astype(bf16)


def _ffn(hb, hf, w1, w3, w2, ln_g, ln_b, tm=1024, tf=256):
    m = hb.shape[0]
    tm = min(tm, m)
    ff = w1.shape[-1]
    row = pl.BlockSpec((tm, D_MODEL), lambda i, j: (i, 0))
    return pl.pallas_call(
        _ffn_kernel,
        grid=(m // tm, ff // tf),
        in_specs=[row, row,
                  pl.BlockSpec((1, D_MODEL, tf), lambda i, j: (0, 0, j)),
                  pl.BlockSpec((1, D_MODEL, tf), lambda i, j: (0, 0, j)),
                  pl.BlockSpec((1, tf, D_MODEL), lambda i, j: (0, j, 0)),
                  pl.BlockSpec((1, D_MODEL), lambda i, j: (0, 0)),
                  pl.BlockSpec((1, D_MODEL), lambda i, j: (0, 0))],
        out_specs=[row, row],
        out_shape=[jax.ShapeDtypeStruct((m, D_MODEL), f32), jax.ShapeDtypeStruct((m, D_MODEL), bf16)],
        scratch_shapes=[pltpu.VMEM((tm, D_MODEL), f32)],
        compiler_params=_cparams(("parallel", "arbitrary")),
        name="ffn",
    )(hb, hf, w1.astype(bf16)[None], w3.astype(bf16)[None], w2.astype(bf16)[None],
      ln_g.reshape(1, D_MODEL).astype(f32), ln_b.reshape(1, D_MODEL).astype(f32))


def _gmm_kernel(te_ref, x_ref, wg_ref, wu_ref, wd_ref, o_ref, acc_ref):
    j = pl.program_id(1)

    @pl.when(j == 0)
    def _():
        acc_ref[...] = jnp.zeros(acc_ref.shape, f32)

    _swiglu_step(x_ref, wg_ref, wu_ref, wd_ref, acc_ref)

    @pl.when(j == pl.num_programs(1) - 1)
    def _():
        o_ref[...] = acc_ref[...].astype(o_ref.dtype)


def _gmm(xs, tile_expert, wg, wu, wd, tm, tf=512):
    n = xs.shape[0]
    ff = wg.shape[-1]
    grid_spec = pltpu.PrefetchScalarGridSpec(
        num_scalar_prefetch=1,
        grid=(n // tm, ff // tf),
        in_specs=[pl.BlockSpec((tm, D_MODEL), lambda i, j, te: (i, 0)),
                  pl.BlockSpec((1, D_MODEL, tf), lambda i, j, te: (te[i], 0, j)),
                  pl.BlockSpec((1, D_MODEL, tf), lambda i, j, te: (te[i], 0, j)),
                  pl.BlockSpec((1, tf, D_MODEL), lambda i, j, te: (te[i], j, 0))],
        out_specs=pl.BlockSpec((tm, D_MODEL), lambda i, j, te: (i, 0)),
        scratch_shapes=[pltpu.VMEM((tm, D_MODEL), f32)])
    return pl.pallas_call(
        _gmm_kernel,
        grid_spec=grid_spec,
        out_shape=jax.ShapeDtypeStruct((n, D_MODEL), f32),
        compiler_params=_cparams(("parallel", "arbitrary")),
        name="moe_gmm",
    )(tile_expert, xs, wg, wu, wd)


def _router_kernel(x_ref, w_ref, comb_ref):
    logits = _dot_f32(x_ref[...], w_ref[...])
    lane = lax.broadcasted_iota(jnp.int32, logits.shape, 1)
    m1 = jnp.max(logits, -1, keepdims=True)
    i1 = jnp.min(jnp.where(logits == m1, lane, N_EXPERTS), -1, keepdims=True)
    rest = jnp.where(lane == i1, -jnp.inf, logits)
    m2 = jnp.max(rest, -1, keepdims=True)
    i2 = jnp.min(jnp.where(rest == m2, lane, N_EXPERTS), -1, keepdims=True)
    e2 = jnp.exp(m2 - m1)
    p1 = 1.0 / (1.0 + e2)
    p2 = e2 / (1.0 + e2)
    comb_ref[...] = jnp.where(lane == i1, p1, 0.0) + jnp.where(lane == i2, p2, 0.0)


def _router(hf, router_w, tm=512):
    m = hf.shape[0]
    tm = min(tm, m)
    return pl.pallas_call(
        _router_kernel,
        grid=(m // tm,),
        in_specs=[pl.BlockSpec((tm, D_MODEL), lambda i: (i, 0)), _full((D_MODEL, N_EXPERTS))],
        out_specs=pl.BlockSpec((tm, N_EXPERTS), lambda i: (i, 0)),
        out_shape=jax.ShapeDtypeStruct((m, N_EXPERTS), f32),
        compiler_params=_cparams(("parallel",)),
        name="router",
    )(hf, router_w.astype(f32))


def _combine_kernel(ys_ref, comb_ref, h_ref, g_ref, b_ref, o_ref, acc_ref):
    e = pl.program_id(1)

    @pl.when(e == 0)
    def _():
        acc_ref[...] = jnp.zeros(acc_ref.shape, f32)

    lane = lax.broadcasted_iota(jnp.int32, comb_ref.shape, 1)
    wcol = jnp.sum(jnp.where(lane == e, comb_ref[...], 0.0), -1, keepdims=True)
    acc_ref[...] += wcol * ys_ref[0]

    @pl.when(e == pl.num_programs(1) - 1)
    def _():
        o_ref[...] = _layer_norm(DEEPNORM_ALPHA * h_ref[...] + acc_ref[...], g_ref[...], b_ref[...])


def _combine(ys, comb, hf, ln_g, ln_b, tm=512):
    m = hf.shape[0]
    tm = min(tm, m)
    return pl.pallas_call(
        _combine_kernel,
        grid=(m // tm, N_EXPERTS),
        in_specs=[pl.BlockSpec((1, tm, D_MODEL), lambda i, e: (e, i, 0)),
                  pl.BlockSpec((tm, N_EXPERTS), lambda i, e: (i, 0)),
                  pl.BlockSpec((tm, D_MODEL), lambda i, e: (i, 0)),
                  pl.BlockSpec((1, D_MODEL), lambda i, e: (0, 0)),
                  pl.BlockSpec((1, D_MODEL), lambda i, e: (0, 0))],
        out_specs=pl.BlockSpec((tm, D_MODEL), lambda i, e: (i, 0)),
        out_shape=jax.ShapeDtypeStruct((m, D_MODEL), f32),
        scratch_shapes=[pltpu.VMEM((tm, D_MODEL), f32)],
        compiler_params=_cparams(("parallel", "arbitrary")),
        name="moe_combine",
    )(ys, comb, hf, ln_g.reshape(1, D_MODEL).astype(f32), ln_b.reshape(1, D_MODEL).astype(f32))


def _moe(hb, hf, router_w, wg, wu, wd, ln_g, ln_b):
    m = hb.shape[0]
    comb = _router(hf, router_w)
    tm = min(1024, m)
    tiles = m // tm
    xs = jnp.tile(hb, (N_EXPERTS, 1))
    tile_expert = jnp.repeat(jnp.arange(N_EXPERTS, dtype=jnp.int32), tiles)
    ys = _gmm(xs, tile_expert, wg.astype(bf16), wu.astype(bf16), wd.astype(bf16), tm)
    return _combine(ys.reshape(N_EXPERTS, m, D_MODEL), comb, hf, ln_g, ln_b)


_SPLITS = (ATTN_W, 2 * KV_W, 3 * DN_QK_W, DN_V_W, 4 * DN_HEADS, POOL_W, N_BRANCH * D_MODEL)
_SPLIT_DTYPES = (bf16, bf16, f32, bf16, f32, bf16, bf16)


def _mixer(hf, hb, bsz, seq, bias_tabs, w_in, sink, conv_w, a_log, dt_bias, dn_norm_w, pool_w, pool_scale,
           w_branch, w_out, ln_g, ln_b):
    points = np.cumsum(_SPLITS)[:-1].tolist()
    ws = jnp.split(w_in.astype(bf16), points, axis=-1)
    aq, akv, dqkv, dz, dab, pu, gates = _inproj(hb, ws, _SPLIT_DTYPES)
    shp = lambda t: t.reshape(bsz, seq, t.shape[-1])
    oa = _attention(shp(aq), shp(akv), bias_tabs, sink)
    qkvn = _dn_prep(shp(dqkv), conv_w)
    odf, odb = _delta(qkvn, shp(dab), a_log, dt_bias)
    op = _pool(shp(pu), pool_w, pool_scale)
    m = bsz * seq
    return _merge(oa.reshape(m, ATTN_W), odf.reshape(m, DN_V_W), odb.reshape(m, DN_V_W), dz, op.reshape(m, POOL_W),
                  gates, hf,
                  dn_norm_w, w_branch, w_out, ln_g, ln_b)


def kernel(x, w_in, attn_sink, rel_bias, conv_w, a_log, dt_bias, dn_norm_w, pool_w, pool_scale, w_branch, w_out,
           ln1_g, ln1_b, ln2_g, ln2_b, ffn_w1, ffn_w3, ffn_w2, router_w, moe_wg, moe_wu, moe_wd):
    bsz, seq, _ = x.shape
    m = bsz * seq
    bias_tabs = _attn_bias_tables(rel_bias)
    hf = x.reshape(m, D_MODEL).astype(f32)
    hb = hf.astype(bf16)
    for layer in range(DEPTH):
        hf, hb = _mixer(hf, hb, bsz, seq, bias_tabs, w_in[layer], attn_sink[layer], conv_w[layer], a_log[layer],
                        dt_bias[layer], dn_norm_w[layer], pool_w[layer], pool_scale[layer], w_branch[layer],
                        w_out[layer], ln1_g[layer], ln1_b[layer])
        i = layer // 2
        if layer % 2 == 0:
            hf, hb = _ffn(hb, hf, ffn_w1[i], ffn_w3[i], ffn_w2[i], ln2_g[layer], ln2_b[layer])
        else:
            hf = _moe(hb, hf, router_w[i], moe_wg[i], moe_wu[i], moe_wd[i], ln2_g[layer], ln2_b[layer])
    return hf.reshape(bsz, seq, D_MODEL).astype(x.dtype)
```

```python
import functools

import numpy as np
import jax
import jax.numpy as jnp
from jax import lax
from jax.experimental import pallas as pl
from jax.experimental.pallas import tpu as pltpu
from jax.experimental.pallas import tpu_sc as plsc

f32 = jnp.float32
bf16 = jnp.bfloat16

D_MODEL = 1024
DEPTH = 2
HEAD_DIM = 64
ATTN_Q_HEADS = 8
ATTN_KV_HEADS = 2
ATTN_GROUP = ATTN_Q_HEADS // ATTN_KV_HEADS
WINDOW = 128
BLOCK = 128
SPAN = BLOCK + 2 * WINDOW
N_BUCKETS = 32
MAX_DISTANCE = 128
DN_HEADS = 8
DN_DK = 64
DN_DV = 64
DN_CONV = 5
DN_CHUNK = 64
POOL_WINDOWS = (2, 4, 8, 16)
N_POOL_GROUPS = 4
POOL_GROUP = 128
ATTN_W = ATTN_Q_HEADS * HEAD_DIM
KV_W = ATTN_KV_HEADS * HEAD_DIM
DN_QK_W = DN_HEADS * DN_DK
DN_V_W = DN_HEADS * DN_DV
POOL_W = N_POOL_GROUPS * POOL_GROUP
N_BRANCH = 3
N_EXPERTS = 8
TOP_K = 2
DEEPNORM_ALPHA = (2.0 * DEPTH) ** 0.25
LN_EPS = 1e-5
RMS_EPS = 1e-6
NEG_INF = -1e30

VMEM_LIMIT_BYTES = 56 * 1024 * 1024
POOL_TILE = 256
POOL_SPAN = 512
CONV_HALO = 8


def _cparams(sem):
    return pltpu.CompilerParams(dimension_semantics=sem, vmem_limit_bytes=VMEM_LIMIT_BYTES)


def _full(shape):
    n = len(shape)
    return pl.BlockSpec(shape, lambda *_: (0,) * n)


def _sigmoid(x):
    return 1.0 / (1.0 + jnp.exp(-x))


def _silu(x):
    return x * _sigmoid(x)


def _layer_norm(x, g, b):
    mu = jnp.mean(x, -1, keepdims=True)
    xc = x - mu
    var = jnp.mean(xc * xc, -1, keepdims=True)
    return xc * lax.rsqrt(var + LN_EPS) * g + b


def _dot(a, b):
    return jnp.dot(a, b, preferred_element_type=f32)


def _dot_nt(a, b):
    return lax.dot_general(a, b, (((1,), (1,)), ((), ())), preferred_element_type=f32)


def _dot_tn(a, b):
    return lax.dot_general(a, b, (((0,), (0,)), ((), ())), preferred_element_type=f32)


def _split3(x):
    hi = x.astype(bf16)
    r = x - hi.astype(f32)
    mid = r.astype(bf16)
    lo = (r - mid.astype(f32)).astype(bf16)
    return hi, mid, lo


def _dot_exact_lhs(a_bf, x):
    hi, mid, lo = _split3(x)
    return _dot(a_bf, hi) + _dot(a_bf, mid) + _dot(a_bf, lo)


def _dot_exact_rhs(x, b_bf):
    hi, mid, lo = _split3(x)
    return _dot(hi, b_bf) + _dot(mid, b_bf) + _dot(lo, b_bf)


def _dot_f32(a, b):
    return jnp.dot(a, b, preferred_element_type=f32, precision=lax.Precision.HIGHEST)


def _inproj_kernel(x_ref, *refs):
    n = len(refs) // 2
    x = x_ref[...]
    for w_ref, o_ref in zip(refs[:n], refs[n:]):
        o_ref[...] = _dot(x, w_ref[...]).astype(o_ref.dtype)


def _inproj(xb, ws, dtypes, tm=256):
    m = xb.shape[0]
    return pl.pallas_call(
        _inproj_kernel,
        grid=(m // tm,),
        in_specs=[pl.BlockSpec((tm, D_MODEL), lambda i: (i, 0))]
        + [pl.BlockSpec(w.shape, lambda i: (0, 0), pipeline_mode=pl.Buffered(1)) for w in ws],
        out_specs=[pl.BlockSpec((tm, w.shape[1]), lambda i: (i, 0)) for w in ws],
        out_shape=[jax.ShapeDtypeStruct((m, w.shape[1]), dt) for w, dt in zip(ws, dtypes)],
        compiler_params=_cparams(("parallel",)),
        name="inproj",
    )(xb, *ws)


def _t5_bucket(rel):
    nb = N_BUCKETS // 2
    max_exact = nb // 2
    n = np.abs(rel)
    large = max_exact + (np.log(np.maximum(n, 1) / max_exact) / np.log(MAX_DISTANCE / max_exact)
                         * (nb - max_exact)).astype(np.int32)
    large = np.minimum(large, nb - 1)
    return (np.where(rel > 0, nb, 0) + np.where(n < max_exact, n, large)).astype(np.int32)


def _bias_bucket_tables():
    r = np.arange(BLOCK)[:, None]
    j = np.arange(SPAN)[None, :]
    tabs = []
    for shift in (0, WINDOW, 2 * WINDOW):
        rel = j - shift - r
        tabs.append(np.where(np.abs(rel) <= WINDOW, _t5_bucket(rel), N_BUCKETS))
    return jnp.asarray(np.stack(tabs), dtype=jnp.int32)


def _bias_table_kernel(bucket_ref, rb_ref, o_ref):
    bucket = bucket_ref[0]
    for h in range(ATTN_Q_HEADS):
        acc = jnp.full(bucket.shape, NEG_INF, f32)
        for b in range(N_BUCKETS):
            acc = jnp.where(bucket == b, rb_ref[b, h], acc)
        o_ref[0, h] = acc


def _attn_bias_tables(rel_bias):
    return pl.pallas_call(
        _bias_table_kernel,
        grid=(3,),
        in_specs=[pl.BlockSpec((1, BLOCK, SPAN), lambda v: (v, 0, 0)),
                  pl.BlockSpec(memory_space=pltpu.SMEM)],
        out_specs=pl.BlockSpec((1, ATTN_Q_HEADS, BLOCK, SPAN), lambda v: (v, 0, 0, 0)),
        out_shape=jax.ShapeDtypeStruct((3, ATTN_Q_HEADS, BLOCK, SPAN), f32),
        compiler_params=_cparams(("parallel",)),
        name="attn_bias",
    )(_bias_bucket_tables(), rel_bias.astype(f32))


def _attn_kernel(q_ref, kv_ref, bias_ref, sink_ref, o_ref, *, seq):
    i = pl.program_id(1)
    ks = pl.multiple_of(jnp.clip(i * BLOCK - WINDOW, 0, seq - SPAN), BLOCK)
    kv = kv_ref[0, pl.ds(ks, SPAN), :]
    q = q_ref[0]
    scale = HEAD_DIM ** -0.5
    heads = range(ATTN_Q_HEADS)
    s = [_dot_nt(q[:, HEAD_DIM * h:HEAD_DIM * (h + 1)],
                 kv[:, HEAD_DIM * (h // ATTN_GROUP):HEAD_DIM * (h // ATTN_GROUP + 1)]) * scale + bias_ref[0, h]
         for h in heads]
    sk = [sink_ref[0:1, h:h + 1] for h in heads]
    m = [jnp.maximum(jnp.max(s[h], -1, keepdims=True), sk[h]) for h in heads]
    p = [jnp.exp(s[h] - m[h]) for h in heads]
    denom = [jnp.sum(p[h], -1, keepdims=True) + jnp.exp(sk[h] - m[h]) for h in heads]
    outs = [_dot(p[h].astype(bf16), kv[:, KV_W + HEAD_DIM * (h // ATTN_GROUP):KV_W + HEAD_DIM * (h // ATTN_GROUP + 1)])
            / denom[h] for h in heads]
    o_ref[0] = jnp.concatenate(outs, -1).astype(o_ref.dtype)


def _attention(aq, akv, bias_tabs, sink):
    b, s, _ = aq.shape
    nb = s // BLOCK
    assert s % BLOCK == 0 and s >= SPAN

    def variant(bi, i):
        return (jnp.where(i == 0, 0, jnp.where(i == nb - 1, 2, 1)), 0, 0, 0)

    return pl.pallas_call(
        functools.partial(_attn_kernel, seq=s),
        grid=(b, nb),
        in_specs=[pl.BlockSpec((1, BLOCK, ATTN_W), lambda bi, i: (bi, i, 0)),
                  pl.BlockSpec((1, s, 2 * KV_W), lambda bi, i: (bi, 0, 0)),
                  pl.BlockSpec((1, ATTN_Q_HEADS, BLOCK, SPAN), variant),
                  _full((1, ATTN_Q_HEADS))],
        out_specs=pl.BlockSpec((1, BLOCK, ATTN_W), lambda bi, i: (bi, i, 0)),
        out_shape=jax.ShapeDtypeStruct((b, s, ATTN_W), bf16),
        compiler_params=_cparams(("parallel", "arbitrary")),
        name="attn",
    )(aq, akv, bias_tabs, sink.reshape(1, ATTN_Q_HEADS).astype(f32))


def _dn_prep_kernel(u_ref, w_ref, ones_ref, o_ref, pad_ref, *, seq):
    sec = pl.program_id(1)
    zeros = jnp.zeros((CONV_HALO, DN_QK_W), f32)
    pad_ref[0:CONV_HALO, :] = zeros
    pad_ref[seq + CONV_HALO:seq + 2 * CONV_HALO, :] = zeros
    pad_ref[CONV_HALO:seq + CONV_HALO, :] = u_ref[0].astype(f32)
    reach = (DN_CONV - 1) // 2
    acc = None
    for k in range(DN_CONV):
        term = pad_ref[CONV_HALO - reach + k:CONV_HALO - reach + k + seq, :] * w_ref[0, k:k + 1, :]
        acc = term if acc is None else acc + term
    y = _silu(acc)

    @pl.when(sec < 2)
    def _():
        ss = _dot_exact_rhs(y * y, ones_ref[...])
        scale = jnp.where(sec == 0, DN_DK ** -0.5, 1.0)
        o_ref[0] = y * lax.rsqrt(ss + 1e-6) * scale

    @pl.when(sec == 2)
    def _():
        o_ref[0] = y


def _head_ones():
    blk = np.arange(DN_QK_W) // DN_DK
    return jnp.asarray(blk[:, None] == blk[None, :], dtype=bf16)


def _dn_prep(dqkv, conv_w):
    b, s, _ = dqkv.shape
    w3 = jnp.transpose(conv_w.reshape(DN_CONV, 3, DN_QK_W), (1, 0, 2)).astype(f32)
    return pl.pallas_call(
        functools.partial(_dn_prep_kernel, seq=s),
        grid=(b, 3),
        in_specs=[pl.BlockSpec((1, s, DN_QK_W), lambda bi, c: (bi, 0, c)),
                  pl.BlockSpec((1, DN_CONV, DN_QK_W), lambda bi, c: (c, 0, 0)),
                  _full((DN_QK_W, DN_QK_W))],
        out_specs=pl.BlockSpec((1, s, DN_QK_W), lambda bi, c: (bi, 0, c)),
        out_shape=jax.ShapeDtypeStruct(dqkv.shape, f32),
        scratch_shapes=[pltpu.VMEM((s + 2 * CONV_HALO, DN_QK_W), f32)],
        compiler_params=_cparams(("parallel", "arbitrary")),
        name="dn_prep",
    )(dqkv, w3, _head_ones())


def _softplus(x):
    return jnp.maximum(x, 0.0) + jnp.log1p(jnp.exp(-jnp.abs(x)))


QUAD = 4
QUAD_W = QUAD * DN_DK
N_QUADS = DN_HEADS // QUAD
DELTA_BLOCK_CHUNKS = 4


def _block_diag(xq):
    lane_head = lax.broadcasted_iota(jnp.int32, xq.shape, 1) // DN_DK
    return jnp.concatenate([jnp.where(lane_head == h, xq, 0.0).astype(bf16) for h in range(QUAD)], axis=0)


def _delta_chunk(q, k, v, gx, bx, d):
    C = DN_CHUNK
    W = DN_QK_W
    sign = 1 - 2 * d
    ii = lax.broadcasted_iota(jnp.int32, (C, W), 0)
    jl = lax.broadcasted_iota(jnp.int32, (C, W), 1) % DN_DK
    order = (ii - jl) * sign
    incl = order >= 0
    strict = order > 0
    rr = lax.broadcasted_iota(jnp.int32, (2 * C, 2 * C), 0)
    cc = lax.broadcasted_iota(jnp.int32, (2 * C, 2 * C), 1)
    tri = jnp.where(((rr % C) - cc) * sign >= 0, 1.0, 0.0)
    lmat = jnp.where(cc < C, tri, jnp.where(rr < C, -1.0, 0.0)).astype(bf16)
    rmat = jnp.concatenate([gx, jnp.where(order <= 0, gx, 0.0)], axis=0)
    dg = None
    for part in _split3(rmat):
        term = _dot(lmat, part)
        dg = term if dg is None else dg + term
    dmat, gc = dg[:C], dg[C:]
    decay = jnp.where(incl, jnp.exp(jnp.where(incl, dmat, 0.0)), 0.0)
    eg = jnp.exp(gc)
    last = C - 1 if d == 0 else 0
    gl = gc[last:last + 1, :]
    kdec = k * jnp.exp(gl - gc)
    cd = jnp.exp(gl)
    kb = k * bx
    eye = jnp.where(ii == jl, 1.0, 0.0)
    return dict(q=q, k=k, bx=bx, decay=decay, strict=strict, eye=eye, kdec=kdec, cd=cd,
                vb=v * bx, kbe=kb * eg, qdec=q * eg)


def _delta_kernel(qf_ref, kf_ref, vf_ref, qb_ref, kb_ref, vb_ref, abf_ref, abb_ref, alog_ref, dtb_ref,
                  of_ref, ob_ref, state_ref):
    C, H = DN_CHUNK, DN_HEADS

    @pl.when(pl.program_id(1) == 0)
    def _():
        state_ref[...] = jnp.zeros(state_ref.shape, f32)

    def expand(x, parts):
        rows = lax.broadcasted_iota(jnp.int32, (H * parts, DN_QK_W), 0) % H
        cols = lax.broadcasted_iota(jnp.int32, (H * parts, DN_QK_W), 1) // DN_DK
        emat = jnp.where(rows == cols, 1.0, 0.0).astype(bf16)
        pieces = [p.astype(f32) for p in _split3(x)[:parts]]
        return _dot(jnp.concatenate(pieces, axis=-1).astype(bf16), emat)

    chains = []
    dirs = ((0, qf_ref, kf_ref, vf_ref, abf_ref), (1, qb_ref, kb_ref, vb_ref, abb_ref))
    for d, q_ref, k_ref, v_ref, ab_ref in dirs:
        ab = ab_ref[0, 0]
        g = -jnp.exp(alog_ref[d]) * _softplus(ab[:, :H] + dtb_ref[d])
        gx = expand(g, 3)
        bx = expand(_sigmoid(ab[:, H:]), 2)
        for c in range(DELTA_BLOCK_CHUNKS):
            rows = slice(C * c, C * (c + 1))
            pre = _delta_chunk(q_ref[0, rows, :], k_ref[0, rows, :], v_ref[0, rows, :], gx[rows], bx[rows], d)
            for qd in range(N_QUADS):
                sl = slice(QUAD_W * qd, QUAD_W * (qd + 1))
                chains.append(dict(d=d, c=c, qd=qd, **{n: x[:, sl] for n, x in pre.items()}))

    for ch in chains:
        kq = _dot_nt(jnp.concatenate([ch["k"], ch["q"]], axis=0).astype(bf16), _block_diag(ch["k"]))
        a_mat = jnp.where(ch["strict"], kq[:C] * ch["bx"] * ch["decay"], 0.0)
        ch["qk"] = kq[C:] * ch["decay"]
        ch["p"] = -a_mat
        ch["t"] = ch["eye"] - a_mat
    for ch in chains:
        ch["p"] = _dot(ch["p"].astype(bf16), _block_diag(ch["p"]))
    for _ in range(4):
        for ch in chains:
            pt = _dot(jnp.concatenate([ch["p"], ch["t"]], axis=0).astype(bf16), _block_diag(ch["p"]))
            ch["p"], ch["t"] = pt[:C], ch["t"] + pt[C:]
    for ch in chains:
        ch["tb"] = (ch["t"] + _dot(ch["t"].astype(bf16), _block_diag(ch["p"]))).astype(bf16)
    for ch in chains:
        ch["u"] = _dot(ch["tb"], _block_diag(ch["vb"]))
        ch["w"] = _dot(ch["tb"], _block_diag(ch["kbe"]))

    row_head = lax.broadcasted_iota(jnp.int32, (QUAD_W, QUAD_W), 0) // DN_DK
    col_head = lax.broadcasted_iota(jnp.int32, (QUAD_W, QUAD_W), 1) // DN_DK
    o_refs = (of_ref, ob_ref)
    for step in range(DELTA_BLOCK_CHUNKS):
        for ch in chains:
            d, c, qd = ch["d"], ch["c"], ch["qd"]
            if c != (step if d == 0 else DELTA_BLOCK_CHUNKS - 1 - step):
                continue
            st = state_ref[d, qd]
            wq = _dot(jnp.concatenate([ch["w"], ch["qdec"]], axis=0).astype(bf16), st.astype(bf16))
            v_new = ch["u"] - wq[:C]
            o = wq[C:] + _dot(ch["qk"].astype(bf16), _block_diag(v_new))
            o_refs[d][0, C * c:C * (c + 1), QUAD_W * qd:QUAD_W * (qd + 1)] = o
            s_new = _dot_tn(ch["kdec"].astype(bf16), v_new.astype(bf16))
            state_ref[d, qd] = st * ch["cd"] + jnp.where(row_head == col_head, s_new, 0.0)


def _delta(qkvn, dab, a_log, dt_bias):
    b, s, _ = qkvn.shape
    H = DN_HEADS
    tb = DN_CHUNK * DELTA_BLOCK_CHUNKS
    nblk = s // tb
    assert s % tb == 0
    d4 = dab.reshape(b, s, 4, H)
    ab = jnp.stack([jnp.concatenate([d4[:, :, r], d4[:, :, 2 + r]], -1) for r in range(2)])
    alog = a_log.astype(f32).reshape(2, 1, H)
    dtb = dt_bias.astype(f32).reshape(2, 1, H)

    def sec(n, rev):
        if rev:
            return pl.BlockSpec((1, tb, DN_QK_W), lambda bi, i: (bi, nblk - 1 - i, n))
        return pl.BlockSpec((1, tb, DN_QK_W), lambda bi, i: (bi, i, n))

    return pl.pallas_call(
        _delta_kernel,
        grid=(b, nblk),
        in_specs=[sec(0, False), sec(1, False), sec(2, False), sec(0, True), sec(1, True), sec(2, True),
                  pl.BlockSpec((1, 1, tb, 2 * H), lambda bi, i: (0, bi, i, 0)),
                  pl.BlockSpec((1, 1, tb, 2 * H), lambda bi, i: (1, bi, nblk - 1 - i, 0)),
                  _full((2, 1, H)), _full((2, 1, H))],
        out_specs=[sec(0, False), sec(0, True)],
        out_shape=[jax.ShapeDtypeStruct((b, s, DN_V_W), f32)] * 2,
        scratch_shapes=[pltpu.VMEM((2, N_QUADS, QUAD_W, QUAD_W), f32)],
        compiler_params=_cparams(("parallel", "arbitrary")),
        name="delta",
    )(qkvn, qkvn, qkvn, qkvn, qkvn, qkvn, ab, ab, alog, dtb)


def _pool_band_tables():
    r = np.arange(POOL_TILE)[:, None]
    j = np.arange(POOL_SPAN)[None, :]
    tabs = np.zeros((3, N_POOL_GROUPS, POOL_TILE, POOL_SPAN), np.float32)
    for vi, shift in enumerate((0, 128, 256)):
        rel = j - shift - r
        for gi, w in enumerate(POOL_WINDOWS):
            tabs[vi, gi] = np.abs(rel) <= w // 2
    return jnp.asarray(tabs, dtype=bf16)


def _pool_kernel(u_ref, band_ref, w_ref, scale_ref, o_ref, *, seq):
    t = pl.program_id(1)
    t0 = pl.multiple_of(t * POOL_TILE, POOL_TILE)
    ks = pl.multiple_of(jnp.clip(t0 - 128, 0, seq - POOL_SPAN), 128)
    win = u_ref[0, pl.ds(ks, POOL_SPAN), :]
    own = u_ref[0, pl.ds(t0, POOL_TILE), :].astype(f32)
    pos = t0 + lax.broadcasted_iota(jnp.int32, (POOL_TILE, POOL_GROUP), 0)
    outs = []
    for gi, w in enumerate(POOL_WINDOWS):
        r = w // 2
        sl = slice(POOL_GROUP * gi, POOL_GROUP * (gi + 1))
        wsum = _dot(band_ref[0, gi], win[:, sl])
        count = (jnp.minimum(pos + r, seq - 1) - jnp.maximum(pos - r, 0) + 1).astype(f32)
        mixed = wsum / count - own[:, sl]
        outs.append(_dot(mixed.astype(bf16), w_ref[gi]))
    o_ref[0] = (jnp.concatenate(outs, -1) * scale_ref[...]).astype(o_ref.dtype)


def _pool(pu, pool_w, pool_scale):
    b, s, _ = pu.shape
    nt = s // POOL_TILE
    assert s % POOL_TILE == 0 and s >= POOL_SPAN

    def variant(bi, t):
        return (jnp.where(t == 0, 0, jnp.where(t == nt - 1, 2, 1)), 0, 0, 0)

    return pl.pallas_call(
        functools.partial(_pool_kernel, seq=s),
        grid=(b, nt),
        in_specs=[pl.BlockSpec((1, s, POOL_W), lambda bi, t: (bi, 0, 0)),
                  pl.BlockSpec((1, N_POOL_GROUPS, POOL_TILE, POOL_SPAN), variant),
                  _full((N_POOL_GROUPS, POOL_GROUP, POOL_GROUP)),
                  _full((1, POOL_W))],
        out_specs=pl.BlockSpec((1, POOL_TILE, POOL_W), lambda bi, t: (bi, t, 0)),
        out_shape=jax.ShapeDtypeStruct((b, s, POOL_W), bf16),
        compiler_params=_cparams(("parallel", "arbitrary")),
        name="pool",
    )(pu, _pool_band_tables(), pool_w.astype(bf16), pool_scale.reshape(1, POOL_W).astype(f32))


def _merge_kernel(oa_ref, odf_ref, odb_ref, z_ref, op_ref, gate_ref, h_ref, ones_ref, nw_ref, wb_ref, wo_ref,
                  g_ref, b_ref, hf_ref, hb_ref):
    od = odf_ref[...] + odb_ref[...]
    ms = _dot_exact_rhs(od * od, ones_ref[...]) * (1.0 / DN_DV)
    od = od * lax.rsqrt(ms + RMS_EPS) * nw_ref[...] * _silu(z_ref[...].astype(f32))
    branches = (oa_ref[...], od.astype(bf16), op_ref[...])
    merged = None
    for n in range(N_BRANCH):
        up = _dot(branches[n], wb_ref[n])
        gate = _sigmoid(gate_ref[:, D_MODEL * n:D_MODEL * (n + 1)].astype(f32))
        merged = gate * up if merged is None else merged + gate * up
    y = _dot(merged.astype(bf16), wo_ref[...])
    hn = _layer_norm(DEEPNORM_ALPHA * h_ref[...] + y, g_ref[...], b_ref[...])
    hf_ref[...] = hn
    hb_ref[...] = hn.astype(bf16)


def _merge(oa, odf, odb, z, op, gates, h, dn_norm_w, w_branch, w_out, ln_g, ln_b, tm=256):
    m = h.shape[0]
    row = lambda w: pl.BlockSpec((tm, w), lambda i: (i, 0))
    nw = jnp.tile(dn_norm_w.astype(f32), DN_HEADS).reshape(1, DN_V_W)
    return pl.pallas_call(
        _merge_kernel,
        grid=(m // tm,),
        in_specs=[row(ATTN_W), row(DN_V_W), row(DN_V_W), row(DN_V_W), row(POOL_W),
                  row(N_BRANCH * D_MODEL), row(D_MODEL),
                  _full((DN_V_W, DN_V_W)), _full((1, DN_V_W)),
                  _full((N_BRANCH, ATTN_W, D_MODEL)), _full((D_MODEL, D_MODEL)),
                  _full((1, D_MODEL)), _full((1, D_MODEL))],
        out_specs=[row(D_MODEL), row(D_MODEL)],
        out_shape=[jax.ShapeDtypeStruct((m, D_MODEL), f32), jax.ShapeDtypeStruct((m, D_MODEL), bf16)],
        compiler_params=_cparams(("parallel",)),
        name="merge",
    )(oa, odf, odb, z, op, gates, h, _head_ones(), nw, w_branch.astype(bf16), w_out.astype(bf16),
      ln_g.reshape(1, D_MODEL).astype(f32), ln_b.reshape(1, D_MODEL).astype(f32))


def _swiglu_step(x, wg_ref, wu_ref, wd_ref, acc_ref):
    g = _dot(x, wg_ref[0])
    u = _dot(x, wu_ref[0])
    acc_ref[...] += _dot((_silu(g) * u).astype(bf16), wd_ref[0])


def _ffn_kernel(x_ref, h_ref, wg_ref, wu_ref, wd_ref, g_ref, b_ref, hf_ref, hb_ref, acc_ref):
    j = pl.program_id(1)

    @pl.when(j == 0)
    def _():
        acc_ref[...] = jnp.zeros(acc_ref.shape, f32)

    _swiglu_step(x_ref[...], wg_ref, wu_ref, wd_ref, acc_ref)

    @pl.when(j == pl.num_programs(1) - 1)
    def _():
        hn = _layer_norm(DEEPNORM_ALPHA * h_ref[...] + acc_ref[...], g_ref[...], b_ref[...])
        hf_ref[...] = hn
        hb_ref[...] = hn.astype(bf16)


def _ffn(hb, hf, w1, w3, w2, ln_g, ln_b, tm=1024, tf=256):
    m = hb.shape[0]
    tm = min(tm, m)
    ff = w1.shape[-1]
    row = pl.BlockSpec((tm, D_MODEL), lambda i, j: (i, 0))
    return pl.pallas_call(
        _ffn_kernel,
        grid=(m // tm, ff // tf),
        in_specs=[row, row,
                  pl.BlockSpec((1, D_MODEL, tf), lambda i, j: (0, 0, j)),
                  pl.BlockSpec((1, D_MODEL, tf), lambda i, j: (0, 0, j)),
                  pl.BlockSpec((1, tf, D_MODEL), lambda i, j: (0, j, 0)),
                  pl.BlockSpec((1, D_MODEL), lambda i, j: (0, 0)),
                  pl.BlockSpec((1, D_MODEL), lambda i, j: (0, 0))],
        out_specs=[row, row],
        out_shape=[jax.ShapeDtypeStruct((m, D_MODEL), f32), jax.ShapeDtypeStruct((m, D_MODEL), bf16)],
        scratch_shapes=[pltpu.VMEM((tm, D_MODEL), f32)],
        compiler_params=_cparams(("parallel", "arbitrary")),
        name="ffn",
    )(hb, hf, w1.astype(bf16)[None], w3.astype(bf16)[None], w2.astype(bf16)[None],
      ln_g.reshape(1, D_MODEL).astype(f32), ln_b.reshape(1, D_MODEL).astype(f32))


MOE_TILE = 1024
SC_CORES = 2
SC_SUBCORES = 16
SC_WORKERS = SC_CORES * SC_SUBCORES
SC_ROWS = 64


def _gmm_kernel(te_ref, tv_ref, x_ref, wg_ref, wu_ref, wd_ref, o_ref, acc_ref):
    i = pl.program_id(0)
    j = pl.program_id(1)
    valid = tv_ref[i]

    @pl.when(j == 0)
    def _():
        acc_ref[...] = jnp.zeros(acc_ref.shape, f32)

    @pl.when(valid > 0)
    def _():
        rows = lax.broadcasted_iota(jnp.int32, x_ref.shape, 0)
        x = jnp.where(rows < valid, x_ref[...], 0.0).astype(bf16)
        _swiglu_step(x, wg_ref, wu_ref, wd_ref, acc_ref)

    @pl.when(j == pl.num_programs(1) - 1)
    def _():
        o_ref[...] = acc_ref[...]


def _gmm(xs, tile_expert, tile_valid, wg, wu, wd, tm, tf=512):
    n = xs.shape[0]
    ff = wg.shape[-1]
    nj = ff // tf

    def fcol(i, j, te, tv):
        return jnp.where(tv[i] > 0, j, nj - 1)

    grid_spec = pltpu.PrefetchScalarGridSpec(
        num_scalar_prefetch=2,
        grid=(n // tm, nj),
        in_specs=[pl.BlockSpec((tm, D_MODEL), lambda i, j, te, tv: (i, 0)),
                  pl.BlockSpec((1, D_MODEL, tf), lambda i, j, te, tv: (te[i], 0, fcol(i, j, te, tv))),
                  pl.BlockSpec((1, D_MODEL, tf), lambda i, j, te, tv: (te[i], 0, fcol(i, j, te, tv))),
                  pl.BlockSpec((1, tf, D_MODEL), lambda i, j, te, tv: (te[i], fcol(i, j, te, tv), 0))],
        out_specs=pl.BlockSpec((tm, D_MODEL), lambda i, j, te, tv: (i, 0)),
        scratch_shapes=[pltpu.VMEM((tm, D_MODEL), f32)])
    return pl.pallas_call(
        _gmm_kernel,
        grid_spec=grid_spec,
        out_shape=jax.ShapeDtypeStruct((n, D_MODEL), f32),
        compiler_params=_cparams(("parallel", "arbitrary")),
        name="moe_gmm",
    )(tile_expert, tile_valid, xs, wg, wu, wd)


def _router_kernel(x_ref, w_ref, mi_ref, mf_ref, cnt_ref, carry_ref):
    @pl.when(pl.program_id(0) == 0)
    def _():
        carry_ref[...] = jnp.zeros(carry_ref.shape, f32)

    logits = _dot_f32(x_ref[...], w_ref[...])
    tm = logits.shape[0]
    lane = lax.broadcasted_iota(jnp.int32, logits.shape, 1)
    m1 = jnp.max(logits, -1, keepdims=True)
    i1 = jnp.min(jnp.where(logits == m1, lane, N_EXPERTS), -1, keepdims=True)
    rest = jnp.where(lane == i1, -jnp.inf, logits)
    m2 = jnp.max(rest, -1, keepdims=True)
    i2 = jnp.min(jnp.where(rest == m2, lane, N_EXPERTS), -1, keepdims=True)
    e2 = jnp.exp(m2 - m1)
    p1 = 1.0 / (1.0 + e2)
    p2 = e2 / (1.0 + e2)
    oh1 = jnp.where(lane == i1, 1.0, 0.0)
    oh2 = jnp.where(lane == i2, 1.0, 0.0)
    both = oh1 + oh2
    rr = lax.broadcasted_iota(jnp.int32, (tm, tm), 0)
    cc = lax.broadcasted_iota(jnp.int32, (tm, tm), 1)
    earlier = jnp.where(cc < rr, 1.0, 0.0).astype(bf16)
    before = _dot(earlier, both.astype(bf16)) + carry_ref[...]
    r1 = jnp.sum(oh1 * before, -1, keepdims=True).astype(jnp.int32)
    r2 = jnp.sum(oh2 * before, -1, keepdims=True).astype(jnp.int32)
    carry_ref[...] += jnp.sum(both, 0, keepdims=True)
    cnt_ref[...] = carry_ref[...].astype(jnp.int32)
    mi_ref[...] = jnp.where(lane == 0, i1, jnp.where(lane == 1, i2, jnp.where(lane == 2, r1,
                                                                                  jnp.where(lane == 3, r2, 0))))
    mf_ref[...] = jnp.where(lane == 0, p1, jnp.where(lane == 1, p2, 0.0))


def _router(hf, router_w, tm=512):
    m = hf.shape[0]
    tm = min(tm, m)
    return pl.pallas_call(
        _router_kernel,
        grid=(m // tm,),
        in_specs=[pl.BlockSpec((tm, D_MODEL), lambda i: (i, 0)), _full((D_MODEL, N_EXPERTS))],
        out_specs=[pl.BlockSpec((tm, N_EXPERTS), lambda i: (i, 0)),
                   pl.BlockSpec((tm, N_EXPERTS), lambda i: (i, 0)),
                   _full((1, N_EXPERTS))],
        out_shape=[jax.ShapeDtypeStruct((m, N_EXPERTS), jnp.int32),
                   jax.ShapeDtypeStruct((m, N_EXPERTS), f32),
                   jax.ShapeDtypeStruct((1, N_EXPERTS), jnp.int32)],
        scratch_shapes=[pltpu.VMEM((1, N_EXPERTS), f32)],
        compiler_params=_cparams(("arbitrary",)),
        name="router",
    )(hf, router_w.astype(f32))


def _sc_mesh():
    return plsc.VectorSubcoreMesh(core_axis_name="c", subcore_axis_name="s",
                                  num_cores=SC_CORES, num_subcores=SC_SUBCORES)


def _sc_worker_base(per_worker):
    return (lax.axis_index("s") * SC_CORES + lax.axis_index("c")) * per_worker


def _sc_scatter_rows(src, idx0, idx1, n_out):
    m, d = src.shape
    per_worker = m // SC_WORKERS
    assert m % (SC_WORKERS * SC_ROWS) == 0

    def body(src_hbm, idx0_hbm, idx1_hbm, out_hbm, idx_v, rows_v, sem):
        base = _sc_worker_base(per_worker)

        @pl.loop(0, per_worker // SC_ROWS)
        def _(j):
            off = pl.multiple_of(base + j * SC_ROWS, SC_ROWS)
            pltpu.sync_copy(src_hbm.at[pl.ds(off, SC_ROWS)], rows_v)
            for idx_hbm in (idx0_hbm, idx1_hbm):
                pltpu.sync_copy(idx_hbm.at[pl.ds(off, SC_ROWS)], idx_v)
                pltpu.async_copy(rows_v, out_hbm.at[idx_v], sem).wait()

    return pl.kernel(
        body, out_type=jax.ShapeDtypeStruct((n_out, d), src.dtype), mesh=_sc_mesh(),
        scratch_types=[pltpu.VMEM((SC_ROWS,), jnp.int32), pltpu.VMEM((SC_ROWS, d), src.dtype),
                       pltpu.SemaphoreType.DMA],
        name="moe_dispatch",
    )(src, idx0, idx1)


def _sc_gather_rows(table, idx):
    n = idx.shape[0]
    d = table.shape[1]
    per_worker = n // SC_WORKERS
    assert n % (SC_WORKERS * SC_ROWS) == 0

    def body(table_hbm, idx_hbm, out_hbm, idx_v, rows_v, sem):
        base = _sc_worker_base(per_worker)

        @pl.loop(0, per_worker // SC_ROWS)
        def _(j):
            off = pl.multiple_of(base + j * SC_ROWS, SC_ROWS)
            pltpu.sync_copy(idx_hbm.at[pl.ds(off, SC_ROWS)], idx_v)
            pltpu.async_copy(table_hbm.at[idx_v], rows_v, sem).wait()
            pltpu.sync_copy(rows_v, out_hbm.at[pl.ds(off, SC_ROWS)])

    return pl.kernel(
        body, out_type=jax.ShapeDtypeStruct((n, d), table.dtype), mesh=_sc_mesh(),
        scratch_types=[pltpu.VMEM((SC_ROWS,), jnp.int32), pltpu.VMEM((SC_ROWS, d), table.dtype),
                       pltpu.SemaphoreType.DMA],
        name="moe_collect",
    )(table, idx)


def _combine_kernel(y0_ref, y1_ref, mf_ref, h_ref, g_ref, b_ref, o_ref):
    p = mf_ref[...]
    y = p[:, 0:1] * y0_ref[0] + p[:, 1:2] * y1_ref[0]
    o_ref[...] = _layer_norm(DEEPNORM_ALPHA * h_ref[...] + y, g_ref[...], b_ref[...])


def _combine(yg, mf, hf, ln_g, ln_b, tm=512):
    m = hf.shape[0]
    tm = min(tm, m)
    return pl.pallas_call(
        _combine_kernel,
        grid=(m // tm,),
        in_specs=[pl.BlockSpec((1, tm, D_MODEL), lambda i: (0, i, 0)),
                  pl.BlockSpec((1, tm, D_MODEL), lambda i: (1, i, 0)),
                  pl.BlockSpec((tm, N_EXPERTS), lambda i: (i, 0)),
                  pl.BlockSpec((tm, D_MODEL), lambda i: (i, 0)),
                  _full((1, D_MODEL)), _full((1, D_MODEL))],
        out_specs=pl.BlockSpec((tm, D_MODEL), lambda i: (i, 0)),
        out_shape=jax.ShapeDtypeStruct((m, D_MODEL), f32),
        compiler_params=_cparams(("parallel",)),
        name="moe_combine",
    )(yg, yg, mf, hf, ln_g.reshape(1, D_MODEL).astype(f32), ln_b.reshape(1, D_MODEL).astype(f32))


def _moe(hf, router_w, wg, wu, wd, ln_g, ln_b):
    m = hf.shape[0]
    tm = MOE_TILE
    mi, mf, cnt = _router(hf, router_w)
    counts = cnt[0]
    padded = (counts + tm - 1) // tm * tm
    ends = jnp.cumsum(padded)
    starts = ends - padded
    experts = jnp.arange(N_EXPERTS, dtype=jnp.int32)
    pos = jnp.sum(jnp.where(mi[:, 0:TOP_K, None] == experts, starts, 0), -1) + mi[:, TOP_K:2 * TOP_K]
    n_tiles = TOP_K * m // tm + N_EXPERTS
    tile_start = jnp.arange(n_tiles, dtype=jnp.int32) * tm
    tile_expert = jnp.minimum(jnp.sum(tile_start[:, None] >= ends[None, :], -1), N_EXPERTS - 1).astype(jnp.int32)
    tile_valid = jnp.clip(starts[tile_expert] + counts[tile_expert] - tile_start, 0, tm).astype(jnp.int32)
    tile_valid = jnp.where(tile_start < ends[-1], tile_valid, 0)
    xs = _sc_scatter_rows(hf, pos[:, 0], pos[:, 1], n_tiles * tm)
    ys = _gmm(xs, tile_expert, tile_valid, wg.astype(bf16), wu.astype(bf16), wd.astype(bf16), tm)
    yg = _sc_gather_rows(ys, jnp.concatenate([pos[:, 0], pos[:, 1]]))
    return _combine(yg.reshape(TOP_K, m, D_MODEL), mf, hf, ln_g, ln_b)


_SPLITS = (ATTN_W, 2 * KV_W, 3 * DN_QK_W, DN_V_W, 4 * DN_HEADS, POOL_W, N_BRANCH * D_MODEL)
_SPLIT_DTYPES = (bf16, bf16, f32, bf16, f32, bf16, bf16)


def _mixer(hf, hb, bsz, seq, bias_tabs, w_in, sink, conv_w, a_log, dt_bias, dn_norm_w, pool_w, pool_scale,
           w_branch, w_out, ln_g, ln_b):
    points = np.cumsum(_SPLITS)[:-1].tolist()
    ws = jnp.split(w_in.astype(bf16), points, axis=-1)
    aq, akv, dqkv, dz, dab, pu, gates = _inproj(hb, ws, _SPLIT_DTYPES)
    shp = lambda t: t.reshape(bsz, seq, t.shape[-1])
    oa = _attention(shp(aq), shp(akv), bias_tabs, sink)
    qkvn = _dn_prep(shp(dqkv), conv_w)
    odf, odb = _delta(qkvn, shp(dab), a_log, dt_bias)
    op = _pool(shp(pu), pool_w, pool_scale)
    m = bsz * seq
    return _merge(oa.reshape(m, ATTN_W), odf.reshape(m, DN_V_W), odb.reshape(m, DN_V_W), dz, op.reshape(m, POOL_W),
                  gates, hf,
                  dn_norm_w, w_branch, w_out, ln_g, ln_b)


def kernel(x, w_in, attn_sink, rel_bias, conv_w, a_log, dt_bias, dn_norm_w, pool_w, pool_scale, w_branch, w_out,
           ln1_g, ln1_b, ln2_g, ln2_b, ffn_w1, ffn_w3, ffn_w2, router_w, moe_wg, moe_wu, moe_wd):
    bsz, seq, _ = x.shape
    m = bsz * seq
    bias_tabs = _attn_bias_tables(rel_bias)
    hf = x.reshape(m, D_MODEL).astype(f32)
    hb = hf.astype(bf16)
    for layer in range(DEPTH):
        hf, hb = _mixer(hf, hb, bsz, seq, bias_tabs, w_in[layer], attn_sink[layer], conv_w[layer], a_log[layer],
                        dt_bias[layer], dn_norm_w[layer], pool_w[layer], pool_scale[layer], w_branch[layer],
                        w_out[layer], ln1_g[layer], ln1_b[layer])
        i = layer // 2
        if layer % 2 == 0:
            hf, hb = _ffn(hb, hf, ffn_w1[i], ffn_w3[i], ffn_w2[i], ln2_g[layer], ln2_b[layer])
        else:
            hf = _moe(hf, router_w[i], moe_wg[i], moe_wu[i], moe_wd[i], ln2_g[layer], ln2_b[layer])
    return hf.reshape(bsz, seq, D_MODEL).astype(x.dtype)
```

```python
import functools

import numpy as np
import jax
import jax.numpy as jnp
from jax import lax
from jax.experimental import pallas as pl
from jax.experimental.pallas import tpu as pltpu
from jax.experimental.pallas import tpu_sc as plsc

f32 = jnp.float32
bf16 = jnp.bfloat16

D_MODEL = 1024
DEPTH = 2
HEAD_DIM = 64
ATTN_Q_HEADS = 8
ATTN_KV_HEADS = 2
ATTN_GROUP = ATTN_Q_HEADS // ATTN_KV_HEADS
WINDOW = 128
BLOCK = 128
SPAN = BLOCK + 2 * WINDOW
N_BUCKETS = 32
MAX_DISTANCE = 128
DN_HEADS = 8
DN_DK = 64
DN_DV = 64
DN_CONV = 5
DN_CHUNK = 64
POOL_WINDOWS = (2, 4, 8, 16)
N_POOL_GROUPS = 4
POOL_GROUP = 128
ATTN_W = ATTN_Q_HEADS * HEAD_DIM
KV_W = ATTN_KV_HEADS * HEAD_DIM
DN_QK_W = DN_HEADS * DN_DK
DN_V_W = DN_HEADS * DN_DV
POOL_W = N_POOL_GROUPS * POOL_GROUP
N_BRANCH = 3
N_EXPERTS = 8
TOP_K = 2
DEEPNORM_ALPHA = (2.0 * DEPTH) ** 0.25
LN_EPS = 1e-5
RMS_EPS = 1e-6
NEG_INF = -1e30

VMEM_LIMIT_BYTES = 56 * 1024 * 1024
POOL_TILE = 256
POOL_SPAN = 512
CONV_HALO = 8
CONV_ROWS = 128


def _cparams(sem):
    return pltpu.CompilerParams(dimension_semantics=sem, vmem_limit_bytes=VMEM_LIMIT_BYTES)


def _full(shape):
    n = len(shape)
    return pl.BlockSpec(shape, lambda *_: (0,) * n)


def _sigmoid(x):
    return 1.0 / (1.0 + jnp.exp(-x))


def _silu(x):
    return x * _sigmoid(x)


def _layer_norm(x, g, b):
    mu = jnp.mean(x, -1, keepdims=True)
    xc = x - mu
    var = jnp.mean(xc * xc, -1, keepdims=True)
    return xc * lax.rsqrt(var + LN_EPS) * g + b


def _dot(a, b):
    return jnp.dot(a, b, preferred_element_type=f32)


def _dot_nt(a, b):
    return lax.dot_general(a, b, (((1,), (1,)), ((), ())), preferred_element_type=f32)


def _dot_tn(a, b):
    return lax.dot_general(a, b, (((0,), (0,)), ((), ())), preferred_element_type=f32)


def _split3(x):
    hi = x.astype(bf16)
    r = x - hi.astype(f32)
    mid = r.astype(bf16)
    lo = (r - mid.astype(f32)).astype(bf16)
    return hi, mid, lo


def _dot_split2_rhs(x, b_bf):
    hi = x.astype(bf16)
    return _dot(hi, b_bf) + _dot((x - hi.astype(f32)).astype(bf16), b_bf)


def _dot_f32(a, b):
    return jnp.dot(a, b, preferred_element_type=f32, precision=lax.Precision.HIGHEST)


def _inproj_kernel(x_ref, *refs):
    n = len(refs) // 2
    x = x_ref[...]
    for w_ref, o_ref in zip(refs[:n], refs[n:]):
        o_ref[...] = _dot(x, w_ref[...]).astype(o_ref.dtype)


def _inproj(xb, ws, dtypes, tm=256):
    m = xb.shape[0]
    return pl.pallas_call(
        _inproj_kernel,
        grid=(m // tm,),
        in_specs=[pl.BlockSpec((tm, D_MODEL), lambda i: (i, 0))]
        + [pl.BlockSpec(w.shape, lambda i: (0, 0), pipeline_mode=pl.Buffered(1)) for w in ws],
        out_specs=[pl.BlockSpec((tm, w.shape[1]), lambda i: (i, 0)) for w in ws],
        out_shape=[jax.ShapeDtypeStruct((m, w.shape[1]), dt) for w, dt in zip(ws, dtypes)],
        compiler_params=_cparams(("parallel",)),
        name="inproj",
    )(xb, *ws)


def _t5_bucket(rel):
    nb = N_BUCKETS // 2
    max_exact = nb // 2
    n = np.abs(rel)
    large = max_exact + (np.log(np.maximum(n, 1) / max_exact) / np.log(MAX_DISTANCE / max_exact)
                         * (nb - max_exact)).astype(np.int32)
    large = np.minimum(large, nb - 1)
    return (np.where(rel > 0, nb, 0) + np.where(n < max_exact, n, large)).astype(np.int32)


ATTN_KV_COLS = 2 * KV_W + KV_W


def _bias_bucket_tables():
    j = np.arange(SPAN)[:, None]
    r = np.arange(BLOCK)[None, :]
    tabs = []
    for shift in (0, WINDOW, 2 * WINDOW):
        rel = j - shift - r
        tabs.append(np.where(np.abs(rel) <= WINDOW, _t5_bucket(rel), N_BUCKETS))
    return jnp.asarray(np.stack(tabs), dtype=jnp.int32)


def _bias_table_kernel(bucket_ref, rb_ref, o_ref):
    bucket = bucket_ref[0]
    for h in range(ATTN_Q_HEADS):
        acc = jnp.full(bucket.shape, NEG_INF, f32)
        for b in range(N_BUCKETS):
            acc = jnp.where(bucket == b, rb_ref[b, h], acc)
        o_ref[0, h] = acc


def _attn_bias_tables(rel_bias):
    return pl.pallas_call(
        _bias_table_kernel,
        grid=(3,),
        in_specs=[pl.BlockSpec((1, SPAN, BLOCK), lambda v: (v, 0, 0)),
                  pl.BlockSpec(memory_space=pltpu.SMEM)],
        out_specs=pl.BlockSpec((1, ATTN_Q_HEADS, SPAN, BLOCK), lambda v: (v, 0, 0, 0)),
        out_shape=jax.ShapeDtypeStruct((3, ATTN_Q_HEADS, SPAN, BLOCK), f32),
        compiler_params=_cparams(("parallel",)),
        name="attn_bias",
    )(_bias_bucket_tables(), rel_bias.astype(f32))


def _attn_kernel(q_ref, kv_ref, bias_ref, sink_ref, o_ref, *, seq):
    i = pl.program_id(1)
    ks = pl.multiple_of(jnp.clip(i * BLOCK - WINDOW, 0, seq - SPAN), BLOCK)
    kv = kv_ref[0, pl.ds(ks, SPAN), :]
    q = q_ref[0]
    scale = HEAD_DIM ** -0.5
    pair_w = 2 * HEAD_DIM
    low = lax.broadcasted_iota(jnp.int32, (BLOCK, pair_w), 1) < HEAD_DIM
    v_t = kv[:, 2 * KV_W:].astype(f32).T.astype(bf16)
    o_t = []
    for g in range(ATTN_KV_HEADS):
        kk = kv[:, pair_w * g:pair_w * (g + 1)]
        qm = []
        for hl in range(ATTN_GROUP):
            h = ATTN_GROUP * g + hl
            qp = q[:, pair_w * (h // 2):pair_w * (h // 2 + 1)]
            qm.append(jnp.where(low if h % 2 == 0 else jnp.logical_not(low), qp, jnp.zeros_like(qp)))
        s_t = _dot_nt(kk, jnp.concatenate(qm, axis=0))
        p_n = []
        for hl in range(ATTN_GROUP):
            h = ATTN_GROUP * g + hl
            s = s_t[:, BLOCK * hl:BLOCK * (hl + 1)] * scale + bias_ref[0, h]
            sk = sink_ref[0:1, h:h + 1]
            m = jnp.maximum(jnp.max(s, 0, keepdims=True), sk)
            p = jnp.exp(s - m)
            denom = jnp.sum(p, 0, keepdims=True) + jnp.exp(sk - m)
            p_n.append((p * (1.0 / denom)).astype(bf16))
        og = _dot(v_t[HEAD_DIM * g:HEAD_DIM * (g + 1), :], jnp.concatenate(p_n, axis=1))
        o_t += [og[:, BLOCK * hl:BLOCK * (hl + 1)] for hl in range(ATTN_GROUP)]
    o_ref[0] = jnp.concatenate(o_t, axis=0).T.astype(o_ref.dtype)


def _attention(aq, akv, bias_tabs, sink):
    b, s, _ = aq.shape
    nb = s // BLOCK
    assert s % BLOCK == 0 and s >= SPAN

    def variant(bi, i):
        return (jnp.where(i == 0, 0, jnp.where(i == nb - 1, 2, 1)), 0, 0, 0)

    return pl.pallas_call(
        functools.partial(_attn_kernel, seq=s),
        grid=(b, nb),
        in_specs=[pl.BlockSpec((1, BLOCK, ATTN_W), lambda bi, i: (bi, i, 0)),
                  pl.BlockSpec((1, s, ATTN_KV_COLS), lambda bi, i: (bi, 0, 0)),
                  pl.BlockSpec((1, ATTN_Q_HEADS, SPAN, BLOCK), variant),
                  _full((1, ATTN_Q_HEADS))],
        out_specs=pl.BlockSpec((1, BLOCK, ATTN_W), lambda bi, i: (bi, i, 0)),
        out_shape=jax.ShapeDtypeStruct((b, s, ATTN_W), bf16),
        compiler_params=_cparams(("parallel", "arbitrary")),
        name="attn",
    )(aq, akv, bias_tabs, sink.reshape(1, ATTN_Q_HEADS).astype(f32))


def _dn_prep_kernel(u_ref, w_ref, ones_ref, o_ref, pad_ref, *, seq):
    sec = pl.program_id(1)
    zeros = jnp.zeros((CONV_HALO, DN_QK_W), f32)
    pad_ref[0:CONV_HALO, :] = zeros
    pad_ref[seq + CONV_HALO:seq + 2 * CONV_HALO, :] = zeros
    pad_ref[CONV_HALO:seq + CONV_HALO, :] = u_ref[0].astype(f32)
    n = CONV_ROWS + 2 * CONV_HALO

    def conv_silu(t0):
        win = pad_ref[t0:t0 + n, :]
        z = [win * w_ref[0, k:k + 1, :] for k in range(DN_CONV)]
        after = pltpu.roll(z[3] + pltpu.roll(z[4], n - 1, axis=0), n - 1, axis=0)
        before = pltpu.roll(z[1] + pltpu.roll(z[0], 1, axis=0), 1, axis=0)
        return _silu((z[2] + after + before)[CONV_HALO:CONV_HALO + CONV_ROWS])

    @pl.when(sec < 2)
    def _():
        scale = jnp.where(sec == 0, DN_DK ** -0.5, 1.0)
        for t0 in range(0, seq, CONV_ROWS):
            y = conv_silu(t0)
            ss = _dot_split2_rhs(y * y, ones_ref[...])
            o_ref[0, t0:t0 + CONV_ROWS, :] = y * lax.rsqrt(ss + 1e-6) * scale

    @pl.when(sec == 2)
    def _():
        for t0 in range(0, seq, CONV_ROWS):
            o_ref[0, t0:t0 + CONV_ROWS, :] = conv_silu(t0)


def _head_ones():
    blk = np.arange(DN_QK_W) // DN_DK
    return jnp.asarray(blk[:, None] == blk[None, :], dtype=bf16)


def _dn_prep(dqkv, conv_w):
    b, s, _ = dqkv.shape
    w3 = jnp.transpose(conv_w.reshape(DN_CONV, 3, DN_QK_W), (1, 0, 2)).astype(f32)
    return pl.pallas_call(
        functools.partial(_dn_prep_kernel, seq=s),
        grid=(b, 3),
        in_specs=[pl.BlockSpec((1, s, DN_QK_W), lambda bi, c: (bi, 0, c)),
                  pl.BlockSpec((1, DN_CONV, DN_QK_W), lambda bi, c: (c, 0, 0)),
                  _full((DN_QK_W, DN_QK_W))],
        out_specs=pl.BlockSpec((1, s, DN_QK_W), lambda bi, c: (bi, 0, c)),
        out_shape=jax.ShapeDtypeStruct(dqkv.shape, f32),
        scratch_shapes=[pltpu.VMEM((s + 2 * CONV_HALO, DN_QK_W), f32)],
        compiler_params=_cparams(("parallel", "arbitrary")),
        name="dn_prep",
    )(dqkv, w3, _head_ones())


def _softplus(x):
    return jnp.maximum(x, 0.0) + jnp.log1p(jnp.exp(-jnp.abs(x)))


QUAD = 4
QUAD_W = QUAD * DN_DK
N_QUADS = DN_HEADS // QUAD
DELTA_BLOCK_CHUNKS = 4


def _block_diag(xq):
    lane_head = lax.broadcasted_iota(jnp.int32, xq.shape, 1) // DN_DK
    return jnp.concatenate([jnp.where(lane_head == h, xq, 0.0).astype(bf16) for h in range(QUAD)], axis=0)


def _delta_chunk(q, k, v, gx, bx, d):
    C = DN_CHUNK
    W = DN_QK_W
    sign = 1 - 2 * d
    ii = lax.broadcasted_iota(jnp.int32, (C, W), 0)
    jl = lax.broadcasted_iota(jnp.int32, (C, W), 1) % DN_DK
    order = (ii - jl) * sign
    incl = order >= 0
    strict = order > 0
    rr = lax.broadcasted_iota(jnp.int32, (2 * C, 2 * C), 0)
    cc = lax.broadcasted_iota(jnp.int32, (2 * C, 2 * C), 1)
    tri = jnp.where(((rr % C) - cc) * sign >= 0, 1.0, 0.0)
    lmat = jnp.where(cc < C, tri, jnp.where(rr < C, -1.0, 0.0)).astype(bf16)
    rmat = jnp.concatenate([gx, jnp.where(order <= 0, gx, 0.0)], axis=0)
    dg = _dot(jnp.concatenate([lmat] * 3, axis=1), jnp.concatenate(_split3(rmat), axis=0))
    dmat, gc = dg[:C], dg[C:]
    decay = jnp.where(incl, jnp.exp(jnp.where(incl, dmat, 0.0)), 0.0)
    eg = jnp.exp(gc)
    last = C - 1 if d == 0 else 0
    gl = gc[last:last + 1, :]
    kdec = k * jnp.exp(gl - gc)
    cd = jnp.exp(gl)
    kb = k * bx
    eye = jnp.where(ii == jl, 1.0, 0.0)
    return dict(q=q, k=k, bx=bx, decay=decay, strict=strict, eye=eye, kdec=kdec, cd=cd,
                vb=v * bx, kbe=kb * eg, qdec=q * eg)


DELTA_CHAINS = 2 * DELTA_BLOCK_CHUNKS * N_QUADS


def _delta_kernel(qf_ref, kf_ref, vf_ref, qb_ref, kb_ref, vb_ref, abf_ref, abb_ref, alog_ref, dtb_ref,
                  of_ref, ob_ref, state_ref, u_s, w_s, qk_s, qdec_s, kdec_s, cd_s, *, nblk):
    C, H, NC = DN_CHUNK, DN_HEADS, DELTA_BLOCK_CHUNKS
    t = pl.program_id(0)
    slot_w = t % 2
    slot_r = 1 - slot_w

    @pl.when(t == 0)
    def _():
        for ref in (u_s, w_s, qk_s, qdec_s, kdec_s, cd_s):
            ref[1] = jnp.zeros(ref.shape[1:], ref.dtype)

    @pl.when((t + nblk - 1) % nblk == 0)
    def _():
        state_ref[...] = jnp.zeros(state_ref.shape, f32)

    def expand(x, parts):
        rows = lax.broadcasted_iota(jnp.int32, (H * parts, DN_QK_W), 0) % H
        cols = lax.broadcasted_iota(jnp.int32, (H * parts, DN_QK_W), 1) // DN_DK
        emat = jnp.where(rows == cols, 1.0, 0.0).astype(bf16)
        pieces = [p.astype(f32) for p in _split3(x)[:parts]]
        return _dot(jnp.concatenate(pieces, axis=-1).astype(bf16), emat)

    chains = []

    def stage_prepare():
        dirs = ((0, qf_ref, kf_ref, vf_ref, abf_ref), (1, qb_ref, kb_ref, vb_ref, abb_ref))
        for d, q_ref, k_ref, v_ref, ab_ref in dirs:
            ab = ab_ref[0, 0]
            g = -jnp.exp(alog_ref[d]) * _softplus(ab[:, :H] + dtb_ref[d])
            gx = expand(g, 3)
            bx = expand(_sigmoid(ab[:, H:]), 2)
            for c in range(NC):
                rows = slice(C * c, C * (c + 1))
                pre = _delta_chunk(q_ref[0, rows, :], k_ref[0, rows, :], v_ref[0, rows, :], gx[rows], bx[rows], d)
                for qd in range(N_QUADS):
                    sl = slice(QUAD_W * qd, QUAD_W * (qd + 1))
                    chains.append({n: x[:, sl] for n, x in pre.items()})

    def stage_gram():
        for ch in chains:
            kq = _dot_nt(jnp.concatenate([ch["k"], ch["q"]], axis=0).astype(bf16), _block_diag(ch["k"]))
            a_mat = jnp.where(ch["strict"], kq[:C] * ch["bx"] * ch["decay"], 0.0)
            ch["qk"] = kq[C:] * ch["decay"]
            ch["p"] = -a_mat
            ch["t"] = ch["eye"] - a_mat

    def stage_square():
        for ch in chains:
            ch["p"] = _dot(ch["p"].astype(bf16), _block_diag(ch["p"]))

    def stage_double():
        for ch in chains:
            pt = _dot(jnp.concatenate([ch["p"], ch["t"]], axis=0).astype(bf16), _block_diag(ch["p"]))
            ch["p"], ch["t"] = pt[:C], ch["t"] + pt[C:]

    def stage_last():
        for ch in chains:
            ch["tb"] = (ch["t"] + _dot(ch["t"].astype(bf16), _block_diag(ch["p"]))).astype(bf16)

    def stage_solve():
        for ch in chains:
            ch["u"] = _dot(ch["tb"], _block_diag(ch["vb"]))
            ch["w"] = _dot(ch["tb"], _block_diag(ch["kbe"]))

    row_head = lax.broadcasted_iota(jnp.int32, (QUAD_W, QUAD_W), 0) // DN_DK
    col_head = lax.broadcasted_iota(jnp.int32, (QUAD_W, QUAD_W), 1) // DN_DK
    o_refs = (of_ref, ob_ref)
    live = {}

    def scan_chains(step):
        for d in range(2):
            c = step if d == 0 else NC - 1 - step
            for qd in range(N_QUADS):
                yield d, c, qd, (d * NC + c) * N_QUADS + qd

    def scan_first(step):
        for d, c, qd, n in scan_chains(step):
            st = state_ref[d, qd]
            wq = _dot(jnp.concatenate([w_s[slot_r, n], qdec_s[slot_r, n]], axis=0), st.astype(bf16))
            live[n] = (st, u_s[slot_r, n] - wq[:C], wq[C:])

    def scan_second(step):
        for d, c, qd, n in scan_chains(step):
            st, v_new, qs = live.pop(n)
            o = qs + _dot(qk_s[slot_r, n], _block_diag(v_new))
            o_refs[d][0, C * c:C * (c + 1), QUAD_W * qd:QUAD_W * (qd + 1)] = o
            s_new = _dot_tn(kdec_s[slot_r, n], v_new.astype(bf16))
            state_ref[d, qd] = st * cd_s[slot_r, n, 0:1, :] + jnp.where(row_head == col_head, s_new, 0.0)

    local = [stage_prepare, stage_gram, stage_square] + [stage_double] * 4 + [stage_last, stage_solve]
    scan = [f for step in range(NC) for f in (functools.partial(scan_first, step),
                                              functools.partial(scan_second, step))]
    for k in range(max(len(local), len(scan))):
        if k < len(local):
            local[k]()
        if k < len(scan):
            scan[k]()

    for n, ch in enumerate(chains):
        u_s[slot_w, n] = ch["u"]
        w_s[slot_w, n] = ch["w"].astype(bf16)
        qk_s[slot_w, n] = ch["qk"].astype(bf16)
        qdec_s[slot_w, n] = ch["qdec"].astype(bf16)
        kdec_s[slot_w, n] = ch["kdec"].astype(bf16)
        cd_s[slot_w, n] = jnp.broadcast_to(ch["cd"], (8, QUAD_W))


def _delta(qkvn, dab, a_log, dt_bias):
    b, s, _ = qkvn.shape
    H = DN_HEADS
    tb = DN_CHUNK * DELTA_BLOCK_CHUNKS
    nblk = s // tb
    total = b * nblk
    assert s % tb == 0
    d4 = dab.reshape(b, s, 4, H)
    ab = jnp.stack([jnp.concatenate([d4[:, :, r], d4[:, :, 2 + r]], -1) for r in range(2)])
    alog = a_log.astype(f32).reshape(2, 1, H)
    dtb = dt_bias.astype(f32).reshape(2, 1, H)

    def block(lin, rev):
        i = lin % nblk
        return lin // nblk, (nblk - 1 - i) if rev else i

    def sec(n, rev, lag):
        def index(t):
            bi, i = block(jnp.maximum(t - 1, 0) if lag else jnp.minimum(t, total - 1), rev)
            return bi, i, n
        return pl.BlockSpec((1, tb, DN_QK_W), index)

    def ab_spec(d):
        def index(t):
            bi, i = block(jnp.minimum(t, total - 1), d == 1)
            return d, bi, i, 0
        return pl.BlockSpec((1, 1, tb, 2 * H), index)

    chain_buf = lambda dt: pltpu.VMEM((2, DELTA_CHAINS, DN_CHUNK, QUAD_W), dt)
    return pl.pallas_call(
        functools.partial(_delta_kernel, nblk=nblk),
        grid=(total + 1,),
        in_specs=[sec(0, False, False), sec(1, False, False), sec(2, False, False),
                  sec(0, True, False), sec(1, True, False), sec(2, True, False),
                  ab_spec(0), ab_spec(1), _full((2, 1, H)), _full((2, 1, H))],
        out_specs=[sec(0, False, True), sec(0, True, True)],
        out_shape=[jax.ShapeDtypeStruct((b, s, DN_V_W), f32)] * 2,
        scratch_shapes=[pltpu.VMEM((2, N_QUADS, QUAD_W, QUAD_W), f32),
                        chain_buf(f32), chain_buf(bf16), chain_buf(bf16), chain_buf(bf16), chain_buf(bf16),
                        pltpu.VMEM((2, DELTA_CHAINS, 8, QUAD_W), f32)],
        compiler_params=_cparams(("arbitrary",)),
        name="delta",
    )(qkvn, qkvn, qkvn, qkvn, qkvn, qkvn, ab, ab, alog, dtb)


def _pool_band_tables():
    r = np.arange(POOL_TILE)[:, None]
    j = np.arange(POOL_SPAN)[None, :]
    tabs = np.zeros((3, N_POOL_GROUPS, POOL_TILE, POOL_SPAN), np.float32)
    for vi, shift in enumerate((0, 128, 256)):
        rel = j - shift - r
        for gi, w in enumerate(POOL_WINDOWS):
            tabs[vi, gi] = np.abs(rel) <= w // 2
    return jnp.asarray(tabs, dtype=bf16)


def _pool_kernel(u_ref, band_ref, w_ref, scale_ref, o_ref, *, seq):
    t = pl.program_id(1)
    t0 = pl.multiple_of(t * POOL_TILE, POOL_TILE)
    ks = pl.multiple_of(jnp.clip(t0 - 128, 0, seq - POOL_SPAN), 128)
    win = u_ref[0, pl.ds(ks, POOL_SPAN), :]
    own = u_ref[0, pl.ds(t0, POOL_TILE), :].astype(f32)
    pos = t0 + lax.broadcasted_iota(jnp.int32, (POOL_TILE, POOL_GROUP), 0)
    outs = []
    for gi, w in enumerate(POOL_WINDOWS):
        r = w // 2
        sl = slice(POOL_GROUP * gi, POOL_GROUP * (gi + 1))
        wsum = _dot(band_ref[0, gi], win[:, sl])
        count = (jnp.minimum(pos + r, seq - 1) - jnp.maximum(pos - r, 0) + 1).astype(f32)
        mixed = wsum / count - own[:, sl]
        outs.append(_dot(mixed.astype(bf16), w_ref[gi]))
    o_ref[0] = (jnp.concatenate(outs, -1) * scale_ref[...]).astype(o_ref.dtype)


def _pool(pu, pool_w, pool_scale):
    b, s, _ = pu.shape
    nt = s // POOL_TILE
    assert s % POOL_TILE == 0 and s >= POOL_SPAN

    def variant(bi, t):
        return (jnp.where(t == 0, 0, jnp.where(t == nt - 1, 2, 1)), 0, 0, 0)

    return pl.pallas_call(
        functools.partial(_pool_kernel, seq=s),
        grid=(b, nt),
        in_specs=[pl.BlockSpec((1, s, POOL_W), lambda bi, t: (bi, 0, 0)),
                  pl.BlockSpec((1, N_POOL_GROUPS, POOL_TILE, POOL_SPAN), variant),
                  _full((N_POOL_GROUPS, POOL_GROUP, POOL_GROUP)),
                  _full((1, POOL_W))],
        out_specs=pl.BlockSpec((1, POOL_TILE, POOL_W), lambda bi, t: (bi, t, 0)),
        out_shape=jax.ShapeDtypeStruct((b, s, POOL_W), bf16),
        compiler_params=_cparams(("parallel", "arbitrary")),
        name="pool",
    )(pu, _pool_band_tables(), pool_w.astype(bf16), pool_scale.reshape(1, POOL_W).astype(f32))


def _merge_kernel(oa_ref, odf_ref, odb_ref, z_ref, op_ref, gate_ref, h_ref, ones_ref, nw_ref, wb_ref, wo_ref,
                  g_ref, b_ref, hf_ref, hb_ref):
    od = odf_ref[...] + odb_ref[...]
    ms = _dot_split2_rhs(od * od, ones_ref[...]) * (1.0 / DN_DV)
    od = od * lax.rsqrt(ms + RMS_EPS) * nw_ref[...] * _silu(z_ref[...].astype(f32))
    branches = (oa_ref[...], od.astype(bf16), op_ref[...])
    merged = None
    for n in range(N_BRANCH):
        up = _dot(branches[n], wb_ref[n])
        gate = _sigmoid(gate_ref[:, D_MODEL * n:D_MODEL * (n + 1)].astype(f32))
        merged = gate * up if merged is None else merged + gate * up
    y = _dot(merged.astype(bf16), wo_ref[...])
    hn = _layer_norm(DEEPNORM_ALPHA * h_ref[...] + y, g_ref[...], b_ref[...])
    hf_ref[...] = hn
    hb_ref[...] = hn.astype(bf16)


def _merge(oa, odf, odb, z, op, gates, h, dn_norm_w, w_branch, w_out, ln_g, ln_b, tm=256):
    m = h.shape[0]
    row = lambda w: pl.BlockSpec((tm, w), lambda i: (i, 0))
    nw = jnp.tile(dn_norm_w.astype(f32), DN_HEADS).reshape(1, DN_V_W)
    return pl.pallas_call(
        _merge_kernel,
        grid=(m // tm,),
        in_specs=[row(ATTN_W), row(DN_V_W), row(DN_V_W), row(DN_V_W), row(POOL_W),
                  row(N_BRANCH * D_MODEL), row(D_MODEL),
                  _full((DN_V_W, DN_V_W)), _full((1, DN_V_W)),
                  _full((N_BRANCH, ATTN_W, D_MODEL)), _full((D_MODEL, D_MODEL)),
                  _full((1, D_MODEL)), _full((1, D_MODEL))],
        out_specs=[row(D_MODEL), row(D_MODEL)],
        out_shape=[jax.ShapeDtypeStruct((m, D_MODEL), f32), jax.ShapeDtypeStruct((m, D_MODEL), bf16)],
        compiler_params=_cparams(("parallel",)),
        name="merge",
    )(oa, odf, odb, z, op, gates, h, _head_ones(), nw, w_branch.astype(bf16), w_out.astype(bf16),
      ln_g.reshape(1, D_MODEL).astype(f32), ln_b.reshape(1, D_MODEL).astype(f32))


def _swiglu_step(x, wg_ref, wu_ref, wd_ref, acc_ref):
    g = _dot(x, wg_ref[0].astype(bf16))
    u = _dot(x, wu_ref[0].astype(bf16))
    acc_ref[...] += _dot((_silu(g) * u).astype(bf16), wd_ref[0].astype(bf16))


def _ffn_kernel(x_ref, h_ref, wg_ref, wu_ref, wd_ref, g_ref, b_ref, hf_ref, hb_ref, acc_ref):
    j = pl.program_id(1)

    @pl.when(j == 0)
    def _():
        acc_ref[...] = jnp.zeros(acc_ref.shape, f32)

    _swiglu_step(x_ref[...], wg_ref, wu_ref, wd_ref, acc_ref)

    @pl.when(j == pl.num_programs(1) - 1)
    def _():
        hn = _layer_norm(DEEPNORM_ALPHA * h_ref[...] + acc_ref[...], g_ref[...], b_ref[...])
        hf_ref[...] = hn
        hb_ref[...] = hn.astype(bf16)


def _ffn(hb, hf, w1, w3, w2, ln_g, ln_b, tm=1024, tf=256):
    m = hb.shape[0]
    tm = min(tm, m)
    ff = w1.shape[-1]
    row = pl.BlockSpec((tm, D_MODEL), lambda i, j: (i, 0))
    return pl.pallas_call(
        _ffn_kernel,
        grid=(m // tm, ff // tf),
        in_specs=[row, row,
                  pl.BlockSpec((1, D_MODEL, tf), lambda i, j: (0, 0, j)),
                  pl.BlockSpec((1, D_MODEL, tf), lambda i, j: (0, 0, j)),
                  pl.BlockSpec((1, tf, D_MODEL), lambda i, j: (0, j, 0)),
                  pl.BlockSpec((1, D_MODEL), lambda i, j: (0, 0)),
                  pl.BlockSpec((1, D_MODEL), lambda i, j: (0, 0))],
        out_specs=[row, row],
        out_shape=[jax.ShapeDtypeStruct((m, D_MODEL), f32), jax.ShapeDtypeStruct((m, D_MODEL), bf16)],
        scratch_shapes=[pltpu.VMEM((tm, D_MODEL), f32)],
        compiler_params=_cparams(("parallel", "arbitrary")),
        name="ffn",
    )(hb, hf, w1[None], w3[None], w2[None],
      ln_g.reshape(1, D_MODEL).astype(f32), ln_b.reshape(1, D_MODEL).astype(f32))


MOE_TILE = 1024
SC_CORES = 2
SC_SUBCORES = 16
SC_WORKERS = SC_CORES * SC_SUBCORES
SC_ROWS = 64


def _gmm_kernel(te_ref, tv_ref, x_ref, wg_ref, wu_ref, wd_ref, o_ref, acc_ref):
    i = pl.program_id(0)
    j = pl.program_id(1)
    valid = tv_ref[i]

    @pl.when(j == 0)
    def _():
        acc_ref[...] = jnp.zeros(acc_ref.shape, f32)

    @pl.when(valid > 0)
    def _():
        rows = lax.broadcasted_iota(jnp.int32, x_ref.shape, 0)
        x = jnp.where(rows < valid, x_ref[...], 0.0).astype(bf16)
        _swiglu_step(x, wg_ref, wu_ref, wd_ref, acc_ref)

    @pl.when(j == pl.num_programs(1) - 1)
    def _():
        o_ref[...] = acc_ref[...]


def _gmm(xs, tile_expert, tile_valid, wg, wu, wd, tm, tf=512):
    n = xs.shape[0]
    ff = wg.shape[-1]
    nj = ff // tf

    def fcol(i, j, te, tv):
        return jnp.where(tv[i] > 0, j, nj - 1)

    grid_spec = pltpu.PrefetchScalarGridSpec(
        num_scalar_prefetch=2,
        grid=(n // tm, nj),
        in_specs=[pl.BlockSpec((tm, D_MODEL), lambda i, j, te, tv: (i, 0)),
                  pl.BlockSpec((1, D_MODEL, tf), lambda i, j, te, tv: (te[i], 0, fcol(i, j, te, tv))),
                  pl.BlockSpec((1, D_MODEL, tf), lambda i, j, te, tv: (te[i], 0, fcol(i, j, te, tv))),
                  pl.BlockSpec((1, tf, D_MODEL), lambda i, j, te, tv: (te[i], fcol(i, j, te, tv), 0))],
        out_specs=pl.BlockSpec((tm, D_MODEL), lambda i, j, te, tv: (i, 0)),
        scratch_shapes=[pltpu.VMEM((tm, D_MODEL), f32)])
    return pl.pallas_call(
        _gmm_kernel,
        grid_spec=grid_spec,
        out_shape=jax.ShapeDtypeStruct((n, D_MODEL), f32),
        compiler_params=_cparams(("parallel", "arbitrary")),
        name="moe_gmm",
    )(tile_expert, tile_valid, xs, wg, wu, wd)


def _router_kernel(x_ref, w_ref, mi_ref, mf_ref, cnt_ref, carry_ref):
    @pl.when(pl.program_id(0) == 0)
    def _():
        carry_ref[...] = jnp.zeros(carry_ref.shape, f32)

    logits = _dot_f32(x_ref[...], w_ref[...])
    tm = logits.shape[0]
    lane = lax.broadcasted_iota(jnp.int32, logits.shape, 1)
    m1 = jnp.max(logits, -1, keepdims=True)
    i1 = jnp.min(jnp.where(logits == m1, lane, N_EXPERTS), -1, keepdims=True)
    rest = jnp.where(lane == i1, -jnp.inf, logits)
    m2 = jnp.max(rest, -1, keepdims=True)
    i2 = jnp.min(jnp.where(rest == m2, lane, N_EXPERTS), -1, keepdims=True)
    e2 = jnp.exp(m2 - m1)
    p1 = 1.0 / (1.0 + e2)
    p2 = e2 / (1.0 + e2)
    oh1 = jnp.where(lane == i1, 1.0, 0.0)
    oh2 = jnp.where(lane == i2, 1.0, 0.0)
    both = oh1 + oh2
    rr = lax.broadcasted_iota(jnp.int32, (tm, tm), 0)
    cc = lax.broadcasted_iota(jnp.int32, (tm, tm), 1)
    earlier = jnp.where(cc < rr, 1.0, 0.0).astype(bf16)
    before = _dot(earlier, both.astype(bf16)) + carry_ref[...]
    r1 = jnp.sum(oh1 * before, -1, keepdims=True).astype(jnp.int32)
    r2 = jnp.sum(oh2 * before, -1, keepdims=True).astype(jnp.int32)
    carry_ref[...] += jnp.sum(both, 0, keepdims=True)
    cnt_ref[...] = carry_ref[...].astype(jnp.int32)
    mi_ref[...] = jnp.where(lane == 0, i1, jnp.where(lane == 1, i2, jnp.where(lane == 2, r1,
                                                                                  jnp.where(lane == 3, r2, 0))))
    mf_ref[...] = jnp.where(lane == 0, p1, jnp.where(lane == 1, p2, 0.0))


def _router(hf, router_w, tm=512):
    m = hf.shape[0]
    tm = min(tm, m)
    return pl.pallas_call(
        _router_kernel,
        grid=(m // tm,),
        in_specs=[pl.BlockSpec((tm, D_MODEL), lambda i: (i, 0)), _full((D_MODEL, N_EXPERTS))],
        out_specs=[pl.BlockSpec((tm, N_EXPERTS), lambda i: (i, 0)),
                   pl.BlockSpec((tm, N_EXPERTS), lambda i: (i, 0)),
                   _full((1, N_EXPERTS))],
        out_shape=[jax.ShapeDtypeStruct((m, N_EXPERTS), jnp.int32),
                   jax.ShapeDtypeStruct((m, N_EXPERTS), f32),
                   jax.ShapeDtypeStruct((1, N_EXPERTS), jnp.int32)],
        scratch_shapes=[pltpu.VMEM((1, N_EXPERTS), f32)],
        compiler_params=_cparams(("arbitrary",)),
        name="router",
    )(hf, router_w.astype(f32))


def _sc_mesh():
    return plsc.VectorSubcoreMesh(core_axis_name="c", subcore_axis_name="s",
                                  num_cores=SC_CORES, num_subcores=SC_SUBCORES)


def _sc_worker_base(per_worker):
    return (lax.axis_index("s") * SC_CORES + lax.axis_index("c")) * per_worker


def _sc_scatter_rows(src, idx0, idx1, n_out):
    m, d = src.shape
    per_worker = m // SC_WORKERS
    assert m % (SC_WORKERS * SC_ROWS) == 0

    def body(src_hbm, idx0_hbm, idx1_hbm, out_hbm, idx_v, rows_v, sem):
        base = _sc_worker_base(per_worker)

        @pl.loop(0, per_worker // SC_ROWS)
        def _(j):
            off = pl.multiple_of(base + j * SC_ROWS, SC_ROWS)
            pltpu.sync_copy(src_hbm.at[pl.ds(off, SC_ROWS)], rows_v)
            for idx_hbm in (idx0_hbm, idx1_hbm):
                pltpu.sync_copy(idx_hbm.at[pl.ds(off, SC_ROWS)], idx_v)
                pltpu.async_copy(rows_v, out_hbm.at[idx_v], sem).wait()

    return pl.kernel(
        body, out_type=jax.ShapeDtypeStruct((n_out, d), src.dtype), mesh=_sc_mesh(),
        scratch_types=[pltpu.VMEM((SC_ROWS,), jnp.int32), pltpu.VMEM((SC_ROWS, d), src.dtype),
                       pltpu.SemaphoreType.DMA],
        name="moe_dispatch",
    )(src, idx0, idx1)


def _sc_gather_rows(table, idx):
    n = idx.shape[0]
    d = table.shape[1]
    per_worker = n // SC_WORKERS
    assert n % (SC_WORKERS * SC_ROWS) == 0

    def body(table_hbm, idx_hbm, out_hbm, idx_v, rows_v, sem):
        base = _sc_worker_base(per_worker)

        @pl.loop(0, per_worker // SC_ROWS)
        def _(j):
            off = pl.multiple_of(base + j * SC_ROWS, SC_ROWS)
            pltpu.sync_copy(idx_hbm.at[pl.ds(off, SC_ROWS)], idx_v)
            pltpu.async_copy(table_hbm.at[idx_v], rows_v, sem).wait()
            pltpu.sync_copy(rows_v, out_hbm.at[pl.ds(off, SC_ROWS)])

    return pl.kernel(
        body, out_type=jax.ShapeDtypeStruct((n, d), table.dtype), mesh=_sc_mesh(),
        scratch_types=[pltpu.VMEM((SC_ROWS,), jnp.int32), pltpu.VMEM((SC_ROWS, d), table.dtype),
                       pltpu.SemaphoreType.DMA],
        name="moe_collect",
    )(table, idx)


def _combine_kernel(y0_ref, y1_ref, mf_ref, h_ref, g_ref, b_ref, o_ref):
    p = mf_ref[...]
    y = p[:, 0:1] * y0_ref[0] + p[:, 1:2] * y1_ref[0]
    o_ref[...] = _layer_norm(DEEPNORM_ALPHA * h_ref[...] + y, g_ref[...], b_ref[...])


def _combine(yg, mf, hf, ln_g, ln_b, tm=512):
    m = hf.shape[0]
    tm = min(tm, m)
    return pl.pallas_call(
        _combine_kernel,
        grid=(m // tm,),
        in_specs=[pl.BlockSpec((1, tm, D_MODEL), lambda i: (0, i, 0)),
                  pl.BlockSpec((1, tm, D_MODEL), lambda i: (1, i, 0)),
                  pl.BlockSpec((tm, N_EXPERTS), lambda i: (i, 0)),
                  pl.BlockSpec((tm, D_MODEL), lambda i: (i, 0)),
                  _full((1, D_MODEL)), _full((1, D_MODEL))],
        out_specs=pl.BlockSpec((tm, D_MODEL), lambda i: (i, 0)),
        out_shape=jax.ShapeDtypeStruct((m, D_MODEL), f32),
        compiler_params=_cparams(("parallel",)),
        name="moe_combine",
    )(yg, yg, mf, hf, ln_g.reshape(1, D_MODEL).astype(f32), ln_b.reshape(1, D_MODEL).astype(f32))


def _moe(hf, router_w, wg, wu, wd, ln_g, ln_b):
    m = hf.shape[0]
    tm = MOE_TILE
    mi, mf, cnt = _router(hf, router_w)
    counts = cnt[0]
    padded = (counts + tm - 1) // tm * tm
    ends = jnp.cumsum(padded)
    starts = ends - padded
    experts = jnp.arange(N_EXPERTS, dtype=jnp.int32)
    pos = jnp.sum(jnp.where(mi[:, 0:TOP_K, None] == experts, starts, 0), -1) + mi[:, TOP_K:2 * TOP_K]
    n_tiles = TOP_K * m // tm + N_EXPERTS
    tile_start = jnp.arange(n_tiles, dtype=jnp.int32) * tm
    tile_expert = jnp.minimum(jnp.sum(tile_start[:, None] >= ends[None, :], -1), N_EXPERTS - 1).astype(jnp.int32)
    tile_valid = jnp.clip(starts[tile_expert] + counts[tile_expert] - tile_start, 0, tm).astype(jnp.int32)
    tile_valid = jnp.where(tile_start < ends[-1], tile_valid, 0)
    xs = _sc_scatter_rows(hf, pos[:, 0], pos[:, 1], n_tiles * tm)
    ys = _gmm(xs, tile_expert, tile_valid, wg, wu, wd, tm)
    yg = _sc_gather_rows(ys, jnp.concatenate([pos[:, 0], pos[:, 1]]))
    return _combine(yg.reshape(TOP_K, m, D_MODEL), mf, hf, ln_g, ln_b)


_SPLITS = (ATTN_W, KV_W, KV_W, 3 * DN_QK_W, DN_V_W, 4 * DN_HEADS, POOL_W, N_BRANCH * D_MODEL)
_SPLIT_DTYPES = (bf16, bf16, f32, bf16, f32, bf16, bf16)


def _mixer(hf, hb, bsz, seq, bias_tabs, w_in, sink, conv_w, a_log, dt_bias, dn_norm_w, pool_w, pool_scale,
           w_branch, w_out, ln_g, ln_b):
    points = np.cumsum(_SPLITS)[:-1].tolist()
    wq, wk, wv, *rest = jnp.split(w_in.astype(bf16), points, axis=-1)
    wkv = jnp.concatenate([wk[:, HEAD_DIM * g:HEAD_DIM * (g + 1)] for g in range(ATTN_KV_HEADS) for _ in range(2)]
                          + [wv], axis=-1)
    aq, akv, dqkv, dz, dab, pu, gates = _inproj(hb, [wq, wkv] + rest, _SPLIT_DTYPES)
    shp = lambda t: t.reshape(bsz, seq, t.shape[-1])
    oa = _attention(shp(aq), shp(akv), bias_tabs, sink)
    qkvn = _dn_prep(shp(dqkv), conv_w)
    odf, odb = _delta(qkvn, shp(dab), a_log, dt_bias)
    op = _pool(shp(pu), pool_w, pool_scale)
    m = bsz * seq
    return _merge(oa.reshape(m, ATTN_W), odf.reshape(m, DN_V_W), odb.reshape(m, DN_V_W), dz, op.reshape(m, POOL_W),
                  gates, hf,
                  dn_norm_w, w_branch, w_out, ln_g, ln_b)


def kernel(x, w_in, attn_sink, rel_bias, conv_w, a_log, dt_bias, dn_norm_w, pool_w, pool_scale, w_branch, w_out,
           ln1_g, ln1_b, ln2_g, ln2_b, ffn_w1, ffn_w3, ffn_w2, router_w, moe_wg, moe_wu, moe_wd):
    bsz, seq, _ = x.shape
    m = bsz * seq
    bias_tabs = _attn_bias_tables(rel_bias)
    hf = x.reshape(m, D_MODEL).astype(f32)
    hb = hf.astype(bf16)
    for layer in range(DEPTH):
        hf, hb = _mixer(hf, hb, bsz, seq, bias_tabs, w_in[layer], attn_sink[layer], conv_w[layer], a_log[layer],
                        dt_bias[layer], dn_norm_w[layer], pool_w[layer], pool_scale[layer], w_branch[layer],
                        w_out[layer], ln1_g[layer], ln1_b[layer])
        i = layer // 2
        if layer % 2 == 0:
            hf, hb = _ffn(hb, hf, ffn_w1[i], ffn_w3[i], ffn_w2[i], ln2_g[layer], ln2_b[layer])
        else:
            hf = _moe(hf, router_w[i], moe_wg[i], moe_wu[i], moe_wd[i], ln2_g[layer], ln2_b[layer])
    return hf.reshape(bsz, seq, D_MODEL).astype(x.dtype)
```

```python
import functools

import numpy as np
import jax
import jax.numpy as jnp
from jax import lax
from jax.experimental import pallas as pl
from jax.experimental.pallas import tpu as pltpu
from jax.experimental.pallas import tpu_sc as plsc

f32 = jnp.float32
bf16 = jnp.bfloat16

D_MODEL = 1024
DEPTH = 2
HEAD_DIM = 64
ATTN_Q_HEADS = 8
ATTN_KV_HEADS = 2
ATTN_GROUP = ATTN_Q_HEADS // ATTN_KV_HEADS
WINDOW = 128
BLOCK = 128
SPAN = BLOCK + 2 * WINDOW
N_BUCKETS = 32
MAX_DISTANCE = 128
DN_HEADS = 8
DN_DK = 64
DN_DV = 64
DN_CONV = 5
DN_CHUNK = 64
POOL_WINDOWS = (2, 4, 8, 16)
N_POOL_GROUPS = 4
POOL_GROUP = 128
ATTN_W = ATTN_Q_HEADS * HEAD_DIM
KV_W = ATTN_KV_HEADS * HEAD_DIM
DN_QK_W = DN_HEADS * DN_DK
DN_V_W = DN_HEADS * DN_DV
POOL_W = N_POOL_GROUPS * POOL_GROUP
N_BRANCH = 3
N_EXPERTS = 8
TOP_K = 2
DEEPNORM_ALPHA = (2.0 * DEPTH) ** 0.25
LN_EPS = 1e-5
RMS_EPS = 1e-6
NEG_INF = -1e30

VMEM_LIMIT_BYTES = 56 * 1024 * 1024
POOL_TILE = 256
POOL_SPAN = 512
CONV_HALO = 8
CONV_ROWS = 128


def _cparams(sem):
    return pltpu.CompilerParams(dimension_semantics=sem, vmem_limit_bytes=VMEM_LIMIT_BYTES)


def _full(shape):
    n = len(shape)
    return pl.BlockSpec(shape, lambda *_: (0,) * n)


def _sigmoid(x):
    return 1.0 / (1.0 + jnp.exp(-x))


def _silu(x):
    return x * _sigmoid(x)


def _layer_norm(x, g, b):
    mu = jnp.mean(x, -1, keepdims=True)
    xc = x - mu
    var = jnp.mean(xc * xc, -1, keepdims=True)
    return xc * lax.rsqrt(var + LN_EPS) * g + b


def _dot(a, b):
    return jnp.dot(a, b, preferred_element_type=f32)


def _dot_nt(a, b):
    return lax.dot_general(a, b, (((1,), (1,)), ((), ())), preferred_element_type=f32)


def _dot_tn(a, b):
    return lax.dot_general(a, b, (((0,), (0,)), ((), ())), preferred_element_type=f32)


def _split3(x):
    hi = x.astype(bf16)
    r = x - hi.astype(f32)
    mid = r.astype(bf16)
    lo = (r - mid.astype(f32)).astype(bf16)
    return hi, mid, lo


def _split2(x):
    hi = x.astype(bf16)
    return hi, (x - hi.astype(f32)).astype(bf16)


def _dot_split2_rhs(x, b_bf):
    hi, lo = _split2(x)
    return _dot(hi, b_bf) + _dot(lo, b_bf)


def _pack_bf16_pair(a, b):
    def rounded(x):
        bits = lax.bitcast_convert_type(x, jnp.uint32)
        return bits + jnp.uint32(0x7FFF) + ((bits >> 16) & jnp.uint32(1))
    return (rounded(b) & jnp.uint32(0xFFFF0000)) | (rounded(a) >> 16)


def _unpack_bf16_pair(p):
    lo = lax.bitcast_convert_type(p << 16, f32)
    hi = lax.bitcast_convert_type(p & jnp.uint32(0xFFFF0000), f32)
    return jnp.concatenate([lo, hi], axis=-1)


def _inproj_kernel(x_ref, *refs):
    n = len(refs) // 2
    x = x_ref[...]
    for w_ref, o_ref in zip(refs[:n], refs[n:]):
        o_ref[...] = _dot(x, w_ref[...]).astype(o_ref.dtype)


def _inproj(xb, ws, dtypes, tm=256):
    m = xb.shape[0]
    return pl.pallas_call(
        _inproj_kernel,
        grid=(m // tm,),
        in_specs=[pl.BlockSpec((tm, D_MODEL), lambda i: (i, 0))]
        + [pl.BlockSpec(w.shape, lambda i: (0, 0), pipeline_mode=pl.Buffered(1)) for w in ws],
        out_specs=[pl.BlockSpec((tm, w.shape[1]), lambda i: (i, 0)) for w in ws],
        out_shape=[jax.ShapeDtypeStruct((m, w.shape[1]), dt) for w, dt in zip(ws, dtypes)],
        compiler_params=_cparams(("parallel",)),
        name="inproj",
    )(xb, *ws)


def _t5_bucket(rel):
    nb = N_BUCKETS // 2
    max_exact = nb // 2
    n = np.abs(rel)
    large = max_exact + (np.log(np.maximum(n, 1) / max_exact) / np.log(MAX_DISTANCE / max_exact)
                         * (nb - max_exact)).astype(np.int32)
    large = np.minimum(large, nb - 1)
    return (np.where(rel > 0, nb, 0) + np.where(n < max_exact, n, large)).astype(np.int32)


ATTN_KV_COLS = 2 * KV_W + KV_W


def _bias_bucket_tables():
    j = np.arange(SPAN)[:, None]
    r = np.arange(BLOCK)[None, :]
    tabs = []
    for shift in (0, WINDOW, 2 * WINDOW):
        rel = j - shift - r
        tabs.append(np.where(np.abs(rel) <= WINDOW, _t5_bucket(rel), N_BUCKETS))
    return jnp.asarray(np.stack(tabs), dtype=jnp.int32)


def _bias_table_kernel(bucket_ref, rb_ref, o_ref):
    bucket = bucket_ref[0]
    for h in range(ATTN_Q_HEADS):
        acc = jnp.full(bucket.shape, NEG_INF, f32)
        for b in range(N_BUCKETS):
            acc = jnp.where(bucket == b, rb_ref[b, h], acc)
        o_ref[0, h] = acc


def _attn_bias_tables(rel_bias):
    return pl.pallas_call(
        _bias_table_kernel,
        grid=(3,),
        in_specs=[pl.BlockSpec((1, SPAN, BLOCK), lambda v: (v, 0, 0)),
                  pl.BlockSpec(memory_space=pltpu.SMEM)],
        out_specs=pl.BlockSpec((1, ATTN_Q_HEADS, SPAN, BLOCK), lambda v: (v, 0, 0, 0)),
        out_shape=jax.ShapeDtypeStruct((3, ATTN_Q_HEADS, SPAN, BLOCK), f32),
        compiler_params=_cparams(("parallel",)),
        name="attn_bias",
    )(_bias_bucket_tables(), rel_bias.astype(f32))


ATTN_STEP_BLOCKS = 2


def _attn_block(q, kv, bias_ref, variant, sink_ref):
    scale = HEAD_DIM ** -0.5
    pair_w = 2 * HEAD_DIM
    low = lax.broadcasted_iota(jnp.int32, (BLOCK, pair_w), 1) < HEAD_DIM
    v_t = kv[:, 2 * KV_W:].astype(f32).T.astype(bf16)
    o_t = []
    for g in range(ATTN_KV_HEADS):
        kk = kv[:, pair_w * g:pair_w * (g + 1)]
        qm = []
        for hl in range(ATTN_GROUP):
            h = ATTN_GROUP * g + hl
            qp = q[:, pair_w * (h // 2):pair_w * (h // 2 + 1)]
            qm.append(jnp.where(low if h % 2 == 0 else jnp.logical_not(low), qp, jnp.zeros_like(qp)))
        s_t = _dot_nt(kk, jnp.concatenate(qm, axis=0))
        p_n = []
        for hl in range(ATTN_GROUP):
            h = ATTN_GROUP * g + hl
            s = s_t[:, BLOCK * hl:BLOCK * (hl + 1)] * scale + bias_ref[variant, h]
            sk = sink_ref[0:1, h:h + 1]
            m = jnp.maximum(jnp.max(s, 0, keepdims=True), sk)
            p = jnp.exp(s - m)
            denom = jnp.sum(p, 0, keepdims=True) + jnp.exp(sk - m)
            p_n.append((p * (1.0 / denom)).astype(bf16))
        og = _dot(v_t[HEAD_DIM * g:HEAD_DIM * (g + 1), :], jnp.concatenate(p_n, axis=1))
        o_t += [og[:, BLOCK * hl:BLOCK * (hl + 1)] for hl in range(ATTN_GROUP)]
    return jnp.concatenate(o_t, axis=0).T


def _attn_kernel(q_ref, kv_ref, bias_ref, sink_ref, o_ref, *, seq):
    nb = seq // BLOCK
    for sb in range(ATTN_STEP_BLOCKS):
        i = pl.program_id(1) * ATTN_STEP_BLOCKS + sb
        ks = pl.multiple_of(jnp.clip(i * BLOCK - WINDOW, 0, seq - SPAN), BLOCK)
        variant = jnp.where(i == 0, 0, jnp.where(i == nb - 1, 2, 1))
        rows = slice(BLOCK * sb, BLOCK * (sb + 1))
        o = _attn_block(q_ref[0, rows, :], kv_ref[0, pl.ds(ks, SPAN), :], bias_ref, variant, sink_ref)
        o_ref[0, rows, :] = o.astype(o_ref.dtype)


def _attention(aq, akv, bias_tabs, sink):
    b, s, _ = aq.shape
    tq = BLOCK * ATTN_STEP_BLOCKS
    assert s % tq == 0 and s >= SPAN
    return pl.pallas_call(
        functools.partial(_attn_kernel, seq=s),
        grid=(b, s // tq),
        in_specs=[pl.BlockSpec((1, tq, ATTN_W), lambda bi, i: (bi, i, 0)),
                  pl.BlockSpec((1, s, ATTN_KV_COLS), lambda bi, i: (bi, 0, 0)),
                  _full(bias_tabs.shape),
                  _full((1, ATTN_Q_HEADS))],
        out_specs=pl.BlockSpec((1, tq, ATTN_W), lambda bi, i: (bi, i, 0)),
        out_shape=jax.ShapeDtypeStruct((b, s, ATTN_W), bf16),
        compiler_params=_cparams(("parallel", "arbitrary")),
        name="attn",
    )(aq, akv, bias_tabs, sink.reshape(1, ATTN_Q_HEADS).astype(f32))


def _dn_prep_kernel(u_ref, w_ref, ones_ref, o_ref, pad_ref, *, seq):
    sec = pl.program_id(1)
    zeros = jnp.zeros((CONV_HALO, DN_QK_W), f32)
    pad_ref[0:CONV_HALO, :] = zeros
    pad_ref[seq + CONV_HALO:seq + 2 * CONV_HALO, :] = zeros
    pad_ref[CONV_HALO:seq + CONV_HALO, :] = u_ref[0].astype(f32)
    n = CONV_ROWS + 2 * CONV_HALO

    def conv_silu(t0):
        win = pad_ref[t0:t0 + n, :]
        z = [win * w_ref[0, k:k + 1, :] for k in range(DN_CONV)]
        after = pltpu.roll(z[3] + pltpu.roll(z[4], n - 1, axis=0), n - 1, axis=0)
        before = pltpu.roll(z[1] + pltpu.roll(z[0], 1, axis=0), 1, axis=0)
        return _silu((z[2] + after + before)[CONV_HALO:CONV_HALO + CONV_ROWS])

    @pl.when(sec < 2)
    def _():
        scale = jnp.where(sec == 0, DN_DK ** -0.5, 1.0)
        for t0 in range(0, seq, CONV_ROWS):
            y = conv_silu(t0)
            ss = _dot_split2_rhs(y * y, ones_ref[...])
            o_ref[0, t0:t0 + CONV_ROWS, :] = y * lax.rsqrt(ss + 1e-6) * scale

    @pl.when(sec == 2)
    def _():
        for t0 in range(0, seq, CONV_ROWS):
            o_ref[0, t0:t0 + CONV_ROWS, :] = conv_silu(t0)


def _head_ones():
    blk = np.arange(DN_QK_W) // DN_DK
    return jnp.asarray(blk[:, None] == blk[None, :], dtype=bf16)


def _dn_prep(dqkv, conv_w):
    b, s, _ = dqkv.shape
    w3 = jnp.transpose(conv_w.reshape(DN_CONV, 3, DN_QK_W), (1, 0, 2)).astype(f32)
    return pl.pallas_call(
        functools.partial(_dn_prep_kernel, seq=s),
        grid=(b, 3),
        in_specs=[pl.BlockSpec((1, s, DN_QK_W), lambda bi, c: (bi, 0, c)),
                  pl.BlockSpec((1, DN_CONV, DN_QK_W), lambda bi, c: (c, 0, 0)),
                  _full((DN_QK_W, DN_QK_W))],
        out_specs=pl.BlockSpec((1, s, DN_QK_W), lambda bi, c: (bi, 0, c)),
        out_shape=jax.ShapeDtypeStruct(dqkv.shape, f32),
        scratch_shapes=[pltpu.VMEM((s + 2 * CONV_HALO, DN_QK_W), f32)],
        compiler_params=_cparams(("parallel", "arbitrary")),
        name="dn_prep",
    )(dqkv, w3, _head_ones())


def _softplus(x):
    return jnp.maximum(x, 0.0) + jnp.log1p(jnp.exp(-jnp.abs(x)))


QUAD = 4
QUAD_W = QUAD * DN_DK
N_QUADS = DN_HEADS // QUAD
DELTA_BLOCK_CHUNKS = 4


def _block_diag(xq):
    lane_head = lax.broadcasted_iota(jnp.int32, xq.shape, 1) // DN_DK
    return jnp.concatenate([jnp.where(lane_head == h, xq, 0.0).astype(bf16) for h in range(QUAD)], axis=0)


def _delta_chunk(q, k, v, gx, bx, d):
    C = DN_CHUNK
    W = DN_QK_W
    sign = 1 - 2 * d
    ii = lax.broadcasted_iota(jnp.int32, (C, W), 0)
    jl = lax.broadcasted_iota(jnp.int32, (C, W), 1) % DN_DK
    order = (ii - jl) * sign
    incl = order >= 0
    strict = order > 0
    rr = lax.broadcasted_iota(jnp.int32, (2 * C, 2 * C), 0)
    cc = lax.broadcasted_iota(jnp.int32, (2 * C, 2 * C), 1)
    tri = jnp.where(((rr % C) - cc) * sign >= 0, 1.0, 0.0)
    lmat = jnp.where(cc < C, tri, jnp.where(rr < C, -1.0, 0.0)).astype(bf16)
    rmat = jnp.concatenate([gx, jnp.where(order <= 0, gx, 0.0)], axis=0)
    dg = _dot(jnp.concatenate([lmat] * 2, axis=1), jnp.concatenate(_split2(rmat), axis=0))
    dmat, gc = dg[:C], dg[C:]
    decay = jnp.where(incl, jnp.exp(jnp.where(incl, dmat, 0.0)), 0.0)
    eg = jnp.exp(gc)
    last = C - 1 if d == 0 else 0
    gl = gc[last:last + 1, :]
    kdec = k * jnp.exp(gl - gc)
    cd = jnp.exp(gl)
    kb = k * bx
    eye = jnp.where(ii == jl, 1.0, 0.0)
    return dict(q=q, k=k, bx=bx, decay=decay, strict=strict, eye=eye, kdec=kdec, cd=cd,
                vb=v * bx, kbe=kb * eg, qdec=q * eg)


DELTA_CHAINS = 2 * DELTA_BLOCK_CHUNKS * N_QUADS


def _delta_kernel(qf_ref, kf_ref, vf_ref, qb_ref, kb_ref, vb_ref, abf_ref, abb_ref, alog_ref, dtb_ref,
                  of_ref, ob_ref, state_ref, u_s, w_s, qk_s, qdec_s, kdec_s, cd_s, *, nblk):
    C, H, NC = DN_CHUNK, DN_HEADS, DELTA_BLOCK_CHUNKS
    t = pl.program_id(0)
    slot_w = t % 2
    slot_r = 1 - slot_w

    @pl.when(t == 0)
    def _():
        for ref in (u_s, w_s, qk_s, qdec_s, kdec_s, cd_s):
            ref[1] = jnp.zeros(ref.shape[1:], ref.dtype)

    @pl.when((t + nblk - 1) % nblk == 0)
    def _():
        state_ref[...] = jnp.zeros(state_ref.shape, f32)

    def expand(x, parts):
        rows = lax.broadcasted_iota(jnp.int32, (H * parts, DN_QK_W), 0) % H
        cols = lax.broadcasted_iota(jnp.int32, (H * parts, DN_QK_W), 1) // DN_DK
        emat = jnp.where(rows == cols, 1.0, 0.0).astype(bf16)
        pieces = [p.astype(f32) for p in _split3(x)[:parts]]
        return _dot(jnp.concatenate(pieces, axis=-1).astype(bf16), emat)

    chains = []

    def stage_prepare():
        dirs = ((0, qf_ref, kf_ref, vf_ref, abf_ref), (1, qb_ref, kb_ref, vb_ref, abb_ref))
        for d, q_ref, k_ref, v_ref, ab_ref in dirs:
            ab = ab_ref[0, 0]
            g = -jnp.exp(alog_ref[d]) * _softplus(ab[:, :H] + dtb_ref[d])
            gx = expand(g, 3)
            bx = expand(_sigmoid(ab[:, H:]), 2)
            for c in range(NC):
                rows = slice(C * c, C * (c + 1))
                pre = _delta_chunk(q_ref[0, rows, :], k_ref[0, rows, :], v_ref[0, rows, :], gx[rows], bx[rows], d)
                for qd in range(N_QUADS):
                    sl = slice(QUAD_W * qd, QUAD_W * (qd + 1))
                    chains.append({n: x[:, sl] for n, x in pre.items()})

    def stage_gram():
        for ch in chains:
            kq = _dot_nt(jnp.concatenate([ch["k"], ch["q"]], axis=0).astype(bf16), _block_diag(ch["k"]))
            a_mat = jnp.where(ch["strict"], kq[:C] * ch["bx"] * ch["decay"], 0.0)
            ch["qk"] = kq[C:] * ch["decay"]
            ch["p"] = -a_mat
            ch["t"] = ch["eye"] - a_mat

    def stage_square():
        for ch in chains:
            ch["p"] = _dot(ch["p"].astype(bf16), _block_diag(ch["p"]))

    def stage_double():
        for ch in chains:
            pt = _dot(jnp.concatenate([ch["p"], ch["t"]], axis=0).astype(bf16), _block_diag(ch["p"]))
            ch["p"], ch["t"] = pt[:C], ch["t"] + pt[C:]

    def stage_last():
        for ch in chains:
            ch["tb"] = (ch["t"] + _dot(ch["t"].astype(bf16), _block_diag(ch["p"]))).astype(bf16)

    def stage_solve():
        for ch in chains:
            ch["u"] = _dot(ch["tb"], _block_diag(ch["vb"]))
            ch["w"] = _dot(ch["tb"], _block_diag(ch["kbe"]))

    row_head = lax.broadcasted_iota(jnp.int32, (QUAD_W, QUAD_W), 0) // DN_DK
    col_head = lax.broadcasted_iota(jnp.int32, (QUAD_W, QUAD_W), 1) // DN_DK
    o_refs = (of_ref, ob_ref)
    live = {}

    def scan_chains(step):
        for d in range(2):
            c = step if d == 0 else NC - 1 - step
            for qd in range(N_QUADS):
                yield d, c, qd, (d * NC + c) * N_QUADS + qd

    def scan_first(step):
        for d, c, qd, n in scan_chains(step):
            st = state_ref[d, qd]
            wq = _dot(jnp.concatenate([w_s[slot_r, n], qdec_s[slot_r, n]], axis=0), st.astype(bf16))
            live[n] = (st, u_s[slot_r, n] - wq[:C], wq[C:])

    def scan_second(step):
        for d, c, qd, n in scan_chains(step):
            st, v_new, qs = live.pop(n)
            o = qs + _dot(qk_s[slot_r, n], _block_diag(v_new))
            o_refs[d][0, C * c:C * (c + 1), QUAD_W * qd:QUAD_W * (qd + 1)] = o
            s_new = _dot_tn(kdec_s[slot_r, n], v_new.astype(bf16))
            state_ref[d, qd] = st * cd_s[slot_r, n, 0:1, :] + jnp.where(row_head == col_head, s_new, 0.0)

    local = [stage_prepare, stage_gram, stage_square] + [stage_double] * 4 + [stage_last, stage_solve]
    scan = [f for step in range(NC) for f in (functools.partial(scan_first, step),
                                              functools.partial(scan_second, step))]
    for k in range(max(len(local), len(scan))):
        if k < len(local):
            local[k]()
        if k < len(scan):
            scan[k]()

    for n, ch in enumerate(chains):
        u_s[slot_w, n] = ch["u"]
        w_s[slot_w, n] = ch["w"].astype(bf16)
        qk_s[slot_w, n] = ch["qk"].astype(bf16)
        qdec_s[slot_w, n] = ch["qdec"].astype(bf16)
        kdec_s[slot_w, n] = ch["kdec"].astype(bf16)
        cd_s[slot_w, n] = jnp.broadcast_to(ch["cd"], (8, QUAD_W))


def _delta(qkvn, dab, a_log, dt_bias):
    b, s, _ = qkvn.shape
    H = DN_HEADS
    tb = DN_CHUNK * DELTA_BLOCK_CHUNKS
    nblk = s // tb
    total = b * nblk
    assert s % tb == 0
    d4 = dab.reshape(b, s, 4, H)
    ab = jnp.stack([jnp.concatenate([d4[:, :, r], d4[:, :, 2 + r]], -1) for r in range(2)])
    alog = a_log.astype(f32).reshape(2, 1, H)
    dtb = dt_bias.astype(f32).reshape(2, 1, H)

    def block(lin, rev):
        i = lin % nblk
        return lin // nblk, (nblk - 1 - i) if rev else i

    def sec(n, rev, lag):
        def index(t):
            bi, i = block(jnp.maximum(t - 1, 0) if lag else jnp.minimum(t, total - 1), rev)
            return bi, i, n
        return pl.BlockSpec((1, tb, DN_QK_W), index)

    def ab_spec(d):
        def index(t):
            bi, i = block(jnp.minimum(t, total - 1), d == 1)
            return d, bi, i, 0
        return pl.BlockSpec((1, 1, tb, 2 * H), index)

    chain_buf = lambda dt: pltpu.VMEM((2, DELTA_CHAINS, DN_CHUNK, QUAD_W), dt)
    return pl.pallas_call(
        functools.partial(_delta_kernel, nblk=nblk),
        grid=(total + 1,),
        in_specs=[sec(0, False, False), sec(1, False, False), sec(2, False, False),
                  sec(0, True, False), sec(1, True, False), sec(2, True, False),
                  ab_spec(0), ab_spec(1), _full((2, 1, H)), _full((2, 1, H))],
        out_specs=[sec(0, False, True), sec(0, True, True)],
        out_shape=[jax.ShapeDtypeStruct((b, s, DN_V_W), f32)] * 2,
        scratch_shapes=[pltpu.VMEM((2, N_QUADS, QUAD_W, QUAD_W), f32),
                        chain_buf(f32), chain_buf(bf16), chain_buf(bf16), chain_buf(bf16), chain_buf(bf16),
                        pltpu.VMEM((2, DELTA_CHAINS, 8, QUAD_W), f32)],
        compiler_params=_cparams(("arbitrary",)),
        name="delta",
    )(qkvn, qkvn, qkvn, qkvn, qkvn, qkvn, ab, ab, alog, dtb)


def _pool_band_tables():
    r = np.arange(POOL_TILE)[:, None]
    j = np.arange(POOL_SPAN)[None, :]
    tabs = np.zeros((3, N_POOL_GROUPS, POOL_TILE, POOL_SPAN), np.float32)
    for vi, shift in enumerate((0, 128, 256)):
        rel = j - shift - r
        for gi, w in enumerate(POOL_WINDOWS):
            tabs[vi, gi] = np.abs(rel) <= w // 2
    return jnp.asarray(tabs, dtype=bf16)


def _pool_kernel(u_ref, band_ref, w_ref, scale_ref, o_ref, *, seq):
    t = pl.program_id(1)
    t0 = pl.multiple_of(t * POOL_TILE, POOL_TILE)
    ks = pl.multiple_of(jnp.clip(t0 - 128, 0, seq - POOL_SPAN), 128)
    win = u_ref[0, pl.ds(ks, POOL_SPAN), :]
    own = u_ref[0, pl.ds(t0, POOL_TILE), :].astype(f32)
    pos = t0 + lax.broadcasted_iota(jnp.int32, (POOL_TILE, POOL_GROUP), 0)
    outs = []
    for gi, w in enumerate(POOL_WINDOWS):
        r = w // 2
        sl = slice(POOL_GROUP * gi, POOL_GROUP * (gi + 1))
        wsum = _dot(band_ref[0, gi], win[:, sl])
        count = (jnp.minimum(pos + r, seq - 1) - jnp.maximum(pos - r, 0) + 1).astype(f32)
        mixed = wsum / count - own[:, sl]
        outs.append(_dot(mixed.astype(bf16), w_ref[gi]))
    o_ref[0] = (jnp.concatenate(outs, -1) * scale_ref[...]).astype(o_ref.dtype)


def _pool(pu, pool_w, pool_scale):
    b, s, _ = pu.shape
    nt = s // POOL_TILE
    assert s % POOL_TILE == 0 and s >= POOL_SPAN

    def variant(bi, t):
        return (jnp.where(t == 0, 0, jnp.where(t == nt - 1, 2, 1)), 0, 0, 0)

    return pl.pallas_call(
        functools.partial(_pool_kernel, seq=s),
        grid=(b, nt),
        in_specs=[pl.BlockSpec((1, s, POOL_W), lambda bi, t: (bi, 0, 0)),
                  pl.BlockSpec((1, N_POOL_GROUPS, POOL_TILE, POOL_SPAN), variant),
                  _full((N_POOL_GROUPS, POOL_GROUP, POOL_GROUP)),
                  _full((1, POOL_W))],
        out_specs=pl.BlockSpec((1, POOL_TILE, POOL_W), lambda bi, t: (bi, t, 0)),
        out_shape=jax.ShapeDtypeStruct((b, s, POOL_W), bf16),
        compiler_params=_cparams(("parallel", "arbitrary")),
        name="pool",
    )(pu, _pool_band_tables(), pool_w.astype(bf16), pool_scale.reshape(1, POOL_W).astype(f32))


def _merge_kernel(oa_ref, odf_ref, odb_ref, z_ref, op_ref, gate_ref, h_ref, ones_ref, nw_ref, wb_ref, wo_ref,
                  g_ref, b_ref, hf_ref, hb_ref, *, packed):
    od = odf_ref[...] + odb_ref[...]
    ms = _dot_split2_rhs(od * od, ones_ref[...]) * (1.0 / DN_DV)
    od = od * lax.rsqrt(ms + RMS_EPS) * nw_ref[...] * _silu(z_ref[...].astype(f32))
    branches = (oa_ref[...], od.astype(bf16), op_ref[...])
    merged = None
    for n in range(N_BRANCH):
        up = _dot(branches[n], wb_ref[n])
        gate = _sigmoid(gate_ref[:, D_MODEL * n:D_MODEL * (n + 1)].astype(f32))
        merged = gate * up if merged is None else merged + gate * up
    y = _dot(merged.astype(bf16), wo_ref[...])
    hn = _layer_norm(DEEPNORM_ALPHA * h_ref[...] + y, g_ref[...], b_ref[...])
    hf_ref[...] = hn
    if packed:
        hb_ref[...] = _pack_bf16_pair(hn[:, :D_MODEL // 2], hn[:, D_MODEL // 2:])
    else:
        hb_ref[...] = hn.astype(bf16)


def _merge(oa, odf, odb, z, op, gates, h, dn_norm_w, w_branch, w_out, ln_g, ln_b, packed, tm=256):
    m = h.shape[0]
    row = lambda w: pl.BlockSpec((tm, w), lambda i: (i, 0))
    nw = jnp.tile(dn_norm_w.astype(f32), DN_HEADS).reshape(1, DN_V_W)
    low = (jax.ShapeDtypeStruct((m, D_MODEL // 2), jnp.uint32) if packed
           else jax.ShapeDtypeStruct((m, D_MODEL), bf16))
    return pl.pallas_call(
        functools.partial(_merge_kernel, packed=packed),
        grid=(m // tm,),
        in_specs=[row(ATTN_W), row(DN_V_W), row(DN_V_W), row(DN_V_W), row(POOL_W),
                  row(N_BRANCH * D_MODEL), row(D_MODEL),
                  _full((DN_V_W, DN_V_W)), _full((1, DN_V_W)),
                  _full((N_BRANCH, ATTN_W, D_MODEL)), _full((D_MODEL, D_MODEL)),
                  _full((1, D_MODEL)), _full((1, D_MODEL))],
        out_specs=[row(D_MODEL), row(low.shape[1])],
        out_shape=[jax.ShapeDtypeStruct((m, D_MODEL), f32), low],
        compiler_params=_cparams(("parallel",)),
        name="merge",
    )(oa, odf, odb, z, op, gates, h, _head_ones(), nw, w_branch.astype(bf16), w_out.astype(bf16),
      ln_g.reshape(1, D_MODEL).astype(f32), ln_b.reshape(1, D_MODEL).astype(f32))


def _swiglu_step(x, wg_ref, wu_ref, wd_ref, acc_ref):
    g = _dot(x, wg_ref[0].astype(bf16))
    u = _dot(x, wu_ref[0].astype(bf16))
    acc_ref[...] += _dot((_silu(g) * u).astype(bf16), wd_ref[0].astype(bf16))


def _ffn_kernel(x_ref, h_ref, wg_ref, wu_ref, wd_ref, g_ref, b_ref, hf_ref, hb_ref, acc_ref):
    j = pl.program_id(1)

    @pl.when(j == 0)
    def _():
        acc_ref[...] = jnp.zeros(acc_ref.shape, f32)

    _swiglu_step(x_ref[...], wg_ref, wu_ref, wd_ref, acc_ref)

    @pl.when(j == pl.num_programs(1) - 1)
    def _():
        hn = _layer_norm(DEEPNORM_ALPHA * h_ref[...] + acc_ref[...], g_ref[...], b_ref[...])
        hf_ref[...] = hn
        hb_ref[...] = hn.astype(bf16)


def _ffn(hb, hf, w1, w3, w2, ln_g, ln_b, tm=1024, tf=256):
    m = hb.shape[0]
    tm = min(tm, m)
    ff = w1.shape[-1]
    row = pl.BlockSpec((tm, D_MODEL), lambda i, j: (i, 0))
    return pl.pallas_call(
        _ffn_kernel,
        grid=(m // tm, ff // tf),
        in_specs=[row, row,
                  pl.BlockSpec((1, D_MODEL, tf), lambda i, j: (0, 0, j)),
                  pl.BlockSpec((1, D_MODEL, tf), lambda i, j: (0, 0, j)),
                  pl.BlockSpec((1, tf, D_MODEL), lambda i, j: (0, j, 0)),
                  pl.BlockSpec((1, D_MODEL), lambda i, j: (0, 0)),
                  pl.BlockSpec((1, D_MODEL), lambda i, j: (0, 0))],
        out_specs=[row, row],
        out_shape=[jax.ShapeDtypeStruct((m, D_MODEL), f32), jax.ShapeDtypeStruct((m, D_MODEL), bf16)],
        scratch_shapes=[pltpu.VMEM((tm, D_MODEL), f32)],
        compiler_params=_cparams(("parallel", "arbitrary")),
        name="ffn",
    )(hb, hf, w1[None], w3[None], w2[None],
      ln_g.reshape(1, D_MODEL).astype(f32), ln_b.reshape(1, D_MODEL).astype(f32))


MOE_TILE = 1024
PACKED_W = D_MODEL // 2
SC_CORES = 2
SC_SUBCORES = 16
SC_WORKERS = SC_CORES * SC_SUBCORES
SC_ROWS = 64


def _gmm_kernel(te_ref, tv_ref, x_ref, wg_ref, wu_ref, wd_ref, o_ref, acc_ref):
    i = pl.program_id(0)
    j = pl.program_id(1)
    valid = tv_ref[i]

    @pl.when(j == 0)
    def _():
        acc_ref[...] = jnp.zeros(acc_ref.shape, f32)

    @pl.when(valid > 0)
    def _():
        rows = lax.broadcasted_iota(jnp.int32, acc_ref.shape, 0)
        x = jnp.where(rows < valid, _unpack_bf16_pair(x_ref[...]), 0.0).astype(bf16)
        _swiglu_step(x, wg_ref, wu_ref, wd_ref, acc_ref)

    @pl.when(j == pl.num_programs(1) - 1)
    def _():
        o_ref[...] = _pack_bf16_pair(acc_ref[:, :PACKED_W], acc_ref[:, PACKED_W:])


def _gmm(xs, tile_expert, tile_valid, wg, wu, wd, tm, tf=512):
    n = xs.shape[0]
    ff = wg.shape[-1]
    nj = ff // tf

    def fcol(i, j, te, tv):
        return jnp.where(tv[i] > 0, j, nj - 1)

    grid_spec = pltpu.PrefetchScalarGridSpec(
        num_scalar_prefetch=2,
        grid=(n // tm, nj),
        in_specs=[pl.BlockSpec((tm, PACKED_W), lambda i, j, te, tv: (i, 0)),
                  pl.BlockSpec((1, D_MODEL, tf), lambda i, j, te, tv: (te[i], 0, fcol(i, j, te, tv))),
                  pl.BlockSpec((1, D_MODEL, tf), lambda i, j, te, tv: (te[i], 0, fcol(i, j, te, tv))),
                  pl.BlockSpec((1, tf, D_MODEL), lambda i, j, te, tv: (te[i], fcol(i, j, te, tv), 0))],
        out_specs=pl.BlockSpec((tm, PACKED_W), lambda i, j, te, tv: (i, 0)),
        scratch_shapes=[pltpu.VMEM((tm, D_MODEL), f32)])
    return pl.pallas_call(
        _gmm_kernel,
        grid_spec=grid_spec,
        out_shape=jax.ShapeDtypeStruct((n, PACKED_W), jnp.uint32),
        compiler_params=_cparams(("parallel", "arbitrary")),
        name="moe_gmm",
    )(tile_expert, tile_valid, xs, wg, wu, wd)


def _router_kernel(x_ref, w_ref, mi_ref, mf_ref, cnt_ref, carry_ref):
    @pl.when(pl.program_id(0) == 0)
    def _():
        carry_ref[...] = jnp.zeros(carry_ref.shape, f32)

    xh, xl = _split2(x_ref[...])
    wh, wl = _split2(w_ref[...])
    logits = _dot(xh, wh) + (_dot(xl, wh) + _dot(xh, wl))
    tm = logits.shape[0]
    lane = lax.broadcasted_iota(jnp.int32, logits.shape, 1)
    m1 = jnp.max(logits, -1, keepdims=True)
    i1 = jnp.min(jnp.where(logits == m1, lane, N_EXPERTS), -1, keepdims=True)
    rest = jnp.where(lane == i1, -jnp.inf, logits)
    m2 = jnp.max(rest, -1, keepdims=True)
    i2 = jnp.min(jnp.where(rest == m2, lane, N_EXPERTS), -1, keepdims=True)
    e2 = jnp.exp(m2 - m1)
    p1 = 1.0 / (1.0 + e2)
    p2 = e2 / (1.0 + e2)
    oh1 = jnp.where(lane == i1, 1.0, 0.0)
    oh2 = jnp.where(lane == i2, 1.0, 0.0)
    both = oh1 + oh2
    rr = lax.broadcasted_iota(jnp.int32, (tm, tm), 0)
    cc = lax.broadcasted_iota(jnp.int32, (tm, tm), 1)
    earlier = jnp.where(cc < rr, 1.0, 0.0).astype(bf16)
    before = _dot(earlier, both.astype(bf16)) + carry_ref[...]
    r1 = jnp.sum(oh1 * before, -1, keepdims=True).astype(jnp.int32)
    r2 = jnp.sum(oh2 * before, -1, keepdims=True).astype(jnp.int32)
    carry_ref[...] += jnp.sum(both, 0, keepdims=True)
    cnt_ref[...] = carry_ref[...].astype(jnp.int32)
    mi_ref[...] = jnp.where(lane == 0, i1, jnp.where(lane == 1, i2, jnp.where(lane == 2, r1,
                                                                                  jnp.where(lane == 3, r2, 0))))
    mf_ref[...] = jnp.where(lane == 0, p1, jnp.where(lane == 1, p2, 0.0))


def _router(hf, router_w, tm=512):
    m = hf.shape[0]
    tm = min(tm, m)
    return pl.pallas_call(
        _router_kernel,
        grid=(m // tm,),
        in_specs=[pl.BlockSpec((tm, D_MODEL), lambda i: (i, 0)), _full((D_MODEL, N_EXPERTS))],
        out_specs=[pl.BlockSpec((tm, N_EXPERTS), lambda i: (i, 0)),
                   pl.BlockSpec((tm, N_EXPERTS), lambda i: (i, 0)),
                   _full((1, N_EXPERTS))],
        out_shape=[jax.ShapeDtypeStruct((m, N_EXPERTS), jnp.int32),
                   jax.ShapeDtypeStruct((m, N_EXPERTS), f32),
                   jax.ShapeDtypeStruct((1, N_EXPERTS), jnp.int32)],
        scratch_shapes=[pltpu.VMEM((1, N_EXPERTS), f32)],
        compiler_params=_cparams(("arbitrary",)),
        name="router",
    )(hf, router_w.astype(f32))


def _sc_mesh():
    return plsc.VectorSubcoreMesh(core_axis_name="c", subcore_axis_name="s",
                                  num_cores=SC_CORES, num_subcores=SC_SUBCORES)


def _sc_worker_base(per_worker):
    return (lax.axis_index("s") * SC_CORES + lax.axis_index("c")) * per_worker


def _sc_scratch(d, dtype):
    return [pltpu.VMEM((SC_ROWS,), jnp.int32), pltpu.VMEM((SC_ROWS,), jnp.int32), pltpu.VMEM((2, SC_ROWS, d), dtype),
            pltpu.SemaphoreType.DMA((2,)), pltpu.SemaphoreType.DMA((2,))]


def _sc_scatter_rows(src, idx0, idx1, n_out):
    m, d = src.shape
    per_worker = m // SC_WORKERS
    n_chunks = per_worker // SC_ROWS
    assert m % (SC_WORKERS * SC_ROWS * 2) == 0

    def body(src_hbm, idx0_hbm, idx1_hbm, out_hbm, idx_a, idx_b, rows_v, sem_in, sem_out):
        base = _sc_worker_base(per_worker)
        idx_v = (idx_a, idx_b)

        def load(j, slot):
            off = pl.multiple_of(base + j * SC_ROWS, SC_ROWS)
            return pltpu.make_async_copy(src_hbm.at[pl.ds(off, SC_ROWS)], rows_v.at[slot], sem_in.at[slot])

        def store(j, slot, idx_hbm):
            off = pl.multiple_of(base + j * SC_ROWS, SC_ROWS)
            pltpu.sync_copy(idx_hbm.at[pl.ds(off, SC_ROWS)], idx_v[slot])
            pltpu.async_copy(rows_v.at[slot], out_hbm.at[idx_v[slot]], sem_out.at[slot]).wait()

        load(0, 0).start()

        @pl.loop(0, n_chunks, step=2)
        def _(j):
            for slot in range(2):
                @pl.when(j + slot + 1 < n_chunks)
                def _():
                    load(j + slot + 1, 1 - slot).start()
                load(j + slot, slot).wait()
                store(j + slot, slot, idx0_hbm)
                store(j + slot, slot, idx1_hbm)

    return pl.kernel(
        body, out_type=jax.ShapeDtypeStruct((n_out, d), src.dtype), mesh=_sc_mesh(),
        scratch_types=_sc_scratch(d, src.dtype), name="moe_dispatch",
    )(src, idx0, idx1)


def _sc_gather_rows(table, idx):
    n = idx.shape[0]
    d = table.shape[1]
    per_worker = n // SC_WORKERS
    n_chunks = per_worker // SC_ROWS
    assert n % (SC_WORKERS * SC_ROWS * 2) == 0

    def body(table_hbm, idx_hbm, out_hbm, idx_a, idx_b, rows_v, sem_in, sem_out):
        base = _sc_worker_base(per_worker)
        idx_v = (idx_a, idx_b)

        def gather(j, slot):
            off = pl.multiple_of(base + j * SC_ROWS, SC_ROWS)
            pltpu.sync_copy(idx_hbm.at[pl.ds(off, SC_ROWS)], idx_v[slot])
            return pltpu.make_async_copy(table_hbm.at[idx_v[slot]], rows_v.at[slot], sem_in.at[slot])

        def wait_gather(slot):
            pltpu.make_async_copy(table_hbm.at[idx_v[slot]], rows_v.at[slot], sem_in.at[slot]).wait()

        def write(j, slot):
            off = pl.multiple_of(base + j * SC_ROWS, SC_ROWS)
            return pltpu.make_async_copy(rows_v.at[slot], out_hbm.at[pl.ds(off, SC_ROWS)], sem_out.at[slot])

        gather(0, 0).start()

        @pl.loop(0, n_chunks, step=2)
        def _(j):
            for slot in range(2):
                @pl.when(j + slot + 1 < n_chunks)
                def _():
                    @pl.when(j + slot >= 1)
                    def _():
                        write(j + slot - 1, 1 - slot).wait()
                    gather(j + slot + 1, 1 - slot).start()
                wait_gather(slot)
                write(j + slot, slot).start()

        write(n_chunks - 2, 0).wait()
        write(n_chunks - 1, 1).wait()

    return pl.kernel(
        body, out_type=jax.ShapeDtypeStruct((n, d), table.dtype), mesh=_sc_mesh(),
        scratch_types=_sc_scratch(d, table.dtype), name="moe_collect",
    )(table, idx)


def _combine_kernel(y0_ref, y1_ref, mf_ref, h_ref, g_ref, b_ref, o_ref):
    p = mf_ref[...]
    y = p[:, 0:1] * _unpack_bf16_pair(y0_ref[0]) + p[:, 1:2] * _unpack_bf16_pair(y1_ref[0])
    o_ref[...] = _layer_norm(DEEPNORM_ALPHA * h_ref[...] + y, g_ref[...], b_ref[...])


def _combine(yg, mf, hf, ln_g, ln_b, tm=512):
    m = hf.shape[0]
    tm = min(tm, m)
    return pl.pallas_call(
        _combine_kernel,
        grid=(m // tm,),
        in_specs=[pl.BlockSpec((1, tm, PACKED_W), lambda i: (0, i, 0)),
                  pl.BlockSpec((1, tm, PACKED_W), lambda i: (1, i, 0)),
                  pl.BlockSpec((tm, N_EXPERTS), lambda i: (i, 0)),
                  pl.BlockSpec((tm, D_MODEL), lambda i: (i, 0)),
                  _full((1, D_MODEL)), _full((1, D_MODEL))],
        out_specs=pl.BlockSpec((tm, D_MODEL), lambda i: (i, 0)),
        out_shape=jax.ShapeDtypeStruct((m, D_MODEL), f32),
        compiler_params=_cparams(("parallel",)),
        name="moe_combine",
    )(yg, yg, mf, hf, ln_g.reshape(1, D_MODEL).astype(f32), ln_b.reshape(1, D_MODEL).astype(f32))


def _moe(hf, hp, router_w, wg, wu, wd, ln_g, ln_b):
    m = hf.shape[0]
    tm = MOE_TILE
    mi, mf, cnt = _router(hf, router_w)
    counts = cnt[0]
    padded = (counts + tm - 1) // tm * tm
    ends = jnp.cumsum(padded)
    starts = ends - padded
    experts = jnp.arange(N_EXPERTS, dtype=jnp.int32)
    pos = jnp.sum(jnp.where(mi[:, 0:TOP_K, None] == experts, starts, 0), -1) + mi[:, TOP_K:2 * TOP_K]
    n_tiles = TOP_K * m // tm + N_EXPERTS
    tile_start = jnp.arange(n_tiles, dtype=jnp.int32) * tm
    tile_expert = jnp.minimum(jnp.sum(tile_start[:, None] >= ends[None, :], -1), N_EXPERTS - 1).astype(jnp.int32)
    tile_valid = jnp.clip(starts[tile_expert] + counts[tile_expert] - tile_start, 0, tm).astype(jnp.int32)
    tile_valid = jnp.where(tile_start < ends[-1], tile_valid, 0)
    xs = _sc_scatter_rows(hp, pos[:, 0], pos[:, 1], n_tiles * tm)
    ys = _gmm(xs, tile_expert, tile_valid, wg, wu, wd, tm)
    yg = _sc_gather_rows(ys, jnp.concatenate([pos[:, 0], pos[:, 1]]))
    return _combine(yg.reshape(TOP_K, m, PACKED_W), mf, hf, ln_g, ln_b)


_SPLITS = (ATTN_W, KV_W, KV_W, 3 * DN_QK_W, DN_V_W, 4 * DN_HEADS, POOL_W, N_BRANCH * D_MODEL)
_SPLIT_DTYPES = (bf16, bf16, f32, bf16, f32, bf16, bf16)


def _mixer(hf, hb, bsz, seq, bias_tabs, w_in, sink, conv_w, a_log, dt_bias, dn_norm_w, pool_w, pool_scale,
           w_branch, w_out, ln_g, ln_b, packed):
    points = np.cumsum(_SPLITS)[:-1].tolist()
    wq, wk, wv, *rest = jnp.split(w_in.astype(bf16), points, axis=-1)
    wkv = jnp.concatenate([wk[:, HEAD_DIM * g:HEAD_DIM * (g + 1)] for g in range(ATTN_KV_HEADS) for _ in range(2)]
                          + [wv], axis=-1)
    aq, akv, dqkv, dz, dab, pu, gates = _inproj(hb, [wq, wkv] + rest, _SPLIT_DTYPES)
    shp = lambda t: t.reshape(bsz, seq, t.shape[-1])
    oa = _attention(shp(aq), shp(akv), bias_tabs, sink)
    qkvn = _dn_prep(shp(dqkv), conv_w)
    odf, odb = _delta(qkvn, shp(dab), a_log, dt_bias)
    op = _pool(shp(pu), pool_w, pool_scale)
    m = bsz * seq
    return _merge(oa.reshape(m, ATTN_W), odf.reshape(m, DN_V_W), odb.reshape(m, DN_V_W), dz, op.reshape(m, POOL_W),
                  gates, hf, dn_norm_w, w_branch, w_out, ln_g, ln_b, packed)


def kernel(x, w_in, attn_sink, rel_bias, conv_w, a_log, dt_bias, dn_norm_w, pool_w, pool_scale, w_branch, w_out,
           ln1_g, ln1_b, ln2_g, ln2_b, ffn_w1, ffn_w3, ffn_w2, router_w, moe_wg, moe_wu, moe_wd):
    bsz, seq, _ = x.shape
    m = bsz * seq
    bias_tabs = _attn_bias_tables(rel_bias)
    hf = x.reshape(m, D_MODEL).astype(f32)
    hb = hf.astype(bf16)
    for layer in range(DEPTH):
        dense = layer % 2 == 0
        hf, hb = _mixer(hf, hb, bsz, seq, bias_tabs, w_in[layer], attn_sink[layer], conv_w[layer], a_log[layer],
                        dt_bias[layer], dn_norm_w[layer], pool_w[layer], pool_scale[layer], w_branch[layer],
                        w_out[layer], ln1_g[layer], ln1_b[layer], packed=not dense)
        i = layer // 2
        if dense:
            hf, hb = _ffn(hb, hf, ffn_w1[i], ffn_w3[i], ffn_w2[i], ln2_g[layer], ln2_b[layer])
        else:
            hf = _moe(hf, hb, router_w[i], moe_wg[i], moe_wu[i], moe_wd[i], ln2_g[layer], ln2_b[layer])
            hb = hf.astype(bf16) if layer + 1 < DEPTH else None
    return hf.reshape(bsz, seq, D_MODEL).astype(x.dtype)
```

```python
import functools

import numpy as np
import jax
import jax.numpy as jnp
from jax import lax
from jax.experimental import pallas as pl
from jax.experimental.pallas import tpu as pltpu
from jax.experimental.pallas import tpu_sc as plsc

f32 = jnp.float32
bf16 = jnp.bfloat16

D_MODEL = 1024
DEPTH = 2
HEAD_DIM = 64
ATTN_Q_HEADS = 8
ATTN_KV_HEADS = 2
ATTN_GROUP = ATTN_Q_HEADS // ATTN_KV_HEADS
WINDOW = 128
BLOCK = 128
SPAN = BLOCK + 2 * WINDOW
N_BUCKETS = 32
MAX_DISTANCE = 128
DN_HEADS = 8
DN_DK = 64
DN_DV = 64
DN_CONV = 5
DN_CHUNK = 64
POOL_WINDOWS = (2, 4, 8, 16)
N_POOL_GROUPS = 4
POOL_GROUP = 128
ATTN_W = ATTN_Q_HEADS * HEAD_DIM
KV_W = ATTN_KV_HEADS * HEAD_DIM
DN_QK_W = DN_HEADS * DN_DK
DN_V_W = DN_HEADS * DN_DV
POOL_W = N_POOL_GROUPS * POOL_GROUP
N_BRANCH = 3
N_EXPERTS = 8
TOP_K = 2
DEEPNORM_ALPHA = (2.0 * DEPTH) ** 0.25
LN_EPS = 1e-5
RMS_EPS = 1e-6
NEG_INF = -1e30

VMEM_LIMIT_BYTES = 56 * 1024 * 1024
POOL_TILE = 256
POOL_SPAN = 512
CONV_HALO = 8
CONV_ROWS = 128


def _cparams(sem):
    return pltpu.CompilerParams(dimension_semantics=sem, vmem_limit_bytes=VMEM_LIMIT_BYTES)


def _full(shape):
    n = len(shape)
    return pl.BlockSpec(shape, lambda *_: (0,) * n)


def _sigmoid(x):
    return 0.5 * jnp.tanh(0.5 * x) + 0.5


def _silu(x):
    return x * _sigmoid(x)


def _layer_norm(x, g, b):
    mu = jnp.mean(x, -1, keepdims=True)
    xc = x - mu
    var = jnp.mean(xc * xc, -1, keepdims=True)
    return xc * lax.rsqrt(var + LN_EPS) * g + b


def _dot(a, b):
    return jnp.dot(a, b, preferred_element_type=f32)


def _dot_nt(a, b):
    return lax.dot_general(a, b, (((1,), (1,)), ((), ())), preferred_element_type=f32)


def _dot_tn(a, b):
    return lax.dot_general(a, b, (((0,), (0,)), ((), ())), preferred_element_type=f32)


def _split3(x):
    hi = x.astype(bf16)
    r = x - hi.astype(f32)
    mid = r.astype(bf16)
    lo = (r - mid.astype(f32)).astype(bf16)
    return hi, mid, lo


def _split2(x):
    hi = x.astype(bf16)
    return hi, (x - hi.astype(f32)).astype(bf16)


def _dot_split2_rhs(x, b_bf):
    hi, lo = _split2(x)
    return _dot(hi, b_bf) + _dot(lo, b_bf)


def _pack_bf16_pair(a, b):
    def rounded(x):
        bits = lax.bitcast_convert_type(x, jnp.uint32)
        return bits + jnp.uint32(0x7FFF) + ((bits >> 16) & jnp.uint32(1))
    return (rounded(b) & jnp.uint32(0xFFFF0000)) | (rounded(a) >> 16)


def _unpack_bf16_pair(p):
    lo = lax.bitcast_convert_type(p << 16, f32)
    hi = lax.bitcast_convert_type(p & jnp.uint32(0xFFFF0000), f32)
    return jnp.concatenate([lo, hi], axis=-1)


def _inproj_kernel(x_ref, *refs):
    n = len(refs) // 2
    x = x_ref[...].astype(bf16)
    for w_ref, o_ref in zip(refs[:n], refs[n:]):
        o_ref[...] = _dot(x, w_ref[...]).astype(o_ref.dtype)


def _inproj(xb, ws, dtypes, tm=256):
    m = xb.shape[0]
    return pl.pallas_call(
        _inproj_kernel,
        grid=(m // tm,),
        in_specs=[pl.BlockSpec((tm, D_MODEL), lambda i: (i, 0))]
        + [pl.BlockSpec(w.shape, lambda i: (0, 0), pipeline_mode=pl.Buffered(1)) for w in ws],
        out_specs=[pl.BlockSpec((tm, w.shape[1]), lambda i: (i, 0)) for w in ws],
        out_shape=[jax.ShapeDtypeStruct((m, w.shape[1]), dt) for w, dt in zip(ws, dtypes)],
        compiler_params=_cparams(("parallel",)),
        name="inproj",
    )(xb, *ws)


def _t5_bucket(rel):
    nb = N_BUCKETS // 2
    max_exact = nb // 2
    n = np.abs(rel)
    large = max_exact + (np.log(np.maximum(n, 1) / max_exact) / np.log(MAX_DISTANCE / max_exact)
                         * (nb - max_exact)).astype(np.int32)
    large = np.minimum(large, nb - 1)
    return (np.where(rel > 0, nb, 0) + np.where(n < max_exact, n, large)).astype(np.int32)


ATTN_KV_COLS = 2 * KV_W + KV_W


def _bias_bucket_tables():
    j = np.arange(SPAN)[:, None]
    r = np.arange(BLOCK)[None, :]
    tabs = []
    for shift in (0, WINDOW, 2 * WINDOW):
        rel = j - shift - r
        tabs.append(np.where(np.abs(rel) <= WINDOW, _t5_bucket(rel), N_BUCKETS))
    return jnp.asarray(np.stack(tabs), dtype=jnp.int32)


def _bias_table_kernel(bucket_ref, rb_ref, o_ref):
    bucket = bucket_ref[0]
    for h in range(ATTN_Q_HEADS):
        acc = jnp.full(bucket.shape, NEG_INF, f32)
        for b in range(N_BUCKETS):
            acc = jnp.where(bucket == b, rb_ref[b, h], acc)
        o_ref[0, h] = acc


def _attn_bias_tables(rel_bias):
    return pl.pallas_call(
        _bias_table_kernel,
        grid=(3,),
        in_specs=[pl.BlockSpec((1, SPAN, BLOCK), lambda v: (v, 0, 0)),
                  pl.BlockSpec(memory_space=pltpu.SMEM)],
        out_specs=pl.BlockSpec((1, ATTN_Q_HEADS, SPAN, BLOCK), lambda v: (v, 0, 0, 0)),
        out_shape=jax.ShapeDtypeStruct((3, ATTN_Q_HEADS, SPAN, BLOCK), f32),
        compiler_params=_cparams(("parallel",)),
        name="attn_bias",
    )(_bias_bucket_tables(), rel_bias.astype(f32))


ATTN_STEP_BLOCKS = 2


def _attn_kernel(q_ref, kv_ref, bias_ref, sink_ref, o_ref, *, seq):
    nb = seq // BLOCK
    scale = HEAD_DIM ** -0.5
    pair_w = 2 * HEAD_DIM
    low = lax.broadcasted_iota(jnp.int32, (BLOCK, pair_w), 1) < HEAD_DIM
    blocks = []
    for sb in range(ATTN_STEP_BLOCKS):
        i = pl.program_id(1) * ATTN_STEP_BLOCKS + sb
        ks = pl.multiple_of(jnp.clip(i * BLOCK - WINDOW, 0, seq - SPAN), BLOCK)
        blocks.append(dict(variant=jnp.where(i == 0, 0, jnp.where(i == nb - 1, 2, 1)),
                           q=q_ref[0, BLOCK * sb:BLOCK * (sb + 1), :], kv=kv_ref[0, pl.ds(ks, SPAN), :]))
    pairs = [(blk, g) for blk in blocks for g in range(ATTN_KV_HEADS)]
    for blk in blocks:
        blk["v_t"] = blk["kv"][:, 2 * KV_W:].astype(f32).T.astype(bf16)
        blk["o_t"] = []
    s_t = []
    for blk, g in pairs:
        kk = blk["kv"][:, pair_w * g:pair_w * (g + 1)]
        qm = []
        for hl in range(ATTN_GROUP):
            h = ATTN_GROUP * g + hl
            qp = blk["q"][:, pair_w * (h // 2):pair_w * (h // 2 + 1)]
            qm.append(jnp.where(low if h % 2 == 0 else jnp.logical_not(low), qp, jnp.zeros_like(qp)))
        s_t.append(_dot_nt(kk, jnp.concatenate(qm, axis=0)))
    p_t = []
    for (blk, g), st in zip(pairs, s_t):
        p_n = []
        for hl in range(ATTN_GROUP):
            h = ATTN_GROUP * g + hl
            s = st[:, BLOCK * hl:BLOCK * (hl + 1)] * scale + bias_ref[blk["variant"], h]
            sk = sink_ref[0:1, h:h + 1]
            m = jnp.maximum(jnp.max(s, 0, keepdims=True), sk)
            p = jnp.exp(s - m)
            denom = jnp.sum(p, 0, keepdims=True) + jnp.exp(sk - m)
            p_n.append((p * (1.0 / denom)).astype(bf16))
        p_t.append(jnp.concatenate(p_n, axis=1))
    for (blk, g), pt in zip(pairs, p_t):
        og = _dot(blk["v_t"][HEAD_DIM * g:HEAD_DIM * (g + 1), :], pt)
        blk["o_t"] += [og[:, BLOCK * hl:BLOCK * (hl + 1)] for hl in range(ATTN_GROUP)]
    for sb, blk in enumerate(blocks):
        o = jnp.concatenate(blk["o_t"], axis=0).T
        o_ref[0, BLOCK * sb:BLOCK * (sb + 1), :] = o.astype(o_ref.dtype)


def _attention(aq, akv, bias_tabs, sink):
    b, s, _ = aq.shape
    tq = BLOCK * ATTN_STEP_BLOCKS
    assert s % tq == 0 and s >= SPAN
    return pl.pallas_call(
        functools.partial(_attn_kernel, seq=s),
        grid=(b, s // tq),
        in_specs=[pl.BlockSpec((1, tq, ATTN_W), lambda bi, i: (bi, i, 0)),
                  pl.BlockSpec((1, s, ATTN_KV_COLS), lambda bi, i: (bi, 0, 0)),
                  _full(bias_tabs.shape),
                  _full((1, ATTN_Q_HEADS))],
        out_specs=pl.BlockSpec((1, tq, ATTN_W), lambda bi, i: (bi, i, 0)),
        out_shape=jax.ShapeDtypeStruct((b, s, ATTN_W), bf16),
        compiler_params=_cparams(("parallel", "arbitrary")),
        name="attn",
    )(aq, akv, bias_tabs, sink.reshape(1, ATTN_Q_HEADS).astype(f32))


def _dn_prep_kernel(u_ref, w_ref, ones_ref, o_ref, pad_ref, *, seq):
    sec = pl.program_id(1)
    zeros = jnp.zeros((CONV_HALO, DN_QK_W), f32)
    pad_ref[0:CONV_HALO, :] = zeros
    pad_ref[seq + CONV_HALO:seq + 2 * CONV_HALO, :] = zeros
    pad_ref[CONV_HALO:seq + CONV_HALO, :] = u_ref[0].astype(f32)
    n = CONV_ROWS + 2 * CONV_HALO

    def conv_silu(t0):
        win = pad_ref[t0:t0 + n, :]
        z = [win * w_ref[0, k:k + 1, :] for k in range(DN_CONV)]
        after = pltpu.roll(z[3] + pltpu.roll(z[4], n - 1, axis=0), n - 1, axis=0)
        before = pltpu.roll(z[1] + pltpu.roll(z[0], 1, axis=0), 1, axis=0)
        return _silu((z[2] + after + before)[CONV_HALO:CONV_HALO + CONV_ROWS])

    @pl.when(sec < 2)
    def _():
        scale = jnp.where(sec == 0, DN_DK ** -0.5, 1.0)
        for t0 in range(0, seq, CONV_ROWS):
            y = conv_silu(t0)
            ss = _dot_split2_rhs(y * y, ones_ref[...])
            o_ref[0, t0:t0 + CONV_ROWS, :] = y * lax.rsqrt(ss + 1e-6) * scale

    @pl.when(sec == 2)
    def _():
        for t0 in range(0, seq, CONV_ROWS):
            o_ref[0, t0:t0 + CONV_ROWS, :] = conv_silu(t0)


def _head_ones():
    blk = np.arange(DN_QK_W) // DN_DK
    return jnp.asarray(blk[:, None] == blk[None, :], dtype=bf16)


def _dn_prep(dqkv, conv_w):
    b, s, _ = dqkv.shape
    w3 = jnp.transpose(conv_w.reshape(DN_CONV, 3, DN_QK_W), (1, 0, 2)).astype(f32)
    return pl.pallas_call(
        functools.partial(_dn_prep_kernel, seq=s),
        grid=(b, 3),
        in_specs=[pl.BlockSpec((1, s, DN_QK_W), lambda bi, c: (bi, 0, c)),
                  pl.BlockSpec((1, DN_CONV, DN_QK_W), lambda bi, c: (c, 0, 0)),
                  _full((DN_QK_W, DN_QK_W))],
        out_specs=pl.BlockSpec((1, s, DN_QK_W), lambda bi, c: (bi, 0, c)),
        out_shape=jax.ShapeDtypeStruct(dqkv.shape, f32),
        scratch_shapes=[pltpu.VMEM((s + 2 * CONV_HALO, DN_QK_W), f32)],
        compiler_params=_cparams(("parallel", "arbitrary")),
        name="dn_prep",
    )(dqkv, w3, _head_ones())


def _softplus(x):
    return jnp.maximum(x, 0.0) + jnp.log1p(jnp.exp(-jnp.abs(x)))


QUAD = 4
QUAD_W = QUAD * DN_DK
N_QUADS = DN_HEADS // QUAD
DELTA_BLOCK_CHUNKS = 4


def _block_diag(xq):
    lane_head = lax.broadcasted_iota(jnp.int32, xq.shape, 1) // DN_DK
    return jnp.concatenate([jnp.where(lane_head == h, xq, 0.0).astype(bf16) for h in range(QUAD)], axis=0)


def _delta_chunk(q, k, v, gx, bx, d):
    C = DN_CHUNK
    W = DN_QK_W
    sign = 1 - 2 * d
    ii = lax.broadcasted_iota(jnp.int32, (C, W), 0)
    jl = lax.broadcasted_iota(jnp.int32, (C, W), 1) % DN_DK
    order = (ii - jl) * sign
    incl = order >= 0
    strict = order > 0
    rr = lax.broadcasted_iota(jnp.int32, (2 * C, 2 * C), 0)
    cc = lax.broadcasted_iota(jnp.int32, (2 * C, 2 * C), 1)
    tri = jnp.where(((rr % C) - cc) * sign >= 0, 1.0, 0.0)
    lmat = jnp.where(cc < C, tri, jnp.where(rr < C, -1.0, 0.0)).astype(bf16)
    rmat = jnp.concatenate([gx, jnp.where(order <= 0, gx, 0.0)], axis=0)
    dg = _dot(jnp.concatenate([lmat] * 2, axis=1), jnp.concatenate(_split2(rmat), axis=0))
    dmat, gc = dg[:C], dg[C:]
    decay = jnp.where(incl, jnp.exp(jnp.where(incl, dmat, 0.0)), 0.0)
    eg = jnp.exp(gc)
    last = C - 1 if d == 0 else 0
    gl = gc[last:last + 1, :]
    kdec = k * jnp.exp(gl - gc)
    cd = jnp.exp(gl)
    kb = k * bx
    eye = jnp.where(ii == jl, 1.0, 0.0)
    return dict(q=q, k=k, bx=bx, decay=decay, strict=strict, eye=eye, kdec=kdec, cd=cd,
                vb=v * bx, kbe=kb * eg, qdec=q * eg)


DELTA_CHAINS = 2 * DELTA_BLOCK_CHUNKS * N_QUADS


def _delta_kernel(qf_ref, kf_ref, vf_ref, qb_ref, kb_ref, vb_ref, abf_ref, abb_ref, alog_ref, dtb_ref,
                  of_ref, ob_ref, state_ref, u_s, w_s, qk_s, qdec_s, kdec_s, cd_s, *, nblk):
    C, H, NC = DN_CHUNK, DN_HEADS, DELTA_BLOCK_CHUNKS
    t = pl.program_id(0)
    slot_w = t % 2
    slot_r = 1 - slot_w

    @pl.when(t == 0)
    def _():
        for ref in (u_s, w_s, qk_s, qdec_s, kdec_s, cd_s):
            ref[1] = jnp.zeros(ref.shape[1:], ref.dtype)

    @pl.when((t + nblk - 1) % nblk == 0)
    def _():
        state_ref[...] = jnp.zeros(state_ref.shape, f32)

    def expand(x, parts):
        rows = lax.broadcasted_iota(jnp.int32, (H * parts, DN_QK_W), 0) % H
        cols = lax.broadcasted_iota(jnp.int32, (H * parts, DN_QK_W), 1) // DN_DK
        emat = jnp.where(rows == cols, 1.0, 0.0).astype(bf16)
        pieces = [p.astype(f32) for p in _split3(x)[:parts]]
        return _dot(jnp.concatenate(pieces, axis=-1).astype(bf16), emat)

    chains = []

    def stage_prepare():
        dirs = ((0, qf_ref, kf_ref, vf_ref, abf_ref), (1, qb_ref, kb_ref, vb_ref, abb_ref))
        for d, q_ref, k_ref, v_ref, ab_ref in dirs:
            ab = ab_ref[0, 0]
            g = -jnp.exp(alog_ref[d]) * _softplus(ab[:, :H] + dtb_ref[d])
            gx = expand(g, 3)
            bx = expand(_sigmoid(ab[:, H:]), 2)
            for c in range(NC):
                rows = slice(C * c, C * (c + 1))
                pre = _delta_chunk(q_ref[0, rows, :], k_ref[0, rows, :], v_ref[0, rows, :], gx[rows], bx[rows], d)
                for qd in range(N_QUADS):
                    sl = slice(QUAD_W * qd, QUAD_W * (qd + 1))
                    chains.append({n: x[:, sl] for n, x in pre.items()})

    def stage_gram():
        for ch in chains:
            kq = _dot_nt(jnp.concatenate([ch["k"], ch["q"]], axis=0).astype(bf16), _block_diag(ch["k"]))
            a_mat = jnp.where(ch["strict"], kq[:C] * ch["bx"] * ch["decay"], 0.0)
            ch["qk"] = kq[C:] * ch["decay"]
            ch["p"] = -a_mat
            ch["t"] = ch["eye"] - a_mat

    def stage_square():
        for ch in chains:
            ch["p"] = _dot(ch["p"].astype(bf16), _block_diag(ch["p"]))

    def stage_double():
        for ch in chains:
            pt = _dot(jnp.concatenate([ch["p"], ch["t"]], axis=0).astype(bf16), _block_diag(ch["p"]))
            ch["p"], ch["t"] = pt[:C], ch["t"] + pt[C:]

    def stage_last():
        for ch in chains:
            ch["tb"] = (ch["t"] + _dot(ch["t"].astype(bf16), _block_diag(ch["p"]))).astype(bf16)

    def stage_solve():
        for ch in chains:
            ch["u"] = _dot(ch["tb"], _block_diag(ch["vb"]))
            ch["w"] = _dot(ch["tb"], _block_diag(ch["kbe"]))

    row_head = lax.broadcasted_iota(jnp.int32, (QUAD_W, QUAD_W), 0) // DN_DK
    col_head = lax.broadcasted_iota(jnp.int32, (QUAD_W, QUAD_W), 1) // DN_DK
    o_refs = (of_ref, ob_ref)
    live = {}

    def scan_chains(step):
        for d in range(2):
            c = step if d == 0 else NC - 1 - step
            for qd in range(N_QUADS):
                yield d, c, qd, (d * NC + c) * N_QUADS + qd

    def scan_first(step):
        for d, c, qd, n in scan_chains(step):
            st = state_ref[d, qd]
            wq = _dot(jnp.concatenate([w_s[slot_r, n], qdec_s[slot_r, n]], axis=0), st.astype(bf16))
            live[n] = (st, u_s[slot_r, n] - wq[:C], wq[C:])

    def scan_second(step):
        for d, c, qd, n in scan_chains(step):
            st, v_new, qs = live.pop(n)
            o = qs + _dot(qk_s[slot_r, n], _block_diag(v_new))
            o_refs[d][0, C * c:C * (c + 1), QUAD_W * qd:QUAD_W * (qd + 1)] = o
            s_new = _dot_tn(kdec_s[slot_r, n], v_new.astype(bf16))
            state_ref[d, qd] = st * cd_s[slot_r, n, 0:1, :] + jnp.where(row_head == col_head, s_new, 0.0)

    local = [stage_prepare, stage_gram, stage_square] + [stage_double] * 4 + [stage_last, stage_solve]
    scan = [f for step in range(NC) for f in (functools.partial(scan_first, step),
                                              functools.partial(scan_second, step))]
    for k in range(max(len(local), len(scan))):
        if k < len(local):
            local[k]()
        if k < len(scan):
            scan[k]()

    for n, ch in enumerate(chains):
        u_s[slot_w, n] = ch["u"]
        w_s[slot_w, n] = ch["w"].astype(bf16)
        qk_s[slot_w, n] = ch["qk"].astype(bf16)
        qdec_s[slot_w, n] = ch["qdec"].astype(bf16)
        kdec_s[slot_w, n] = ch["kdec"].astype(bf16)
        cd_s[slot_w, n] = jnp.broadcast_to(ch["cd"], (8, QUAD_W))


def _delta(qkvn, dab, a_log, dt_bias):
    b, s, _ = qkvn.shape
    H = DN_HEADS
    tb = DN_CHUNK * DELTA_BLOCK_CHUNKS
    nblk = s // tb
    total = b * nblk
    assert s % tb == 0
    d4 = dab.reshape(b, s, 4, H)
    ab = jnp.stack([jnp.concatenate([d4[:, :, r], d4[:, :, 2 + r]], -1) for r in range(2)])
    alog = a_log.astype(f32).reshape(2, 1, H)
    dtb = dt_bias.astype(f32).reshape(2, 1, H)

    def block(lin, rev):
        i = lin % nblk
        return lin // nblk, (nblk - 1 - i) if rev else i

    def sec(n, rev, lag):
        def index(t):
            bi, i = block(jnp.maximum(t - 1, 0) if lag else jnp.minimum(t, total - 1), rev)
            return bi, i, n
        return pl.BlockSpec((1, tb, DN_QK_W), index)

    def ab_spec(d):
        def index(t):
            bi, i = block(jnp.minimum(t, total - 1), d == 1)
            return d, bi, i, 0
        return pl.BlockSpec((1, 1, tb, 2 * H), index)

    chain_buf = lambda dt: pltpu.VMEM((2, DELTA_CHAINS, DN_CHUNK, QUAD_W), dt)
    return pl.pallas_call(
        functools.partial(_delta_kernel, nblk=nblk),
        grid=(total + 1,),
        in_specs=[sec(0, False, False), sec(1, False, False), sec(2, False, False),
                  sec(0, True, False), sec(1, True, False), sec(2, True, False),
                  ab_spec(0), ab_spec(1), _full((2, 1, H)), _full((2, 1, H))],
        out_specs=[sec(0, False, True), sec(0, True, True)],
        out_shape=[jax.ShapeDtypeStruct((b, s, DN_V_W), f32)] * 2,
        scratch_shapes=[pltpu.VMEM((2, N_QUADS, QUAD_W, QUAD_W), f32),
                        chain_buf(f32), chain_buf(bf16), chain_buf(bf16), chain_buf(bf16), chain_buf(bf16),
                        pltpu.VMEM((2, DELTA_CHAINS, 8, QUAD_W), f32)],
        compiler_params=_cparams(("arbitrary",)),
        name="delta",
    )(qkvn, qkvn, qkvn, qkvn, qkvn, qkvn, ab, ab, alog, dtb)


def _pool_band_tables():
    r = np.arange(POOL_TILE)[:, None]
    j = np.arange(POOL_SPAN)[None, :]
    tabs = np.zeros((3, N_POOL_GROUPS, POOL_TILE, POOL_SPAN), np.float32)
    for vi, shift in enumerate((0, 128, 256)):
        rel = j - shift - r
        for gi, w in enumerate(POOL_WINDOWS):
            tabs[vi, gi] = np.abs(rel) <= w // 2
    return jnp.asarray(tabs, dtype=bf16)


def _pool_tile(u_ref, band_ref, w_ref, scale_ref, t, seq):
    nt = seq // POOL_TILE
    t0 = pl.multiple_of(t * POOL_TILE, POOL_TILE)
    ks = pl.multiple_of(jnp.clip(t0 - 128, 0, seq - POOL_SPAN), 128)
    variant = jnp.where(t == 0, 0, jnp.where(t == nt - 1, 2, 1))
    win = u_ref[0, pl.ds(ks, POOL_SPAN), :]
    own = u_ref[0, pl.ds(t0, POOL_TILE), :].astype(f32)
    pos = t0 + lax.broadcasted_iota(jnp.int32, (POOL_TILE, POOL_GROUP), 0)
    outs = []
    for gi, w in enumerate(POOL_WINDOWS):
        r = w // 2
        sl = slice(POOL_GROUP * gi, POOL_GROUP * (gi + 1))
        wsum = _dot(band_ref[variant, gi], win[:, sl])
        count = (jnp.minimum(pos + r, seq - 1) - jnp.maximum(pos - r, 0) + 1).astype(f32)
        mixed = wsum / count - own[:, sl]
        outs.append(_dot(mixed.astype(bf16), w_ref[gi]))
    return jnp.concatenate(outs, -1) * scale_ref[...]


def _route_tile(x, w, carry_ref):
    xh, xl = _split2(x)
    wh, wl = _split2(w)
    logits = _dot(xh, wh) + (_dot(xl, wh) + _dot(xh, wl))
    tm = logits.shape[0]
    lane = lax.broadcasted_iota(jnp.int32, logits.shape, 1)
    m1 = jnp.max(logits, -1, keepdims=True)
    i1 = jnp.min(jnp.where(logits == m1, lane, N_EXPERTS), -1, keepdims=True)
    rest = jnp.where(lane == i1, -jnp.inf, logits)
    m2 = jnp.max(rest, -1, keepdims=True)
    i2 = jnp.min(jnp.where(rest == m2, lane, N_EXPERTS), -1, keepdims=True)
    e2 = jnp.exp(m2 - m1)
    p1 = 1.0 / (1.0 + e2)
    p2 = e2 / (1.0 + e2)
    oh1 = jnp.where(lane == i1, 1.0, 0.0)
    oh2 = jnp.where(lane == i2, 1.0, 0.0)
    both = oh1 + oh2
    rr = lax.broadcasted_iota(jnp.int32, (tm, tm), 0)
    cc = lax.broadcasted_iota(jnp.int32, (tm, tm), 1)
    earlier = jnp.where(cc < rr, 1.0, 0.0).astype(bf16)
    before = _dot(earlier, both.astype(bf16)) + carry_ref[...]
    r1 = jnp.sum(oh1 * before, -1, keepdims=True).astype(jnp.int32)
    r2 = jnp.sum(oh2 * before, -1, keepdims=True).astype(jnp.int32)
    carry_ref[...] += jnp.sum(both, 0, keepdims=True)
    mi = jnp.where(lane == 0, i1, jnp.where(lane == 1, i2, jnp.where(lane == 2, r1, jnp.where(lane == 3, r2, 0))))
    mf = jnp.where(lane == 0, p1, jnp.where(lane == 1, p2, 0.0))
    return mi, mf


def _merge_kernel(oa_ref, odf_ref, odb_ref, z_ref, pu_ref, gate_ref, h_ref, ones_ref, nw_ref, band_ref, pw_ref,
                  ps_ref, wb_ref, wo_ref, g_ref, b_ref, *refs, seq, routed):
    if routed:
        rw_ref, hf_ref, hb_ref, mi_ref, mf_ref, cnt_ref, carry_ref = refs

        @pl.when(pl.program_id(0) == 0)
        def _():
            carry_ref[...] = jnp.zeros(carry_ref.shape, f32)
    else:
        hf_ref, hb_ref = refs
    od = odf_ref[...] + odb_ref[...]
    ms = _dot_split2_rhs(od * od, ones_ref[...]) * (1.0 / DN_DV)
    od = od * lax.rsqrt(ms + RMS_EPS) * nw_ref[...] * _silu(z_ref[...].astype(f32))
    op = _pool_tile(pu_ref, band_ref, pw_ref, ps_ref, pl.program_id(0) % (seq // POOL_TILE), seq)
    branches = (oa_ref[...], od.astype(bf16), op.astype(bf16))
    merged = None
    for n in range(N_BRANCH):
        up = _dot(branches[n], wb_ref[n])
        gate = _sigmoid(gate_ref[:, D_MODEL * n:D_MODEL * (n + 1)].astype(f32))
        merged = gate * up if merged is None else merged + gate * up
    y = _dot(merged.astype(bf16), wo_ref[...])
    hn = _layer_norm(DEEPNORM_ALPHA * h_ref[...] + y, g_ref[...], b_ref[...])
    hf_ref[...] = hn
    if routed:
        hb_ref[...] = _pack_bf16_pair(hn[:, :D_MODEL // 2], hn[:, D_MODEL // 2:])
        mi_ref[...], mf_ref[...] = _route_tile(hn, rw_ref[...], carry_ref)
        cnt_ref[...] = carry_ref[...].astype(jnp.int32)
    else:
        hb_ref[...] = hn.astype(bf16)


def _merge(oa, odf, odb, z, pu, gates, h, dn_norm_w, pool_w, pool_scale, w_branch, w_out, ln_g, ln_b, router_w):
    m = h.shape[0]
    bsz, seq, _ = pu.shape
    tm = POOL_TILE
    nt = seq // tm
    assert seq % tm == 0 and seq >= POOL_SPAN
    routed = router_w is not None
    row = lambda w: pl.BlockSpec((tm, w), lambda i: (i, 0))
    nw = jnp.tile(dn_norm_w.astype(f32), DN_HEADS).reshape(1, DN_V_W)
    in_specs = [row(ATTN_W), row(DN_V_W), row(DN_V_W), row(DN_V_W),
                pl.BlockSpec((1, seq, POOL_W), lambda i: (i // nt, 0, 0)),
                row(N_BRANCH * D_MODEL), row(D_MODEL),
                _full((DN_V_W, DN_V_W)), _full((1, DN_V_W)),
                _full((3, N_POOL_GROUPS, POOL_TILE, POOL_SPAN)), _full((N_POOL_GROUPS, POOL_GROUP, POOL_GROUP)),
                _full((1, POOL_W)),
                _full((N_BRANCH, ATTN_W, D_MODEL)), _full((D_MODEL, D_MODEL)),
                _full((1, D_MODEL)), _full((1, D_MODEL))]
    args = [oa, odf, odb, z, pu, gates, h, _head_ones(), nw, _pool_band_tables(), pool_w.astype(bf16),
            pool_scale.reshape(1, POOL_W).astype(f32), w_branch.astype(bf16), w_out.astype(bf16),
            ln_g.reshape(1, D_MODEL).astype(f32), ln_b.reshape(1, D_MODEL).astype(f32)]
    out_specs = [row(D_MODEL)]
    out_shape = [jax.ShapeDtypeStruct((m, D_MODEL), f32)]
    scratch = []
    if routed:
        in_specs.append(_full((D_MODEL, N_EXPERTS)))
        args.append(router_w.astype(f32))
        out_specs += [row(PACKED_W), row(N_EXPERTS), row(N_EXPERTS), _full((1, N_EXPERTS))]
        out_shape += [jax.ShapeDtypeStruct((m, PACKED_W), jnp.uint32),
                      jax.ShapeDtypeStruct((m, N_EXPERTS), jnp.int32),
                      jax.ShapeDtypeStruct((m, N_EXPERTS), f32),
                      jax.ShapeDtypeStruct((1, N_EXPERTS), jnp.int32)]
        scratch.append(pltpu.VMEM((1, N_EXPERTS), f32))
    else:
        out_specs.append(row(D_MODEL))
        out_shape.append(jax.ShapeDtypeStruct((m, D_MODEL), bf16))
    return pl.pallas_call(
        functools.partial(_merge_kernel, seq=seq, routed=routed),
        grid=(m // tm,),
        in_specs=in_specs,
        out_specs=out_specs,
        out_shape=out_shape,
        scratch_shapes=scratch,
        compiler_params=_cparams(("arbitrary",)),
        name="merge",
    )(*args)


def _swiglu_step(x, wg_ref, wu_ref, wd_ref, acc_ref):
    g = _dot(x, wg_ref[0].astype(bf16))
    u = _dot(x, wu_ref[0].astype(bf16))
    acc_ref[...] += _dot((_silu(g) * u).astype(bf16), wd_ref[0].astype(bf16))


def _ffn_kernel(x_ref, h_ref, wg_ref, wu_ref, wd_ref, g_ref, b_ref, hf_ref, hb_ref, acc_ref):
    j = pl.program_id(1)

    @pl.when(j == 0)
    def _():
        acc_ref[...] = jnp.zeros(acc_ref.shape, f32)

    _swiglu_step(x_ref[...], wg_ref, wu_ref, wd_ref, acc_ref)

    @pl.when(j == pl.num_programs(1) - 1)
    def _():
        hn = _layer_norm(DEEPNORM_ALPHA * h_ref[...] + acc_ref[...], g_ref[...], b_ref[...])
        hf_ref[...] = hn
        hb_ref[...] = hn.astype(bf16)


def _ffn(hb, hf, w1, w3, w2, ln_g, ln_b, tm=1024, tf=256):
    m = hb.shape[0]
    tm = min(tm, m)
    ff = w1.shape[-1]
    row = pl.BlockSpec((tm, D_MODEL), lambda i, j: (i, 0))
    return pl.pallas_call(
        _ffn_kernel,
        grid=(m // tm, ff // tf),
        in_specs=[row, row,
                  pl.BlockSpec((1, D_MODEL, tf), lambda i, j: (0, 0, j)),
                  pl.BlockSpec((1, D_MODEL, tf), lambda i, j: (0, 0, j)),
                  pl.BlockSpec((1, tf, D_MODEL), lambda i, j: (0, j, 0)),
                  pl.BlockSpec((1, D_MODEL), lambda i, j: (0, 0)),
                  pl.BlockSpec((1, D_MODEL), lambda i, j: (0, 0))],
        out_specs=[row, row],
        out_shape=[jax.ShapeDtypeStruct((m, D_MODEL), f32), jax.ShapeDtypeStruct((m, D_MODEL), bf16)],
        scratch_shapes=[pltpu.VMEM((tm, D_MODEL), f32)],
        compiler_params=_cparams(("parallel", "arbitrary")),
        name="ffn",
    )(hb, hf, w1[None], w3[None], w2[None],
      ln_g.reshape(1, D_MODEL).astype(f32), ln_b.reshape(1, D_MODEL).astype(f32))


MOE_TILE = 1024
PACKED_W = D_MODEL // 2
SC_CORES = 2
SC_SUBCORES = 16
SC_WORKERS = SC_CORES * SC_SUBCORES
SC_ROWS = 64


def _gmm_kernel(te_ref, tv_ref, x_ref, wg_ref, wu_ref, wd_ref, o_ref, acc_ref):
    i = pl.program_id(0)
    j = pl.program_id(1)
    valid = tv_ref[i]

    @pl.when(j == 0)
    def _():
        acc_ref[...] = jnp.zeros(acc_ref.shape, f32)

    @pl.when(valid > 0)
    def _():
        rows = lax.broadcasted_iota(jnp.int32, acc_ref.shape, 0)
        x = jnp.where(rows < valid, _unpack_bf16_pair(x_ref[...]), 0.0).astype(bf16)
        _swiglu_step(x, wg_ref, wu_ref, wd_ref, acc_ref)

    @pl.when(j == pl.num_programs(1) - 1)
    def _():
        o_ref[...] = _pack_bf16_pair(acc_ref[:, :PACKED_W], acc_ref[:, PACKED_W:])


def _gmm(xs, tile_expert, tile_valid, wg, wu, wd, tm, tf=512):
    n = xs.shape[0]
    ff = wg.shape[-1]
    nj = ff // tf

    def fcol(i, j, te, tv):
        return jnp.where(tv[i] > 0, j, nj - 1)

    grid_spec = pltpu.PrefetchScalarGridSpec(
        num_scalar_prefetch=2,
        grid=(n // tm, nj),
        in_specs=[pl.BlockSpec((tm, PACKED_W), lambda i, j, te, tv: (i, 0)),
                  pl.BlockSpec((1, D_MODEL, tf), lambda i, j, te, tv: (te[i], 0, fcol(i, j, te, tv))),
                  pl.BlockSpec((1, D_MODEL, tf), lambda i, j, te, tv: (te[i], 0, fcol(i, j, te, tv))),
                  pl.BlockSpec((1, tf, D_MODEL), lambda i, j, te, tv: (te[i], fcol(i, j, te, tv), 0))],
        out_specs=pl.BlockSpec((tm, PACKED_W), lambda i, j, te, tv: (i, 0)),
        scratch_shapes=[pltpu.VMEM((tm, D_MODEL), f32)])
    return pl.pallas_call(
        _gmm_kernel,
        grid_spec=grid_spec,
        out_shape=jax.ShapeDtypeStruct((n, PACKED_W), jnp.uint32),
        compiler_params=_cparams(("parallel", "arbitrary")),
        name="moe_gmm",
    )(tile_expert, tile_valid, xs, wg, wu, wd)


def _sc_mesh():
    return plsc.VectorSubcoreMesh(core_axis_name="c", subcore_axis_name="s",
                                  num_cores=SC_CORES, num_subcores=SC_SUBCORES)


def _sc_worker_base(per_worker):
    return (lax.axis_index("s") * SC_CORES + lax.axis_index("c")) * per_worker


def _sc_scratch(d, dtype):
    return [pltpu.VMEM((SC_ROWS,), jnp.int32), pltpu.VMEM((SC_ROWS,), jnp.int32), pltpu.VMEM((2, SC_ROWS, d), dtype),
            pltpu.SemaphoreType.DMA((2,)), pltpu.SemaphoreType.DMA((2,))]


def _sc_scatter_rows(src, idx0, idx1, n_out):
    m, d = src.shape
    per_worker = m // SC_WORKERS
    n_chunks = per_worker // SC_ROWS
    assert m % (SC_WORKERS * SC_ROWS * 2) == 0

    def body(src_hbm, idx0_hbm, idx1_hbm, out_hbm, idx_a, idx_b, rows_v, sem_in, sem_out):
        base = _sc_worker_base(per_worker)
        idx_v = (idx_a, idx_b)

        def load(j, slot):
            off = pl.multiple_of(base + j * SC_ROWS, SC_ROWS)
            return pltpu.make_async_copy(src_hbm.at[pl.ds(off, SC_ROWS)], rows_v.at[slot], sem_in.at[slot])

        def store(j, slot, idx_hbm):
            off = pl.multiple_of(base + j * SC_ROWS, SC_ROWS)
            pltpu.sync_copy(idx_hbm.at[pl.ds(off, SC_ROWS)], idx_v[slot])
            pltpu.async_copy(rows_v.at[slot], out_hbm.at[idx_v[slot]], sem_out.at[slot]).wait()

        load(0, 0).start()

        @pl.loop(0, n_chunks, step=2)
        def _(j):
            for slot in range(2):
                @pl.when(j + slot + 1 < n_chunks)
                def _():
                    load(j + slot + 1, 1 - slot).start()
                load(j + slot, slot).wait()
                store(j + slot, slot, idx0_hbm)
                store(j + slot, slot, idx1_hbm)

    return pl.kernel(
        body, out_type=jax.ShapeDtypeStruct((n_out, d), src.dtype), mesh=_sc_mesh(),
        scratch_types=_sc_scratch(d, src.dtype), name="moe_dispatch",
    )(src, idx0, idx1)


def _sc_gather_rows(table, idx):
    n = idx.shape[0]
    d = table.shape[1]
    per_worker = n // SC_WORKERS
    n_chunks = per_worker // SC_ROWS
    assert n % (SC_WORKERS * SC_ROWS * 2) == 0

    def body(table_hbm, idx_hbm, out_hbm, idx_a, idx_b, rows_v, sem_in, sem_out):
        base = _sc_worker_base(per_worker)
        idx_v = (idx_a, idx_b)

        def gather(j, slot):
            off = pl.multiple_of(base + j * SC_ROWS, SC_ROWS)
            pltpu.sync_copy(idx_hbm.at[pl.ds(off, SC_ROWS)], idx_v[slot])
            return pltpu.make_async_copy(table_hbm.at[idx_v[slot]], rows_v.at[slot], sem_in.at[slot])

        def wait_gather(slot):
            pltpu.make_async_copy(table_hbm.at[idx_v[slot]], rows_v.at[slot], sem_in.at[slot]).wait()

        def write(j, slot):
            off = pl.multiple_of(base + j * SC_ROWS, SC_ROWS)
            return pltpu.make_async_copy(rows_v.at[slot], out_hbm.at[pl.ds(off, SC_ROWS)], sem_out.at[slot])

        gather(0, 0).start()

        @pl.loop(0, n_chunks, step=2)
        def _(j):
            for slot in range(2):
                @pl.when(j + slot + 1 < n_chunks)
                def _():
                    @pl.when(j + slot >= 1)
                    def _():
                        write(j + slot - 1, 1 - slot).wait()
                    gather(j + slot + 1, 1 - slot).start()
                wait_gather(slot)
                write(j + slot, slot).start()

        write(n_chunks - 2, 0).wait()
        write(n_chunks - 1, 1).wait()

    return pl.kernel(
        body, out_type=jax.ShapeDtypeStruct((n, d), table.dtype), mesh=_sc_mesh(),
        scratch_types=_sc_scratch(d, table.dtype), name="moe_collect",
    )(table, idx)


def _combine_kernel(y0_ref, y1_ref, mf_ref, h_ref, g_ref, b_ref, o_ref):
    p = mf_ref[...]
    y = p[:, 0:1] * _unpack_bf16_pair(y0_ref[0]) + p[:, 1:2] * _unpack_bf16_pair(y1_ref[0])
    o_ref[...] = _layer_norm(DEEPNORM_ALPHA * h_ref[...] + y, g_ref[...], b_ref[...])


def _combine(yg, mf, hf, ln_g, ln_b, tm=512):
    m = hf.shape[0]
    tm = min(tm, m)
    return pl.pallas_call(
        _combine_kernel,
        grid=(m // tm,),
        in_specs=[pl.BlockSpec((1, tm, PACKED_W), lambda i: (0, i, 0)),
                  pl.BlockSpec((1, tm, PACKED_W), lambda i: (1, i, 0)),
                  pl.BlockSpec((tm, N_EXPERTS), lambda i: (i, 0)),
                  pl.BlockSpec((tm, D_MODEL), lambda i: (i, 0)),
                  _full((1, D_MODEL)), _full((1, D_MODEL))],
        out_specs=pl.BlockSpec((tm, D_MODEL), lambda i: (i, 0)),
        out_shape=jax.ShapeDtypeStruct((m, D_MODEL), f32),
        compiler_params=_cparams(("parallel",)),
        name="moe_combine",
    )(yg, yg, mf, hf, ln_g.reshape(1, D_MODEL).astype(f32), ln_b.reshape(1, D_MODEL).astype(f32))


def _moe(hf, hp, mi, mf, cnt, wg, wu, wd, ln_g, ln_b):
    m = hf.shape[0]
    tm = MOE_TILE
    counts = cnt[0]
    padded = (counts + tm - 1) // tm * tm
    ends = jnp.cumsum(padded)
    starts = ends - padded
    experts = jnp.arange(N_EXPERTS, dtype=jnp.int32)
    pos = jnp.sum(jnp.where(mi[:, 0:TOP_K, None] == experts, starts, 0), -1) + mi[:, TOP_K:2 * TOP_K]
    n_tiles = TOP_K * m // tm + N_EXPERTS
    tile_start = jnp.arange(n_tiles, dtype=jnp.int32) * tm
    tile_expert = jnp.minimum(jnp.sum(tile_start[:, None] >= ends[None, :], -1), N_EXPERTS - 1).astype(jnp.int32)
    tile_valid = jnp.clip(starts[tile_expert] + counts[tile_expert] - tile_start, 0, tm).astype(jnp.int32)
    tile_valid = jnp.where(tile_start < ends[-1], tile_valid, 0)
    xs = _sc_scatter_rows(hp, pos[:, 0], pos[:, 1], n_tiles * tm)
    ys = _gmm(xs, tile_expert, tile_valid, wg, wu, wd, tm)
    yg = _sc_gather_rows(ys, jnp.concatenate([pos[:, 0], pos[:, 1]]))
    return _combine(yg.reshape(TOP_K, m, PACKED_W), mf, hf, ln_g, ln_b)


_SPLITS = (ATTN_W, KV_W, KV_W, 3 * DN_QK_W, DN_V_W, 4 * DN_HEADS, POOL_W, N_BRANCH * D_MODEL)
_SPLIT_DTYPES = (bf16, bf16, f32, bf16, f32, bf16, bf16)


def _mixer(hf, hb, bsz, seq, bias_tabs, w_in, sink, conv_w, a_log, dt_bias, dn_norm_w, pool_w, pool_scale,
           w_branch, w_out, ln_g, ln_b, router_w):
    points = np.cumsum(_SPLITS)[:-1].tolist()
    wq, wk, wv, *rest = jnp.split(w_in.astype(bf16), points, axis=-1)
    wkv = jnp.concatenate([wk[:, HEAD_DIM * g:HEAD_DIM * (g + 1)] for g in range(ATTN_KV_HEADS) for _ in range(2)]
                          + [wv], axis=-1)
    aq, akv, dqkv, dz, dab, pu, gates = _inproj(hf if hb is None else hb, [wq, wkv] + rest, _SPLIT_DTYPES)
    shp = lambda t: t.reshape(bsz, seq, t.shape[-1])
    oa = _attention(shp(aq), shp(akv), bias_tabs, sink)
    qkvn = _dn_prep(shp(dqkv), conv_w)
    odf, odb = _delta(qkvn, shp(dab), a_log, dt_bias)
    m = bsz * seq
    return _merge(oa.reshape(m, ATTN_W), odf.reshape(m, DN_V_W), odb.reshape(m, DN_V_W), dz, shp(pu), gates, hf,
                  dn_norm_w, pool_w, pool_scale, w_branch, w_out, ln_g, ln_b, router_w)


def kernel(x, w_in, attn_sink, rel_bias, conv_w, a_log, dt_bias, dn_norm_w, pool_w, pool_scale, w_branch, w_out,
           ln1_g, ln1_b, ln2_g, ln2_b, ffn_w1, ffn_w3, ffn_w2, router_w, moe_wg, moe_wu, moe_wd):
    bsz, seq, _ = x.shape
    m = bsz * seq
    bias_tabs = _attn_bias_tables(rel_bias)
    hf = x.reshape(m, D_MODEL).astype(f32)
    hb = None
    for layer in range(DEPTH):
        dense = layer % 2 == 0
        i = layer // 2
        hf, *rest = _mixer(hf, hb, bsz, seq, bias_tabs, w_in[layer], attn_sink[layer], conv_w[layer], a_log[layer],
                           dt_bias[layer], dn_norm_w[layer], pool_w[layer], pool_scale[layer], w_branch[layer],
                           w_out[layer], ln1_g[layer], ln1_b[layer], None if dense else router_w[i])
        if dense:
            hf, hb = _ffn(rest[0], hf, ffn_w1[i], ffn_w3[i], ffn_w2[i], ln2_g[layer], ln2_b[layer])
        else:
            hf = _moe(hf, *rest, moe_wg[i], moe_wu[i], moe_wd[i], ln2_g[layer], ln2_b[layer])
            hb = None
    return hf.reshape(bsz, seq, D_MODEL).astype(x.dtype)
```

```python
import functools

import numpy as np
import jax
import jax.numpy as jnp
from jax import lax
from jax.experimental import pallas as pl
from jax.experimental.pallas import tpu as pltpu
from jax.experimental.pallas import tpu_sc as plsc

f32 = jnp.float32
bf16 = jnp.bfloat16

D_MODEL = 1024
DEPTH = 2
HEAD_DIM = 64
ATTN_Q_HEADS = 8
ATTN_KV_HEADS = 2
ATTN_GROUP = ATTN_Q_HEADS // ATTN_KV_HEADS
WINDOW = 128
BLOCK = 128
SPAN = BLOCK + 2 * WINDOW
N_BUCKETS = 32
MAX_DISTANCE = 128
DN_HEADS = 8
DN_DK = 64
DN_DV = 64
DN_CONV = 5
DN_CHUNK = 64
POOL_WINDOWS = (2, 4, 8, 16)
N_POOL_GROUPS = 4
POOL_GROUP = 128
ATTN_W = ATTN_Q_HEADS * HEAD_DIM
KV_W = ATTN_KV_HEADS * HEAD_DIM
DN_QK_W = DN_HEADS * DN_DK
DN_V_W = DN_HEADS * DN_DV
POOL_W = N_POOL_GROUPS * POOL_GROUP
N_BRANCH = 3
N_EXPERTS = 8
TOP_K = 2
DEEPNORM_ALPHA = (2.0 * DEPTH) ** 0.25
LN_EPS = 1e-5
RMS_EPS = 1e-6
NEG_INF = -1e30

VMEM_LIMIT_BYTES = 56 * 1024 * 1024
POOL_TILE = 256
POOL_SPAN = 512


def _cparams(sem):
    return pltpu.CompilerParams(dimension_semantics=sem, vmem_limit_bytes=VMEM_LIMIT_BYTES)


def _full(shape):
    n = len(shape)
    return pl.BlockSpec(shape, lambda *_: (0,) * n)


def _sigmoid(x):
    return 0.5 * jnp.tanh(0.5 * x) + 0.5


def _silu(x):
    return x * _sigmoid(x)


def _layer_norm(x, g, b):
    mu = jnp.mean(x, -1, keepdims=True)
    xc = x - mu
    var = jnp.mean(xc * xc, -1, keepdims=True)
    return xc * lax.rsqrt(var + LN_EPS) * g + b


def _dot(a, b):
    return jnp.dot(a, b, preferred_element_type=f32)


def _dot_nt(a, b):
    return lax.dot_general(a, b, (((1,), (1,)), ((), ())), preferred_element_type=f32)


def _dot_tn(a, b):
    return lax.dot_general(a, b, (((0,), (0,)), ((), ())), preferred_element_type=f32)


def _split3(x):
    hi = x.astype(bf16)
    r = x - hi.astype(f32)
    mid = r.astype(bf16)
    lo = (r - mid.astype(f32)).astype(bf16)
    return hi, mid, lo


def _split2(x):
    hi = x.astype(bf16)
    return hi, (x - hi.astype(f32)).astype(bf16)


def _dot_split2_rhs(x, b_bf):
    hi, lo = _split2(x)
    return _dot(hi, b_bf) + _dot(lo, b_bf)


def _pack_bf16_pair(a, b):
    def rounded(x):
        bits = lax.bitcast_convert_type(x, jnp.uint32)
        return bits + jnp.uint32(0x7FFF) + ((bits >> 16) & jnp.uint32(1))
    return (rounded(b) & jnp.uint32(0xFFFF0000)) | (rounded(a) >> 16)


def _unpack_bf16_pair(p):
    lo = lax.bitcast_convert_type(p << 16, f32)
    hi = lax.bitcast_convert_type(p & jnp.uint32(0xFFFF0000), f32)
    return jnp.concatenate([lo, hi], axis=-1)


INPROJ_TILE = 256
DN_HALO = 16


def _head_ones():
    blk = np.arange(DN_QK_W) // DN_DK
    return jnp.asarray(blk[:, None] == blk[None, :], dtype=bf16)


def _inproj_kernel(x_ref, xp_ref, xn_ref, cw_ref, ones_ref, *refs, nt, dn):
    n = len(refs) // 2
    w_refs, o_refs = refs[:n], refs[n:]
    x = x_ref[...].astype(bf16)
    t = pl.program_id(0) % nt
    x_ext = jnp.concatenate([xp_ref[...].astype(bf16), x, xn_ref[...].astype(bf16)], axis=0)
    rows = x_ext.shape[0]
    row = lax.broadcasted_iota(jnp.int32, (rows, DN_QK_W), 0)
    first = jnp.where(t == 0, DN_HALO, 0)
    end = jnp.where(t == nt - 1, rows - DN_HALO, rows)
    inside = (row >= first) & (row < end)
    u = _dot(x_ext, w_refs[dn][...])
    jobs = [(idx, c0) for idx in range(n) if idx != dn for c0 in range(0, o_refs[idx].shape[1], D_MODEL)]
    jobs.sort(key=lambda job: -min(D_MODEL, o_refs[job[0]].shape[1] - job[1]))
    for sec in range(3):
        for idx, c0 in jobs[sec::3]:
            c1 = min(c0 + D_MODEL, o_refs[idx].shape[1])
            o_refs[idx][:, c0:c1] = _dot(x, w_refs[idx][:, c0:c1]).astype(o_refs[idx].dtype)
        cols = slice(DN_QK_W * sec, DN_QK_W * (sec + 1))
        us = jnp.where(inside, u[:, cols], 0.0)
        z = [us * cw_ref[k:k + 1, cols] for k in range(DN_CONV)]
        after = pltpu.roll(z[3] + pltpu.roll(z[4], rows - 1, axis=0), rows - 1, axis=0)
        before = pltpu.roll(z[1] + pltpu.roll(z[0], 1, axis=0), 1, axis=0)
        y = _silu((z[2] + after + before)[DN_HALO:rows - DN_HALO])
        if sec < 2:
            ss = _dot((y * y).astype(bf16), ones_ref[...])
            y = y * lax.rsqrt(ss + 1e-6) * (DN_DK ** -0.5 if sec == 0 else 1.0)
        o_refs[dn][:, cols] = y


def _inproj(x, ws, dtypes, conv_w, seq, dn):
    m = x.shape[0]
    tm = INPROJ_TILE
    halo_blocks = tm // DN_HALO
    assert seq % tm == 0 and tm % DN_HALO == 0 and DN_CONV == 5
    last = m // DN_HALO - 1
    return pl.pallas_call(
        functools.partial(_inproj_kernel, nt=seq // tm, dn=dn),
        grid=(m // tm,),
        in_specs=[pl.BlockSpec((tm, D_MODEL), lambda i: (i, 0)),
                  pl.BlockSpec((DN_HALO, D_MODEL), lambda i: (jnp.maximum(i * halo_blocks - 1, 0), 0)),
                  pl.BlockSpec((DN_HALO, D_MODEL), lambda i: (jnp.minimum((i + 1) * halo_blocks, last), 0)),
                  _full(conv_w.shape), _full((DN_QK_W, DN_QK_W))]
        + [pl.BlockSpec(w.shape, lambda i: (0, 0), pipeline_mode=pl.Buffered(1)) for w in ws],
        out_specs=[pl.BlockSpec((tm, w.shape[1]), lambda i: (i, 0)) for w in ws],
        out_shape=[jax.ShapeDtypeStruct((m, w.shape[1]), dt) for w, dt in zip(ws, dtypes)],
        compiler_params=_cparams(("parallel",)),
        name="inproj",
    )(x, x, x, conv_w.astype(f32), _head_ones(), *ws)


def _t5_bucket(rel):
    nb = N_BUCKETS // 2
    max_exact = nb // 2
    n = np.abs(rel)
    large = max_exact + (np.log(np.maximum(n, 1) / max_exact) / np.log(MAX_DISTANCE / max_exact)
                         * (nb - max_exact)).astype(np.int32)
    large = np.minimum(large, nb - 1)
    return (np.where(rel > 0, nb, 0) + np.where(n < max_exact, n, large)).astype(np.int32)


ATTN_KV_COLS = 2 * KV_W + KV_W


def _bias_bucket_tables():
    j = np.arange(SPAN)[:, None]
    r = np.arange(BLOCK)[None, :]
    tabs = []
    for shift in (0, WINDOW, 2 * WINDOW):
        rel = j - shift - r
        tabs.append(np.where(np.abs(rel) <= WINDOW, _t5_bucket(rel), N_BUCKETS))
    return jnp.asarray(np.stack(tabs), dtype=jnp.int32)


def _bias_table_kernel(bucket_ref, rb_ref, o_ref):
    bucket = bucket_ref[0]
    for h in range(ATTN_Q_HEADS):
        acc = jnp.full(bucket.shape, NEG_INF, f32)
        for b in range(N_BUCKETS):
            acc = jnp.where(bucket == b, rb_ref[b, h], acc)
        o_ref[0, h] = acc


def _attn_bias_tables(rel_bias):
    return pl.pallas_call(
        _bias_table_kernel,
        grid=(3,),
        in_specs=[pl.BlockSpec((1, SPAN, BLOCK), lambda v: (v, 0, 0)),
                  pl.BlockSpec(memory_space=pltpu.SMEM)],
        out_specs=pl.BlockSpec((1, ATTN_Q_HEADS, SPAN, BLOCK), lambda v: (v, 0, 0, 0)),
        out_shape=jax.ShapeDtypeStruct((3, ATTN_Q_HEADS, SPAN, BLOCK), f32),
        compiler_params=_cparams(("parallel",)),
        name="attn_bias",
    )(_bias_bucket_tables(), rel_bias.astype(f32))


ATTN_STEP_BLOCKS = 2


def _attn_kernel(q_ref, kv_ref, bias_ref, sink_ref, o_ref, *, seq):
    nb = seq // BLOCK
    scale = HEAD_DIM ** -0.5
    pair_w = 2 * HEAD_DIM
    low = lax.broadcasted_iota(jnp.int32, (BLOCK, pair_w), 1) < HEAD_DIM
    blocks = []
    for sb in range(ATTN_STEP_BLOCKS):
        i = pl.program_id(1) * ATTN_STEP_BLOCKS + sb
        ks = pl.multiple_of(jnp.clip(i * BLOCK - WINDOW, 0, seq - SPAN), BLOCK)
        blocks.append(dict(variant=jnp.where(i == 0, 0, jnp.where(i == nb - 1, 2, 1)),
                           q=q_ref[0, BLOCK * sb:BLOCK * (sb + 1), :], kv=kv_ref[0, pl.ds(ks, SPAN), :]))
    pairs = [(blk, g) for blk in blocks for g in range(ATTN_KV_HEADS)]
    for blk in blocks:
        blk["v_t"] = blk["kv"][:, 2 * KV_W:].astype(f32).T.astype(bf16)
        blk["o_t"] = []
    s_t = []
    for blk, g in pairs:
        kk = blk["kv"][:, pair_w * g:pair_w * (g + 1)]
        qm = []
        for hl in range(ATTN_GROUP):
            h = ATTN_GROUP * g + hl
            qp = blk["q"][:, pair_w * (h // 2):pair_w * (h // 2 + 1)]
            qm.append(jnp.where(low if h % 2 == 0 else jnp.logical_not(low), qp, jnp.zeros_like(qp)))
        s_t.append(_dot_nt(kk, jnp.concatenate(qm, axis=0)))
    p_t = []
    for (blk, g), st in zip(pairs, s_t):
        p_n = []
        for hl in range(ATTN_GROUP):
            h = ATTN_GROUP * g + hl
            s = st[:, BLOCK * hl:BLOCK * (hl + 1)] * scale + bias_ref[blk["variant"], h]
            sk = sink_ref[0:1, h:h + 1]
            m = jnp.maximum(jnp.max(s, 0, keepdims=True), sk)
            p = jnp.exp(s - m)
            denom = jnp.sum(p, 0, keepdims=True) + jnp.exp(sk - m)
            p_n.append((p * (1.0 / denom)).astype(bf16))
        p_t.append(jnp.concatenate(p_n, axis=1))
    for (blk, g), pt in zip(pairs, p_t):
        og = _dot(blk["v_t"][HEAD_DIM * g:HEAD_DIM * (g + 1), :], pt)
        blk["o_t"] += [og[:, BLOCK * hl:BLOCK * (hl + 1)] for hl in range(ATTN_GROUP)]
    for sb, blk in enumerate(blocks):
        o = jnp.concatenate(blk["o_t"], axis=0).T
        o_ref[0, BLOCK * sb:BLOCK * (sb + 1), :] = o.astype(o_ref.dtype)


def _attention(aq, akv, bias_tabs, sink):
    b, s, _ = aq.shape
    tq = BLOCK * ATTN_STEP_BLOCKS
    assert s % tq == 0 and s >= SPAN
    return pl.pallas_call(
        functools.partial(_attn_kernel, seq=s),
        grid=(b, s // tq),
        in_specs=[pl.BlockSpec((1, tq, ATTN_W), lambda bi, i: (bi, i, 0)),
                  pl.BlockSpec((1, s, ATTN_KV_COLS), lambda bi, i: (bi, 0, 0)),
                  _full(bias_tabs.shape),
                  _full((1, ATTN_Q_HEADS))],
        out_specs=pl.BlockSpec((1, tq, ATTN_W), lambda bi, i: (bi, i, 0)),
        out_shape=jax.ShapeDtypeStruct((b, s, ATTN_W), bf16),
        compiler_params=_cparams(("parallel", "arbitrary")),
        name="attn",
    )(aq, akv, bias_tabs, sink.reshape(1, ATTN_Q_HEADS).astype(f32))


def _softplus(x):
    return jnp.maximum(x, 0.0) + jnp.log1p(jnp.exp(-jnp.abs(x)))


QUAD = 4
QUAD_W = QUAD * DN_DK
N_QUADS = DN_HEADS // QUAD
DELTA_BLOCK_CHUNKS = 4


def _block_diag(xq):
    lane_head = lax.broadcasted_iota(jnp.int32, xq.shape, 1) // DN_DK
    return jnp.concatenate([jnp.where(lane_head == h, xq, 0.0).astype(bf16) for h in range(QUAD)], axis=0)


def _delta_chunk(q, k, v, gx, bx, d):
    C = DN_CHUNK
    W = DN_QK_W
    sign = 1 - 2 * d
    ii = lax.broadcasted_iota(jnp.int32, (C, W), 0)
    jl = lax.broadcasted_iota(jnp.int32, (C, W), 1) % DN_DK
    order = (ii - jl) * sign
    incl = order >= 0
    strict = order > 0
    rr = lax.broadcasted_iota(jnp.int32, (2 * C, 2 * C), 0)
    cc = lax.broadcasted_iota(jnp.int32, (2 * C, 2 * C), 1)
    tri = jnp.where(((rr % C) - cc) * sign >= 0, 1.0, 0.0)
    lmat = jnp.where(cc < C, tri, jnp.where(rr < C, -1.0, 0.0)).astype(bf16)
    rmat = jnp.concatenate([gx, jnp.where(order <= 0, gx, 0.0)], axis=0)
    dg = _dot(jnp.concatenate([lmat] * 2, axis=1), jnp.concatenate(_split2(rmat), axis=0))
    dmat, gc = dg[:C], dg[C:]
    decay = jnp.where(incl, jnp.exp(jnp.where(incl, dmat, 0.0)), 0.0)
    eg = jnp.exp(gc)
    last = C - 1 if d == 0 else 0
    gl = gc[last:last + 1, :]
    kdec = k * jnp.exp(gl - gc)
    cd = jnp.exp(gl)
    kb = k * bx
    eye = jnp.where(ii == jl, 1.0, 0.0)
    return dict(q=q, k=k, bx=bx, decay=decay, strict=strict, eye=eye, kdec=kdec, cd=cd,
                vb=v * bx, kbe=kb * eg, qdec=q * eg)


DELTA_CHAINS = 2 * DELTA_BLOCK_CHUNKS * N_QUADS


def _delta_kernel(qf_ref, kf_ref, vf_ref, qb_ref, kb_ref, vb_ref, abf_ref, abb_ref, alog_ref, dtb_ref,
                  of_ref, ob_ref, state_ref, u_s, w_s, qk_s, qdec_s, kdec_s, cd_s, *, nblk):
    C, H, NC = DN_CHUNK, DN_HEADS, DELTA_BLOCK_CHUNKS
    t = pl.program_id(0)
    slot_w = t % 2
    slot_r = 1 - slot_w

    @pl.when(t == 0)
    def _():
        for ref in (u_s, w_s, qk_s, qdec_s, kdec_s, cd_s):
            ref[1] = jnp.zeros(ref.shape[1:], ref.dtype)

    @pl.when((t + nblk - 1) % nblk == 0)
    def _():
        state_ref[...] = jnp.zeros(state_ref.shape, f32)

    def expand(x, parts):
        rows = lax.broadcasted_iota(jnp.int32, (H * parts, DN_QK_W), 0) % H
        cols = lax.broadcasted_iota(jnp.int32, (H * parts, DN_QK_W), 1) // DN_DK
        emat = jnp.where(rows == cols, 1.0, 0.0).astype(bf16)
        pieces = [p.astype(f32) for p in _split3(x)[:parts]]
        return _dot(jnp.concatenate(pieces, axis=-1).astype(bf16), emat)

    chains = []

    def stage_prepare():
        dirs = ((0, qf_ref, kf_ref, vf_ref, abf_ref), (1, qb_ref, kb_ref, vb_ref, abb_ref))
        for d, q_ref, k_ref, v_ref, ab_ref in dirs:
            ab = ab_ref[0, 0]
            g = -jnp.exp(alog_ref[d]) * _softplus(ab[:, :H] + dtb_ref[d])
            gx = expand(g, 3)
            bx = expand(_sigmoid(ab[:, H:]), 2)
            for c in range(NC):
                rows = slice(C * c, C * (c + 1))
                pre = _delta_chunk(q_ref[0, rows, :], k_ref[0, rows, :], v_ref[0, rows, :], gx[rows], bx[rows], d)
                for qd in range(N_QUADS):
                    sl = slice(QUAD_W * qd, QUAD_W * (qd + 1))
                    chains.append({n: x[:, sl] for n, x in pre.items()})

    def stage_gram():
        for ch in chains:
            kq = _dot_nt(jnp.concatenate([ch["k"], ch["q"]], axis=0).astype(bf16), _block_diag(ch["k"]))
            a_mat = jnp.where(ch["strict"], kq[:C] * ch["bx"] * ch["decay"], 0.0)
            ch["qk"] = kq[C:] * ch["decay"]
            ch["p"] = -a_mat
            ch["t"] = ch["eye"] - a_mat

    def stage_square():
        for ch in chains:
            ch["p"] = _dot(ch["p"].astype(bf16), _block_diag(ch["p"]))

    def stage_double():
        for ch in chains:
            pt = _dot(jnp.concatenate([ch["p"], ch["t"]], axis=0).astype(bf16), _block_diag(ch["p"]))
            ch["p"], ch["t"] = pt[:C], ch["t"] + pt[C:]

    def stage_last():
        for ch in chains:
            ch["tb"] = (ch["t"] + _dot(ch["t"].astype(bf16), _block_diag(ch["p"]))).astype(bf16)

    def stage_solve():
        for ch in chains:
            ch["u"] = _dot(ch["tb"], _block_diag(ch["vb"]))
            ch["w"] = _dot(ch["tb"], _block_diag(ch["kbe"]))

    row_head = lax.broadcasted_iota(jnp.int32, (QUAD_W, QUAD_W), 0) // DN_DK
    col_head = lax.broadcasted_iota(jnp.int32, (QUAD_W, QUAD_W), 1) // DN_DK
    o_refs = (of_ref, ob_ref)
    live = {}

    def scan_chains(step):
        for d in range(2):
            c = step if d == 0 else NC - 1 - step
            for qd in range(N_QUADS):
                yield d, c, qd, (d * NC + c) * N_QUADS + qd

    def scan_first(step):
        for d, c, qd, n in scan_chains(step):
            st = state_ref[d, qd]
            wq = _dot(jnp.concatenate([w_s[slot_r, n], qdec_s[slot_r, n]], axis=0), st.astype(bf16))
            live[n] = (st, u_s[slot_r, n] - wq[:C], wq[C:])

    def scan_second(step):
        for d, c, qd, n in scan_chains(step):
            st, v_new, qs = live.pop(n)
            o = qs + _dot(qk_s[slot_r, n], _block_diag(v_new))
            o_refs[d][0, C * c:C * (c + 1), QUAD_W * qd:QUAD_W * (qd + 1)] = o
            s_new = _dot_tn(kdec_s[slot_r, n], v_new.astype(bf16))
            state_ref[d, qd] = st * cd_s[slot_r, n, 0:1, :] + jnp.where(row_head == col_head, s_new, 0.0)

    local = [stage_prepare, stage_gram, stage_square] + [stage_double] * 4 + [stage_last, stage_solve]
    scan = [f for step in range(NC) for f in (functools.partial(scan_first, step),
                                              functools.partial(scan_second, step))]
    for k in range(max(len(local), len(scan))):
        if k < len(local):
            local[k]()
        if k < len(scan):
            scan[k]()

    for n, ch in enumerate(chains):
        u_s[slot_w, n] = ch["u"]
        w_s[slot_w, n] = ch["w"].astype(bf16)
        qk_s[slot_w, n] = ch["qk"].astype(bf16)
        qdec_s[slot_w, n] = ch["qdec"].astype(bf16)
        kdec_s[slot_w, n] = ch["kdec"].astype(bf16)
        cd_s[slot_w, n] = jnp.broadcast_to(ch["cd"], (8, QUAD_W))


def _delta(qkvn, dab, a_log, dt_bias):
    b, s, _ = qkvn.shape
    H = DN_HEADS
    tb = DN_CHUNK * DELTA_BLOCK_CHUNKS
    nblk = s // tb
    total = b * nblk
    assert s % tb == 0
    d4 = dab.reshape(b, s, 4, H)
    ab = jnp.stack([jnp.concatenate([d4[:, :, r], d4[:, :, 2 + r]], -1) for r in range(2)])
    alog = a_log.astype(f32).reshape(2, 1, H)
    dtb = dt_bias.astype(f32).reshape(2, 1, H)

    def block(lin, rev):
        i = lin % nblk
        return lin // nblk, (nblk - 1 - i) if rev else i

    def sec(n, rev, lag):
        def index(t):
            bi, i = block(jnp.maximum(t - 1, 0) if lag else jnp.minimum(t, total - 1), rev)
            return bi, i, n
        return pl.BlockSpec((1, tb, DN_QK_W), index)

    def ab_spec(d):
        def index(t):
            bi, i = block(jnp.minimum(t, total - 1), d == 1)
            return d, bi, i, 0
        return pl.BlockSpec((1, 1, tb, 2 * H), index)

    chain_buf = lambda dt: pltpu.VMEM((2, DELTA_CHAINS, DN_CHUNK, QUAD_W), dt)
    return pl.pallas_call(
        functools.partial(_delta_kernel, nblk=nblk),
        grid=(total + 1,),
        in_specs=[sec(0, False, False), sec(1, False, False), sec(2, False, False),
                  sec(0, True, False), sec(1, True, False), sec(2, True, False),
                  ab_spec(0), ab_spec(1), _full((2, 1, H)), _full((2, 1, H))],
        out_specs=[sec(0, False, True), sec(0, True, True)],
        out_shape=[jax.ShapeDtypeStruct((b, s, DN_V_W), f32)] * 2,
        scratch_shapes=[pltpu.VMEM((2, N_QUADS, QUAD_W, QUAD_W), f32),
                        chain_buf(f32), chain_buf(bf16), chain_buf(bf16), chain_buf(bf16), chain_buf(bf16),
                        pltpu.VMEM((2, DELTA_CHAINS, 8, QUAD_W), f32)],
        compiler_params=_cparams(("arbitrary",)),
        name="delta",
    )(qkvn, qkvn, qkvn, qkvn, qkvn, qkvn, ab, ab, alog, dtb)


def _pool_band_tables():
    r = np.arange(POOL_TILE)[:, None]
    j = np.arange(POOL_SPAN)[None, :]
    tabs = np.zeros((3, N_POOL_GROUPS, POOL_TILE, POOL_SPAN), np.float32)
    for vi, shift in enumerate((0, 128, 256)):
        rel = j - shift - r
        for gi, w in enumerate(POOL_WINDOWS):
            tabs[vi, gi] = np.abs(rel) <= w // 2
    return jnp.asarray(tabs, dtype=bf16)


def _pool_tile(u_ref, band_ref, w_ref, scale_ref, t, seq):
    nt = seq // POOL_TILE
    t0 = pl.multiple_of(t * POOL_TILE, POOL_TILE)
    ks = pl.multiple_of(jnp.clip(t0 - 128, 0, seq - POOL_SPAN), 128)
    variant = jnp.where(t == 0, 0, jnp.where(t == nt - 1, 2, 1))
    win = u_ref[0, pl.ds(ks, POOL_SPAN), :]
    own = u_ref[0, pl.ds(t0, POOL_TILE), :].astype(f32)
    pos = t0 + lax.broadcasted_iota(jnp.int32, (POOL_TILE, POOL_GROUP), 0)
    outs = []
    for gi, w in enumerate(POOL_WINDOWS):
        r = w // 2
        sl = slice(POOL_GROUP * gi, POOL_GROUP * (gi + 1))
        wsum = _dot(band_ref[variant, gi], win[:, sl])
        count = (jnp.minimum(pos + r, seq - 1) - jnp.maximum(pos - r, 0) + 1).astype(f32)
        mixed = wsum / count - own[:, sl]
        outs.append(_dot(mixed.astype(bf16), w_ref[gi]))
    return jnp.concatenate(outs, -1) * scale_ref[...]


def _route_tile(x, w, carry_ref):
    xh, xl = _split2(x)
    wh, wl = _split2(w)
    logits = _dot(xh, wh) + (_dot(xl, wh) + _dot(xh, wl))
    tm = logits.shape[0]
    lane = lax.broadcasted_iota(jnp.int32, logits.shape, 1)
    m1 = jnp.max(logits, -1, keepdims=True)
    i1 = jnp.min(jnp.where(logits == m1, lane, N_EXPERTS), -1, keepdims=True)
    rest = jnp.where(lane == i1, -jnp.inf, logits)
    m2 = jnp.max(rest, -1, keepdims=True)
    i2 = jnp.min(jnp.where(rest == m2, lane, N_EXPERTS), -1, keepdims=True)
    e2 = jnp.exp(m2 - m1)
    p1 = 1.0 / (1.0 + e2)
    p2 = e2 / (1.0 + e2)
    oh1 = jnp.where(lane == i1, 1.0, 0.0)
    oh2 = jnp.where(lane == i2, 1.0, 0.0)
    both = oh1 + oh2
    rr = lax.broadcasted_iota(jnp.int32, (tm, tm), 0)
    cc = lax.broadcasted_iota(jnp.int32, (tm, tm), 1)
    earlier = jnp.where(cc < rr, 1.0, 0.0).astype(bf16)
    before = _dot(earlier, both.astype(bf16)) + carry_ref[...]
    r1 = jnp.sum(oh1 * before, -1, keepdims=True).astype(jnp.int32)
    r2 = jnp.sum(oh2 * before, -1, keepdims=True).astype(jnp.int32)
    carry_ref[...] += jnp.sum(both, 0, keepdims=True)
    mi = jnp.where(lane == 0, i1, jnp.where(lane == 1, i2, jnp.where(lane == 2, r1, jnp.where(lane == 3, r2, 0))))
    mf = jnp.where(lane == 0, p1, jnp.where(lane == 1, p2, 0.0))
    return mi, mf


def _merge_kernel(oa_ref, odf_ref, odb_ref, z_ref, pu_ref, gate_ref, h_ref, ones_ref, nw_ref, band_ref, pw_ref,
                  ps_ref, wb_ref, wo_ref, g_ref, b_ref, *refs, seq, routed):
    if routed:
        rw_ref, hf_ref, hb_ref, mi_ref, mf_ref, cnt_ref, carry_ref = refs

        @pl.when(pl.program_id(0) == 0)
        def _():
            carry_ref[...] = jnp.zeros(carry_ref.shape, f32)
    else:
        hf_ref, hb_ref = refs
    od = odf_ref[...] + odb_ref[...]
    ms = _dot_split2_rhs(od * od, ones_ref[...]) * (1.0 / DN_DV)
    od = od * lax.rsqrt(ms + RMS_EPS) * nw_ref[...] * _silu(z_ref[...].astype(f32))
    op = _pool_tile(pu_ref, band_ref, pw_ref, ps_ref, pl.program_id(0) % (seq // POOL_TILE), seq)
    branches = (oa_ref[...], od.astype(bf16), op.astype(bf16))
    merged = None
    for n in range(N_BRANCH):
        up = _dot(branches[n], wb_ref[n])
        gate = _sigmoid(gate_ref[:, D_MODEL * n:D_MODEL * (n + 1)].astype(f32))
        merged = gate * up if merged is None else merged + gate * up
    y = _dot(merged.astype(bf16), wo_ref[...])
    hn = _layer_norm(DEEPNORM_ALPHA * h_ref[...] + y, g_ref[...], b_ref[...])
    hf_ref[...] = hn
    if routed:
        hb_ref[...] = _pack_bf16_pair(hn[:, :D_MODEL // 2], hn[:, D_MODEL // 2:])
        mi_ref[...], mf_ref[...] = _route_tile(hn, rw_ref[...], carry_ref)
        cnt_ref[...] = carry_ref[...].astype(jnp.int32)
    else:
        hb_ref[...] = hn.astype(bf16)


def _merge(oa, odf, odb, z, pu, gates, h, dn_norm_w, pool_w, pool_scale, w_branch, w_out, ln_g, ln_b, router_w):
    m = h.shape[0]
    bsz, seq, _ = pu.shape
    tm = POOL_TILE
    nt = seq // tm
    assert seq % tm == 0 and seq >= POOL_SPAN
    routed = router_w is not None
    row = lambda w: pl.BlockSpec((tm, w), lambda i: (i, 0))
    nw = jnp.tile(dn_norm_w.astype(f32), DN_HEADS).reshape(1, DN_V_W)
    in_specs = [row(ATTN_W), row(DN_V_W), row(DN_V_W), row(DN_V_W),
                pl.BlockSpec((1, seq, POOL_W), lambda i: (i // nt, 0, 0)),
                row(N_BRANCH * D_MODEL), row(D_MODEL),
                _full((DN_V_W, DN_V_W)), _full((1, DN_V_W)),
                _full((3, N_POOL_GROUPS, POOL_TILE, POOL_SPAN)), _full((N_POOL_GROUPS, POOL_GROUP, POOL_GROUP)),
                _full((1, POOL_W)),
                _full((N_BRANCH, ATTN_W, D_MODEL)), _full((D_MODEL, D_MODEL)),
                _full((1, D_MODEL)), _full((1, D_MODEL))]
    args = [oa, odf, odb, z, pu, gates, h, _head_ones(), nw, _pool_band_tables(), pool_w.astype(bf16),
            pool_scale.reshape(1, POOL_W).astype(f32), w_branch.astype(bf16), w_out.astype(bf16),
            ln_g.reshape(1, D_MODEL).astype(f32), ln_b.reshape(1, D_MODEL).astype(f32)]
    out_specs = [row(D_MODEL)]
    out_shape = [jax.ShapeDtypeStruct((m, D_MODEL), f32)]
    scratch = []
    if routed:
        in_specs.append(_full((D_MODEL, N_EXPERTS)))
        args.append(router_w.astype(f32))
        out_specs += [row(PACKED_W), row(N_EXPERTS), row(N_EXPERTS), _full((1, N_EXPERTS))]
        out_shape += [jax.ShapeDtypeStruct((m, PACKED_W), jnp.uint32),
                      jax.ShapeDtypeStruct((m, N_EXPERTS), jnp.int32),
                      jax.ShapeDtypeStruct((m, N_EXPERTS), f32),
                      jax.ShapeDtypeStruct((1, N_EXPERTS), jnp.int32)]
        scratch.append(pltpu.VMEM((1, N_EXPERTS), f32))
    else:
        out_specs.append(row(D_MODEL))
        out_shape.append(jax.ShapeDtypeStruct((m, D_MODEL), bf16))
    return pl.pallas_call(
        functools.partial(_merge_kernel, seq=seq, routed=routed),
        grid=(m // tm,),
        in_specs=in_specs,
        out_specs=out_specs,
        out_shape=out_shape,
        scratch_shapes=scratch,
        compiler_params=_cparams(("arbitrary",)),
        name="merge",
    )(*args)


def _swiglu_step(x, wg_ref, wu_ref, wd_ref, acc_ref):
    g = _dot(x, wg_ref[0].astype(bf16))
    u = _dot(x, wu_ref[0].astype(bf16))
    acc_ref[...] += _dot((_silu(g) * u).astype(bf16), wd_ref[0].astype(bf16))


def _ffn_kernel(x_ref, h_ref, wg_ref, wu_ref, wd_ref, g_ref, b_ref, hf_ref, hb_ref, acc_ref):
    j = pl.program_id(1)

    @pl.when(j == 0)
    def _():
        acc_ref[...] = jnp.zeros(acc_ref.shape, f32)

    _swiglu_step(x_ref[...], wg_ref, wu_ref, wd_ref, acc_ref)

    @pl.when(j == pl.num_programs(1) - 1)
    def _():
        hn = _layer_norm(DEEPNORM_ALPHA * h_ref[...] + acc_ref[...], g_ref[...], b_ref[...])
        hf_ref[...] = hn
        hb_ref[...] = hn.astype(bf16)


def _ffn(hb, hf, w1, w3, w2, ln_g, ln_b, tm=1024, tf=256):
    m = hb.shape[0]
    tm = min(tm, m)
    ff = w1.shape[-1]
    row = pl.BlockSpec((tm, D_MODEL), lambda i, j: (i, 0))
    return pl.pallas_call(
        _ffn_kernel,
        grid=(m // tm, ff // tf),
        in_specs=[row, row,
                  pl.BlockSpec((1, D_MODEL, tf), lambda i, j: (0, 0, j)),
                  pl.BlockSpec((1, D_MODEL, tf), lambda i, j: (0, 0, j)),
                  pl.BlockSpec((1, tf, D_MODEL), lambda i, j: (0, j, 0)),
                  pl.BlockSpec((1, D_MODEL), lambda i, j: (0, 0)),
                  pl.BlockSpec((1, D_MODEL), lambda i, j: (0, 0))],
        out_specs=[row, row],
        out_shape=[jax.ShapeDtypeStruct((m, D_MODEL), f32), jax.ShapeDtypeStruct((m, D_MODEL), bf16)],
        scratch_shapes=[pltpu.VMEM((tm, D_MODEL), f32)],
        compiler_params=_cparams(("parallel", "arbitrary")),
        name="ffn",
    )(hb, hf, w1[None], w3[None], w2[None],
      ln_g.reshape(1, D_MODEL).astype(f32), ln_b.reshape(1, D_MODEL).astype(f32))


MOE_TILE = 1024
PACKED_W = D_MODEL // 2
SC_CORES = 2
SC_SUBCORES = 16
SC_WORKERS = SC_CORES * SC_SUBCORES
SC_ROWS = 64


def _gmm_kernel(te_ref, tv_ref, x_ref, wg_ref, wu_ref, wd_ref, o_ref, acc_ref):
    i = pl.program_id(0)
    j = pl.program_id(1)
    valid = tv_ref[i]

    @pl.when(j == 0)
    def _():
        acc_ref[...] = jnp.zeros(acc_ref.shape, f32)

    @pl.when(valid > 0)
    def _():
        rows = lax.broadcasted_iota(jnp.int32, acc_ref.shape, 0)
        x = jnp.where(rows < valid, _unpack_bf16_pair(x_ref[...]), 0.0).astype(bf16)
        _swiglu_step(x, wg_ref, wu_ref, wd_ref, acc_ref)

    @pl.when(j == pl.num_programs(1) - 1)
    def _():
        o_ref[...] = _pack_bf16_pair(acc_ref[:, :PACKED_W], acc_ref[:, PACKED_W:])


def _gmm(xs, tile_expert, tile_valid, wg, wu, wd, tm, tf=512):
    n = xs.shape[0]
    ff = wg.shape[-1]
    nj = ff // tf

    def fcol(i, j, te, tv):
        return jnp.where(tv[i] > 0, j, nj - 1)

    grid_spec = pltpu.PrefetchScalarGridSpec(
        num_scalar_prefetch=2,
        grid=(n // tm, nj),
        in_specs=[pl.BlockSpec((tm, PACKED_W), lambda i, j, te, tv: (i, 0)),
                  pl.BlockSpec((1, D_MODEL, tf), lambda i, j, te, tv: (te[i], 0, fcol(i, j, te, tv))),
                  pl.BlockSpec((1, D_MODEL, tf), lambda i, j, te, tv: (te[i], 0, fcol(i, j, te, tv))),
                  pl.BlockSpec((1, tf, D_MODEL), lambda i, j, te, tv: (te[i], fcol(i, j, te, tv), 0))],
        out_specs=pl.BlockSpec((tm, PACKED_W), lambda i, j, te, tv: (i, 0)),
        scratch_shapes=[pltpu.VMEM((tm, D_MODEL), f32)])
    return pl.pallas_call(
        _gmm_kernel,
        grid_spec=grid_spec,
        out_shape=jax.ShapeDtypeStruct((n, PACKED_W), jnp.uint32),
        compiler_params=_cparams(("parallel", "arbitrary")),
        name="moe_gmm",
    )(tile_expert, tile_valid, xs, wg, wu, wd)


def _sc_mesh():
    return plsc.VectorSubcoreMesh(core_axis_name="c", subcore_axis_name="s",
                                  num_cores=SC_CORES, num_subcores=SC_SUBCORES)


def _sc_worker_base(per_worker):
    return (lax.axis_index("s") * SC_CORES + lax.axis_index("c")) * per_worker


def _sc_scratch(d, dtype):
    return [pltpu.VMEM((SC_ROWS,), jnp.int32), pltpu.VMEM((SC_ROWS,), jnp.int32), pltpu.VMEM((2, SC_ROWS, d), dtype),
            pltpu.SemaphoreType.DMA((2,)), pltpu.SemaphoreType.DMA((2,))]


def _sc_scatter_rows(src, idx0, idx1, n_out):
    m, d = src.shape
    per_worker = m // SC_WORKERS
    n_chunks = per_worker // SC_ROWS
    assert m % (SC_WORKERS * SC_ROWS * 2) == 0

    def body(src_hbm, idx0_hbm, idx1_hbm, out_hbm, idx_a, idx_b, rows_v, sem_in, sem_out):
        base = _sc_worker_base(per_worker)
        idx_v = (idx_a, idx_b)

        def load(j, slot):
            off = pl.multiple_of(base + j * SC_ROWS, SC_ROWS)
            return pltpu.make_async_copy(src_hbm.at[pl.ds(off, SC_ROWS)], rows_v.at[slot], sem_in.at[slot])

        def store(j, slot, idx_hbm):
            off = pl.multiple_of(base + j * SC_ROWS, SC_ROWS)
            pltpu.sync_copy(idx_hbm.at[pl.ds(off, SC_ROWS)], idx_v[slot])
            pltpu.async_copy(rows_v.at[slot], out_hbm.at[idx_v[slot]], sem_out.at[slot]).wait()

        load(0, 0).start()

        @pl.loop(0, n_chunks, step=2)
        def _(j):
            for slot in range(2):
                @pl.when(j + slot + 1 < n_chunks)
                def _():
                    load(j + slot + 1, 1 - slot).start()
                load(j + slot, slot).wait()
                store(j + slot, slot, idx0_hbm)
                store(j + slot, slot, idx1_hbm)

    return pl.kernel(
        body, out_type=jax.ShapeDtypeStruct((n_out, d), src.dtype), mesh=_sc_mesh(),
        scratch_types=_sc_scratch(d, src.dtype), name="moe_dispatch",
    )(src, idx0, idx1)


def _sc_gather_rows(table, idx):
    n = idx.shape[0]
    d = table.shape[1]
    per_worker = n // SC_WORKERS
    n_chunks = per_worker // SC_ROWS
    assert n % (SC_WORKERS * SC_ROWS * 2) == 0

    def body(table_hbm, idx_hbm, out_hbm, idx_a, idx_b, rows_v, sem_in, sem_out):
        base = _sc_worker_base(per_worker)
        idx_v = (idx_a, idx_b)

        def gather(j, slot):
            off = pl.multiple_of(base + j * SC_ROWS, SC_ROWS)
            pltpu.sync_copy(idx_hbm.at[pl.ds(off, SC_ROWS)], idx_v[slot])
            return pltpu.make_async_copy(table_hbm.at[idx_v[slot]], rows_v.at[slot], sem_in.at[slot])

        def wait_gather(slot):
            pltpu.make_async_copy(table_hbm.at[idx_v[slot]], rows_v.at[slot], sem_in.at[slot]).wait()

        def write(j, slot):
            off = pl.multiple_of(base + j * SC_ROWS, SC_ROWS)
            return pltpu.make_async_copy(rows_v.at[slot], out_hbm.at[pl.ds(off, SC_ROWS)], sem_out.at[slot])

        gather(0, 0).start()

        @pl.loop(0, n_chunks, step=2)
        def _(j):
            for slot in range(2):
                @pl.when(j + slot + 1 < n_chunks)
                def _():
                    @pl.when(j + slot >= 1)
                    def _():
                        write(j + slot - 1, 1 - slot).wait()
                    gather(j + slot + 1, 1 - slot).start()
                wait_gather(slot)
                write(j + slot, slot).start()

        write(n_chunks - 2, 0).wait()
        write(n_chunks - 1, 1).wait()

    return pl.kernel(
        body, out_type=jax.ShapeDtypeStruct((n, d), table.dtype), mesh=_sc_mesh(),
        scratch_types=_sc_scratch(d, table.dtype), name="moe_collect",
    )(table, idx)


def _combine_kernel(y0_ref, y1_ref, mf_ref, h_ref, g_ref, b_ref, o_ref):
    p = mf_ref[...]
    y = p[:, 0:1] * _unpack_bf16_pair(y0_ref[0]) + p[:, 1:2] * _unpack_bf16_pair(y1_ref[0])
    o_ref[...] = _layer_norm(DEEPNORM_ALPHA * h_ref[...] + y, g_ref[...], b_ref[...])


def _combine(yg, mf, hf, ln_g, ln_b, tm=512):
    m = hf.shape[0]
    tm = min(tm, m)
    return pl.pallas_call(
        _combine_kernel,
        grid=(m // tm,),
        in_specs=[pl.BlockSpec((1, tm, PACKED_W), lambda i: (0, i, 0)),
                  pl.BlockSpec((1, tm, PACKED_W), lambda i: (1, i, 0)),
                  pl.BlockSpec((tm, N_EXPERTS), lambda i: (i, 0)),
                  pl.BlockSpec((tm, D_MODEL), lambda i: (i, 0)),
                  _full((1, D_MODEL)), _full((1, D_MODEL))],
        out_specs=pl.BlockSpec((tm, D_MODEL), lambda i: (i, 0)),
        out_shape=jax.ShapeDtypeStruct((m, D_MODEL), f32),
        compiler_params=_cparams(("parallel",)),
        name="moe_combine",
    )(yg, yg, mf, hf, ln_g.reshape(1, D_MODEL).astype(f32), ln_b.reshape(1, D_MODEL).astype(f32))


def _moe(hf, hp, mi, mf, cnt, wg, wu, wd, ln_g, ln_b):
    m = hf.shape[0]
    tm = MOE_TILE
    counts = cnt[0]
    padded = (counts + tm - 1) // tm * tm
    ends = jnp.cumsum(padded)
    starts = ends - padded
    experts = jnp.arange(N_EXPERTS, dtype=jnp.int32)
    pos = jnp.sum(jnp.where(mi[:, 0:TOP_K, None] == experts, starts, 0), -1) + mi[:, TOP_K:2 * TOP_K]
    n_tiles = TOP_K * m // tm + N_EXPERTS
    tile_start = jnp.arange(n_tiles, dtype=jnp.int32) * tm
    tile_expert = jnp.minimum(jnp.sum(tile_start[:, None] >= ends[None, :], -1), N_EXPERTS - 1).astype(jnp.int32)
    tile_valid = jnp.clip(starts[tile_expert] + counts[tile_expert] - tile_start, 0, tm).astype(jnp.int32)
    tile_valid = jnp.where(tile_start < ends[-1], tile_valid, 0)
    xs = _sc_scatter_rows(hp, pos[:, 0], pos[:, 1], n_tiles * tm)
    ys = _gmm(xs, tile_expert, tile_valid, wg, wu, wd, tm)
    yg = _sc_gather_rows(ys, jnp.concatenate([pos[:, 0], pos[:, 1]]))
    return _combine(yg.reshape(TOP_K, m, PACKED_W), mf, hf, ln_g, ln_b)


_SPLITS = (ATTN_W, KV_W, KV_W, 3 * DN_QK_W, DN_V_W, 4 * DN_HEADS, POOL_W, N_BRANCH * D_MODEL)
_SPLIT_DTYPES = (bf16, bf16, f32, bf16, f32, bf16, bf16)


def _mixer(hf, hb, bsz, seq, bias_tabs, w_in, sink, conv_w, a_log, dt_bias, dn_norm_w, pool_w, pool_scale,
           w_branch, w_out, ln_g, ln_b, router_w):
    points = np.cumsum(_SPLITS)[:-1].tolist()
    wq, wk, wv, *rest = jnp.split(w_in.astype(bf16), points, axis=-1)
    wkv = jnp.concatenate([wk[:, HEAD_DIM * g:HEAD_DIM * (g + 1)] for g in range(ATTN_KV_HEADS) for _ in range(2)]
                          + [wv], axis=-1)
    aq, akv, qkvn, dz, dab, pu, gates = _inproj(hf if hb is None else hb, [wq, wkv] + rest, _SPLIT_DTYPES,
                                                conv_w, seq, dn=2)
    shp = lambda t: t.reshape(bsz, seq, t.shape[-1])
    oa = _attention(shp(aq), shp(akv), bias_tabs, sink)
    odf, odb = _delta(shp(qkvn), shp(dab), a_log, dt_bias)
    m = bsz * seq
    return _merge(oa.reshape(m, ATTN_W), odf.reshape(m, DN_V_W), odb.reshape(m, DN_V_W), dz, shp(pu), gates, hf,
                  dn_norm_w, pool_w, pool_scale, w_branch, w_out, ln_g, ln_b, router_w)


def kernel(x, w_in, attn_sink, rel_bias, conv_w, a_log, dt_bias, dn_norm_w, pool_w, pool_scale, w_branch, w_out,
           ln1_g, ln1_b, ln2_g, ln2_b, ffn_w1, ffn_w3, ffn_w2, router_w, moe_wg, moe_wu, moe_wd):
    bsz, seq, _ = x.shape
    m = bsz * seq
    bias_tabs = _attn_bias_tables(rel_bias)
    hf = x.reshape(m, D_MODEL).astype(f32)
    hb = None
    for layer in range(DEPTH):
        dense = layer % 2 == 0
        i = layer // 2
        hf, *rest = _mixer(hf, hb, bsz, seq, bias_tabs, w_in[layer], attn_sink[layer], conv_w[layer], a_log[layer],
                           dt_bias[layer], dn_norm_w[layer], pool_w[layer], pool_scale[layer], w_branch[layer],
                           w_out[layer], ln1_g[layer], ln1_b[layer], None if dense else router_w[i])
        if dense:
            hf, hb = _ffn(rest[0], hf, ffn_w1[i], ffn_w3[i], ffn_w2[i], ln2_g[layer], ln2_b[layer])
        else:
            hf = _moe(hf, *rest, moe_wg[i], moe_wu[i], moe_wd[i], ln2_g[layer], ln2_b[layer])
            hb = None
    return hf.reshape(bsz, seq, D_MODEL).astype(x.dtype)
```

```python
import functools

import numpy as np
import jax
import jax.numpy as jnp
from jax import lax
from jax.experimental import pallas as pl
from jax.experimental.pallas import tpu as pltpu
from jax.experimental.pallas import tpu_sc as plsc

f32 = jnp.float32
bf16 = jnp.bfloat16

D_MODEL = 1024
DEPTH = 2
HEAD_DIM = 64
ATTN_Q_HEADS = 8
ATTN_KV_HEADS = 2
ATTN_GROUP = ATTN_Q_HEADS // ATTN_KV_HEADS
WINDOW = 128
BLOCK = 128
SPAN = BLOCK + 2 * WINDOW
N_BUCKETS = 32
MAX_DISTANCE = 128
DN_HEADS = 8
DN_DK = 64
DN_DV = 64
DN_CONV = 5
DN_CHUNK = 64
POOL_WINDOWS = (2, 4, 8, 16)
N_POOL_GROUPS = 4
POOL_GROUP = 128
ATTN_W = ATTN_Q_HEADS * HEAD_DIM
KV_W = ATTN_KV_HEADS * HEAD_DIM
DN_QK_W = DN_HEADS * DN_DK
DN_V_W = DN_HEADS * DN_DV
POOL_W = N_POOL_GROUPS * POOL_GROUP
N_BRANCH = 3
N_EXPERTS = 8
TOP_K = 2
DEEPNORM_ALPHA = (2.0 * DEPTH) ** 0.25
LN_EPS = 1e-5
RMS_EPS = 1e-6
NEG_INF = -1e30

VMEM_LIMIT_BYTES = 56 * 1024 * 1024
LANES = 128
POOL_TILE = 256
POOL_SPAN = 512


def _cparams(sem):
    return pltpu.CompilerParams(dimension_semantics=sem, vmem_limit_bytes=VMEM_LIMIT_BYTES)


def _full(shape):
    n = len(shape)
    return pl.BlockSpec(shape, lambda *_: (0,) * n)


def _sigmoid(x):
    return 0.5 * jnp.tanh(0.5 * x) + 0.5


def _silu(x):
    return x * _sigmoid(x)


def _layer_norm(x, g, b):
    mu = jnp.mean(x, -1, keepdims=True)
    xc = x - mu
    var = jnp.mean(xc * xc, -1, keepdims=True)
    return xc * lax.rsqrt(var + LN_EPS) * g + b


def _dot(a, b):
    return jnp.dot(a, b, preferred_element_type=f32)


def _dot_nt(a, b):
    return lax.dot_general(a, b, (((1,), (1,)), ((), ())), preferred_element_type=f32)


def _dot_tn(a, b):
    return lax.dot_general(a, b, (((0,), (0,)), ((), ())), preferred_element_type=f32)


def _split3(x):
    hi = x.astype(bf16)
    r = x - hi.astype(f32)
    mid = r.astype(bf16)
    lo = (r - mid.astype(f32)).astype(bf16)
    return hi, mid, lo


def _split2(x):
    hi = x.astype(bf16)
    return hi, (x - hi.astype(f32)).astype(bf16)


def _dot_split2_rhs(x, b_bf):
    hi, lo = _split2(x)
    return _dot(hi, b_bf) + _dot(lo, b_bf)


def _pack_bf16_pair(a, b):
    def rounded(x):
        bits = lax.bitcast_convert_type(x, jnp.uint32)
        return bits + jnp.uint32(0x7FFF) + ((bits >> 16) & jnp.uint32(1))
    return (rounded(b) & jnp.uint32(0xFFFF0000)) | (rounded(a) >> 16)


def _unpack_bf16_pair(p):
    lo = lax.bitcast_convert_type(p << 16, f32)
    hi = lax.bitcast_convert_type(p & jnp.uint32(0xFFFF0000), f32)
    return jnp.concatenate([lo, hi], axis=-1)


INPROJ_TILE = 256
DN_HALO = 16


def _head_ones():
    blk = np.arange(DN_QK_W) // DN_DK
    return jnp.asarray(blk[:, None] == blk[None, :], dtype=bf16)


def _inproj_kernel(x_ref, xp_ref, xn_ref, cw_ref, ones_ref, *refs, nt, dn):
    n = len(refs) // 2
    w_refs, o_refs = refs[:n], refs[n:]
    x = x_ref[...].astype(bf16)
    t = pl.program_id(0) % nt
    x_ext = jnp.concatenate([xp_ref[...].astype(bf16), x, xn_ref[...].astype(bf16)], axis=0)
    rows = x_ext.shape[0]
    row = lax.broadcasted_iota(jnp.int32, (rows, DN_QK_W), 0)
    first = jnp.where(t == 0, DN_HALO, 0)
    end = jnp.where(t == nt - 1, rows - DN_HALO, rows)
    inside = (row >= first) & (row < end)
    u = _dot(x_ext, w_refs[dn][...])
    jobs = [(idx, c0) for idx in range(n) if idx != dn for c0 in range(0, o_refs[idx].shape[1], D_MODEL)]
    jobs.sort(key=lambda job: -min(D_MODEL, o_refs[job[0]].shape[1] - job[1]))
    for sec in range(3):
        for idx, c0 in jobs[sec::3]:
            c1 = min(c0 + D_MODEL, o_refs[idx].shape[1])
            o_refs[idx][:, c0:c1] = _dot(x, w_refs[idx][:, c0:c1]).astype(o_refs[idx].dtype)
        cols = slice(DN_QK_W * sec, DN_QK_W * (sec + 1))
        us = jnp.where(inside, u[:, cols], 0.0)
        z = [us * cw_ref[k:k + 1, cols] for k in range(DN_CONV)]
        after = pltpu.roll(z[3] + pltpu.roll(z[4], rows - 1, axis=0), rows - 1, axis=0)
        before = pltpu.roll(z[1] + pltpu.roll(z[0], 1, axis=0), 1, axis=0)
        y = _silu((z[2] + after + before)[DN_HALO:rows - DN_HALO])
        if sec < 2:
            ss = _dot((y * y).astype(bf16), ones_ref[...])
            y = y * lax.rsqrt(ss + 1e-6) * (DN_DK ** -0.5 if sec == 0 else 1.0)
        o_refs[dn][:, cols] = y


def _inproj(x, ws, dtypes, conv_w, seq, dn):
    m = x.shape[0]
    tm = INPROJ_TILE
    halo_blocks = tm // DN_HALO
    assert seq % tm == 0 and tm % DN_HALO == 0 and DN_CONV == 5
    last = m // DN_HALO - 1
    return pl.pallas_call(
        functools.partial(_inproj_kernel, nt=seq // tm, dn=dn),
        grid=(m // tm,),
        in_specs=[pl.BlockSpec((tm, D_MODEL), lambda i: (i, 0)),
                  pl.BlockSpec((DN_HALO, D_MODEL), lambda i: (jnp.maximum(i * halo_blocks - 1, 0), 0)),
                  pl.BlockSpec((DN_HALO, D_MODEL), lambda i: (jnp.minimum((i + 1) * halo_blocks, last), 0)),
                  _full(conv_w.shape), _full((DN_QK_W, DN_QK_W))]
        + [pl.BlockSpec(w.shape, lambda i: (0, 0), pipeline_mode=pl.Buffered(1)) for w in ws],
        out_specs=[pl.BlockSpec((tm, w.shape[1]), lambda i: (i, 0)) for w in ws],
        out_shape=[jax.ShapeDtypeStruct((m, w.shape[1]), dt) for w, dt in zip(ws, dtypes)],
        compiler_params=_cparams(("parallel",)),
        name="inproj",
    )(x, x, x, conv_w.astype(f32), _head_ones(), *ws)


def _t5_bucket(rel):
    nb = N_BUCKETS // 2
    max_exact = nb // 2
    n = np.abs(rel)
    large = max_exact + (np.log(np.maximum(n, 1) / max_exact) / np.log(MAX_DISTANCE / max_exact)
                         * (nb - max_exact)).astype(np.int32)
    large = np.minimum(large, nb - 1)
    return (np.where(rel > 0, nb, 0) + np.where(n < max_exact, n, large)).astype(np.int32)


ATTN_KV_COLS = 2 * KV_W + KV_W


def _bias_bucket_tables():
    j = np.arange(SPAN)[:, None]
    r = np.arange(BLOCK)[None, :]
    tabs = []
    for shift in (0, WINDOW, 2 * WINDOW):
        rel = j - shift - r
        tabs.append(np.where(np.abs(rel) <= WINDOW, _t5_bucket(rel), N_BUCKETS))
    return jnp.asarray(np.stack(tabs), dtype=jnp.int32)


def _bias_table_kernel(bucket_ref, rb_ref, o_ref):
    bucket = bucket_ref[0]
    for h in range(ATTN_Q_HEADS):
        acc = jnp.full(bucket.shape, NEG_INF, f32)
        for b in range(N_BUCKETS):
            acc = jnp.where(bucket == b, rb_ref[b, h], acc)
        o_ref[0, h] = acc


def _attn_bias_tables(rel_bias):
    return pl.pallas_call(
        _bias_table_kernel,
        grid=(3,),
        in_specs=[pl.BlockSpec((1, SPAN, BLOCK), lambda v: (v, 0, 0)),
                  pl.BlockSpec(memory_space=pltpu.SMEM)],
        out_specs=pl.BlockSpec((1, ATTN_Q_HEADS, SPAN, BLOCK), lambda v: (v, 0, 0, 0)),
        out_shape=jax.ShapeDtypeStruct((3, ATTN_Q_HEADS, SPAN, BLOCK), f32),
        compiler_params=_cparams(("parallel",)),
        name="attn_bias",
    )(_bias_bucket_tables(), rel_bias.astype(f32))


ATTN_STEP_BLOCKS = 2


def _attn_kernel(q_ref, kv_ref, bias_ref, sink_ref, o_ref, *, seq):
    nb = seq // BLOCK
    scale = HEAD_DIM ** -0.5
    pair_w = 2 * HEAD_DIM
    low = lax.broadcasted_iota(jnp.int32, (BLOCK, pair_w), 1) < HEAD_DIM
    blocks = []
    for sb in range(ATTN_STEP_BLOCKS):
        i = pl.program_id(1) * ATTN_STEP_BLOCKS + sb
        ks = pl.multiple_of(jnp.clip(i * BLOCK - WINDOW, 0, seq - SPAN), BLOCK)
        blocks.append(dict(variant=jnp.where(i == 0, 0, jnp.where(i == nb - 1, 2, 1)),
                           q=q_ref[0, BLOCK * sb:BLOCK * (sb + 1), :], kv=kv_ref[0, pl.ds(ks, SPAN), :]))
    pairs = [(blk, g) for blk in blocks for g in range(ATTN_KV_HEADS)]
    for blk in blocks:
        blk["v_t"] = blk["kv"][:, 2 * KV_W:].astype(f32).T.astype(bf16)
        blk["o_t"] = []
    s_t = []
    for blk, g in pairs:
        kk = blk["kv"][:, pair_w * g:pair_w * (g + 1)]
        qm = []
        for hl in range(ATTN_GROUP):
            h = ATTN_GROUP * g + hl
            qp = blk["q"][:, pair_w * (h // 2):pair_w * (h // 2 + 1)]
            qm.append(jnp.where(low if h % 2 == 0 else jnp.logical_not(low), qp, jnp.zeros_like(qp)))
        s_t.append(_dot_nt(kk, jnp.concatenate(qm, axis=0)))
    p_t = []
    for (blk, g), st in zip(pairs, s_t):
        p_n = []
        for hl in range(ATTN_GROUP):
            h = ATTN_GROUP * g + hl
            s = st[:, BLOCK * hl:BLOCK * (hl + 1)] * scale + bias_ref[blk["variant"], h]
            sk = sink_ref[0:1, h:h + 1]
            m = jnp.maximum(jnp.max(s, 0, keepdims=True), sk)
            p = jnp.exp(s - m)
            denom = jnp.sum(p, 0, keepdims=True) + jnp.exp(sk - m)
            p_n.append((p * (1.0 / denom)).astype(bf16))
        p_t.append(jnp.concatenate(p_n, axis=1))
    for (blk, g), pt in zip(pairs, p_t):
        og = _dot(blk["v_t"][HEAD_DIM * g:HEAD_DIM * (g + 1), :], pt)
        blk["o_t"] += [og[:, BLOCK * hl:BLOCK * (hl + 1)] for hl in range(ATTN_GROUP)]
    for sb, blk in enumerate(blocks):
        o = jnp.concatenate(blk["o_t"], axis=0).T
        o_ref[0, BLOCK * sb:BLOCK * (sb + 1), :] = o.astype(o_ref.dtype)


def _attention(aq, akv, bias_tabs, sink):
    b, s, _ = aq.shape
    tq = BLOCK * ATTN_STEP_BLOCKS
    assert s % tq == 0 and s >= SPAN
    return pl.pallas_call(
        functools.partial(_attn_kernel, seq=s),
        grid=(b, s // tq),
        in_specs=[pl.BlockSpec((1, tq, ATTN_W), lambda bi, i: (bi, i, 0)),
                  pl.BlockSpec((1, s, ATTN_KV_COLS), lambda bi, i: (bi, 0, 0)),
                  _full(bias_tabs.shape),
                  _full((1, ATTN_Q_HEADS))],
        out_specs=pl.BlockSpec((1, tq, ATTN_W), lambda bi, i: (bi, i, 0)),
        out_shape=jax.ShapeDtypeStruct((b, s, ATTN_W), bf16),
        compiler_params=_cparams(("parallel", "arbitrary")),
        name="attn",
    )(aq, akv, bias_tabs, sink.reshape(1, ATTN_Q_HEADS).astype(f32))


def _softplus(x):
    return jnp.maximum(x, 0.0) + jnp.log1p(jnp.exp(-jnp.abs(x)))


QUAD = 4
QUAD_W = QUAD * DN_DK
N_QUADS = DN_HEADS // QUAD
DELTA_BLOCK_CHUNKS = 4


def _block_diag(xq):
    lane_head = lax.broadcasted_iota(jnp.int32, xq.shape, 1) // DN_DK
    return jnp.concatenate([jnp.where(lane_head == h, xq, 0.0).astype(bf16) for h in range(QUAD)], axis=0)


def _delta_chunk(q, k, v, gx, bx, d):
    C = DN_CHUNK
    W = DN_QK_W
    sign = 1 - 2 * d
    ii = lax.broadcasted_iota(jnp.int32, (C, W), 0)
    jl = lax.broadcasted_iota(jnp.int32, (C, W), 1) % DN_DK
    order = (ii - jl) * sign
    incl = order >= 0
    strict = order > 0
    rr = lax.broadcasted_iota(jnp.int32, (2 * C, 2 * C), 0)
    cc = lax.broadcasted_iota(jnp.int32, (2 * C, 2 * C), 1)
    tri = jnp.where(((rr % C) - cc) * sign >= 0, 1.0, 0.0)
    lmat = jnp.where(cc < C, tri, jnp.where(rr < C, -1.0, 0.0)).astype(bf16)
    rmat = jnp.concatenate([gx, jnp.where(order <= 0, gx, 0.0)], axis=0)
    dg = _dot(jnp.concatenate([lmat] * 2, axis=1), jnp.concatenate(_split2(rmat), axis=0))
    dmat, gc = dg[:C], dg[C:]
    decay = jnp.where(incl, jnp.exp(jnp.where(incl, dmat, 0.0)), 0.0)
    eg = jnp.exp(gc)
    last = C - 1 if d == 0 else 0
    gl = gc[last:last + 1, :]
    kdec = k * jnp.exp(gl - gc)
    cd = jnp.exp(gl)
    kb = k * bx
    eye = jnp.where(ii == jl, 1.0, 0.0)
    return dict(q=q, k=k, bx=bx, decay=decay, strict=strict, eye=eye, kdec=kdec, cd=cd,
                vb=v * bx, kbe=kb * eg, qdec=q * eg)


DELTA_CHAINS = 2 * DELTA_BLOCK_CHUNKS * N_QUADS


def _delta_kernel(qf_ref, kf_ref, vf_ref, qb_ref, kb_ref, vb_ref, abf_ref, abb_ref, alog_ref, dtb_ref,
                  of_ref, ob_ref, state_ref, u_s, w_s, qk_s, qdec_s, kdec_s, cd_s, *, nblk):
    C, H, NC = DN_CHUNK, DN_HEADS, DELTA_BLOCK_CHUNKS
    t = pl.program_id(0)
    slot_w = t % 2
    slot_r = 1 - slot_w

    @pl.when(t == 0)
    def _():
        for ref in (u_s, w_s, qk_s, qdec_s, kdec_s, cd_s):
            ref[1] = jnp.zeros(ref.shape[1:], ref.dtype)

    @pl.when((t + nblk - 1) % nblk == 0)
    def _():
        state_ref[...] = jnp.zeros(state_ref.shape, f32)

    def expand(x, parts):
        rows = lax.broadcasted_iota(jnp.int32, (H * parts, DN_QK_W), 0) % H
        cols = lax.broadcasted_iota(jnp.int32, (H * parts, DN_QK_W), 1) // DN_DK
        emat = jnp.where(rows == cols, 1.0, 0.0).astype(bf16)
        pieces = [p.astype(f32) for p in _split3(x)[:parts]]
        return _dot(jnp.concatenate(pieces, axis=-1).astype(bf16), emat)

    chains = []

    def stage_prepare():
        dirs = ((0, qf_ref, kf_ref, vf_ref, abf_ref), (1, qb_ref, kb_ref, vb_ref, abb_ref))
        for d, q_ref, k_ref, v_ref, ab_ref in dirs:
            ab = ab_ref[0, 0]
            g = -jnp.exp(alog_ref[d]) * _softplus(ab[:, :H] + dtb_ref[d])
            gx = expand(g, 3)
            bx = expand(_sigmoid(ab[:, H:]), 2)
            for c in range(NC):
                rows = slice(C * c, C * (c + 1))
                pre = _delta_chunk(q_ref[0, rows, :], k_ref[0, rows, :], v_ref[0, rows, :], gx[rows], bx[rows], d)
                for qd in range(N_QUADS):
                    sl = slice(QUAD_W * qd, QUAD_W * (qd + 1))
                    chains.append({n: x[:, sl] for n, x in pre.items()})

    def stage_gram():
        for ch in chains:
            kq = _dot_nt(jnp.concatenate([ch["k"], ch["q"]], axis=0).astype(bf16), _block_diag(ch["k"]))
            a_mat = jnp.where(ch["strict"], kq[:C] * ch["bx"] * ch["decay"], 0.0)
            ch["qk"] = kq[C:] * ch["decay"]
            ch["p"] = -a_mat
            ch["t"] = ch["eye"] - a_mat

    def stage_square():
        for ch in chains:
            ch["p"] = _dot(ch["p"].astype(bf16), _block_diag(ch["p"]))

    def stage_double():
        for ch in chains:
            pt = _dot(jnp.concatenate([ch["p"], ch["t"]], axis=0).astype(bf16), _block_diag(ch["p"]))
            ch["p"], ch["t"] = pt[:C], ch["t"] + pt[C:]

    def stage_last():
        for ch in chains:
            ch["tb"] = (ch["t"] + _dot(ch["t"].astype(bf16), _block_diag(ch["p"]))).astype(bf16)

    def stage_solve():
        for ch in chains:
            ch["u"] = _dot(ch["tb"], _block_diag(ch["vb"]))
            ch["w"] = _dot(ch["tb"], _block_diag(ch["kbe"]))

    row_head = lax.broadcasted_iota(jnp.int32, (QUAD_W, QUAD_W), 0) // DN_DK
    col_head = lax.broadcasted_iota(jnp.int32, (QUAD_W, QUAD_W), 1) // DN_DK
    o_refs = (of_ref, ob_ref)
    live = {}

    def scan_chains(step):
        for d in range(2):
            c = step if d == 0 else NC - 1 - step
            for qd in range(N_QUADS):
                yield d, c, qd, (d * NC + c) * N_QUADS + qd

    def scan_first(step):
        for d, c, qd, n in scan_chains(step):
            st = state_ref[d, qd]
            wq = _dot(jnp.concatenate([w_s[slot_r, n], qdec_s[slot_r, n]], axis=0), st.astype(bf16))
            live[n] = (st, u_s[slot_r, n] - wq[:C], wq[C:])

    def scan_second(step):
        for d, c, qd, n in scan_chains(step):
            st, v_new, qs = live.pop(n)
            o = qs + _dot(qk_s[slot_r, n], _block_diag(v_new))
            o_refs[d][0, C * c:C * (c + 1), QUAD_W * qd:QUAD_W * (qd + 1)] = o
            s_new = _dot_tn(kdec_s[slot_r, n], v_new.astype(bf16))
            state_ref[d, qd] = st * cd_s[slot_r, n, 0:1, :] + jnp.where(row_head == col_head, s_new, 0.0)

    local = [stage_prepare, stage_gram, stage_square] + [stage_double] * 4 + [stage_last, stage_solve]
    scan = [f for step in range(NC) for f in (functools.partial(scan_first, step),
                                              functools.partial(scan_second, step))]
    for k in range(max(len(local), len(scan))):
        if k < len(local):
            local[k]()
        if k < len(scan):
            scan[k]()

    for n, ch in enumerate(chains):
        u_s[slot_w, n] = ch["u"]
        w_s[slot_w, n] = ch["w"].astype(bf16)
        qk_s[slot_w, n] = ch["qk"].astype(bf16)
        qdec_s[slot_w, n] = ch["qdec"].astype(bf16)
        kdec_s[slot_w, n] = ch["kdec"].astype(bf16)
        cd_s[slot_w, n] = jnp.broadcast_to(ch["cd"], (8, QUAD_W))


def _delta(qkvn, dab, a_log, dt_bias):
    b, s, _ = qkvn.shape
    H = DN_HEADS
    tb = DN_CHUNK * DELTA_BLOCK_CHUNKS
    nblk = s // tb
    total = b * nblk
    assert s % tb == 0
    d4 = dab.reshape(b, s, 4, H)
    ab = jnp.stack([jnp.concatenate([d4[:, :, r], d4[:, :, 2 + r]], -1) for r in range(2)])
    alog = a_log.astype(f32).reshape(2, 1, H)
    dtb = dt_bias.astype(f32).reshape(2, 1, H)

    def block(lin, rev):
        i = lin % nblk
        return lin // nblk, (nblk - 1 - i) if rev else i

    def sec(n, rev, lag):
        def index(t):
            bi, i = block(jnp.maximum(t - 1, 0) if lag else jnp.minimum(t, total - 1), rev)
            return bi, i, n
        return pl.BlockSpec((1, tb, DN_QK_W), index)

    def ab_spec(d):
        def index(t):
            bi, i = block(jnp.minimum(t, total - 1), d == 1)
            return d, bi, i, 0
        return pl.BlockSpec((1, 1, tb, 2 * H), index)

    chain_buf = lambda dt: pltpu.VMEM((2, DELTA_CHAINS, DN_CHUNK, QUAD_W), dt)
    return pl.pallas_call(
        functools.partial(_delta_kernel, nblk=nblk),
        grid=(total + 1,),
        in_specs=[sec(0, False, False), sec(1, False, False), sec(2, False, False),
                  sec(0, True, False), sec(1, True, False), sec(2, True, False),
                  ab_spec(0), ab_spec(1), _full((2, 1, H)), _full((2, 1, H))],
        out_specs=[sec(0, False, True), sec(0, True, True)],
        out_shape=[jax.ShapeDtypeStruct((b, s, DN_V_W), f32)] * 2,
        scratch_shapes=[pltpu.VMEM((2, N_QUADS, QUAD_W, QUAD_W), f32),
                        chain_buf(f32), chain_buf(bf16), chain_buf(bf16), chain_buf(bf16), chain_buf(bf16),
                        pltpu.VMEM((2, DELTA_CHAINS, 8, QUAD_W), f32)],
        compiler_params=_cparams(("arbitrary",)),
        name="delta",
    )(qkvn, qkvn, qkvn, qkvn, qkvn, qkvn, ab, ab, alog, dtb)


def _pool_band_tables():
    r = np.arange(POOL_TILE)[:, None]
    j = np.arange(POOL_SPAN)[None, :]
    tabs = np.zeros((3, N_POOL_GROUPS, POOL_TILE, POOL_SPAN), np.float32)
    for vi, shift in enumerate((0, 128, 256)):
        rel = j - shift - r
        for gi, w in enumerate(POOL_WINDOWS):
            tabs[vi, gi] = np.abs(rel) <= w // 2
    return jnp.asarray(tabs, dtype=bf16)


def _pool_tile(u_ref, band_ref, w_ref, scale_ref, t, seq):
    nt = seq // POOL_TILE
    t0 = pl.multiple_of(t * POOL_TILE, POOL_TILE)
    ks = pl.multiple_of(jnp.clip(t0 - 128, 0, seq - POOL_SPAN), 128)
    variant = jnp.where(t == 0, 0, jnp.where(t == nt - 1, 2, 1))
    win = u_ref[0, pl.ds(ks, POOL_SPAN), :]
    own = u_ref[0, pl.ds(t0, POOL_TILE), :].astype(f32)
    pos = t0 + lax.broadcasted_iota(jnp.int32, (POOL_TILE, POOL_GROUP), 0)
    outs = []
    for gi, w in enumerate(POOL_WINDOWS):
        r = w // 2
        sl = slice(POOL_GROUP * gi, POOL_GROUP * (gi + 1))
        wsum = _dot(band_ref[variant, gi], win[:, sl])
        count = (jnp.minimum(pos + r, seq - 1) - jnp.maximum(pos - r, 0) + 1).astype(f32)
        mixed = wsum / count - own[:, sl]
        outs.append(_dot(mixed.astype(bf16), w_ref[gi]))
    return jnp.concatenate(outs, -1) * scale_ref[...]


def _route_tile(x, w_t, carry_ref):
    xh, xl = _split2(x)
    wh, wl = _split2(w_t)
    logits = _dot_nt(wh, xh) + (_dot_nt(wh, xl) + _dot_nt(wl, xh))
    tm = logits.shape[1]
    sub = lax.broadcasted_iota(jnp.int32, logits.shape, 0)
    m1 = jnp.max(logits, 0, keepdims=True)
    i1 = jnp.min(jnp.where(logits == m1, sub, N_EXPERTS), 0, keepdims=True)
    rest = jnp.where(sub == i1, -jnp.inf, logits)
    m2 = jnp.max(rest, 0, keepdims=True)
    i2 = jnp.min(jnp.where(rest == m2, sub, N_EXPERTS), 0, keepdims=True)
    e2 = jnp.exp(m2 - m1)
    p1 = 1.0 / (1.0 + e2)
    p2 = e2 / (1.0 + e2)
    oh1 = jnp.where(sub == i1, 1.0, 0.0)
    oh2 = jnp.where(sub == i2, 1.0, 0.0)
    both = oh1 + oh2
    rr = lax.broadcasted_iota(jnp.int32, (tm, tm), 0)
    cc = lax.broadcasted_iota(jnp.int32, (tm, tm), 1)
    earlier = jnp.where(rr < cc, 1.0, 0.0).astype(bf16)
    before = _dot(both.astype(bf16), earlier) + carry_ref[...]
    r1 = jnp.sum(oh1 * before, 0, keepdims=True).astype(jnp.int32)
    r2 = jnp.sum(oh2 * before, 0, keepdims=True).astype(jnp.int32)
    carry_ref[...] += jnp.sum(both, 1, keepdims=True)
    mi = jnp.where(sub == 0, i1, jnp.where(sub == 1, i2, jnp.where(sub == 2, r1, jnp.where(sub == 3, r2, 0))))
    mf = jnp.where(sub == 0, p1, jnp.where(sub == 1, p2, 0.0))
    return mi, mf


def _merge_kernel(oa_ref, odf_ref, odb_ref, z_ref, pu_ref, gate_ref, h_ref, ones_ref, nw_ref, band_ref, pw_ref,
                  ps_ref, wb_ref, wo_ref, g_ref, b_ref, *refs, seq, routed):
    if routed:
        rw_ref, hf_ref, hb_ref, mi_ref, mf_ref, cnt_ref, carry_ref = refs

        @pl.when(pl.program_id(0) == 0)
        def _():
            carry_ref[...] = jnp.zeros(carry_ref.shape, f32)
    else:
        hf_ref, hb_ref = refs
    od = odf_ref[...] + odb_ref[...]
    ms = _dot_split2_rhs(od * od, ones_ref[...]) * (1.0 / DN_DV)
    od = od * lax.rsqrt(ms + RMS_EPS) * nw_ref[...] * _silu(z_ref[...].astype(f32))
    op = _pool_tile(pu_ref, band_ref, pw_ref, ps_ref, pl.program_id(0) % (seq // POOL_TILE), seq)
    branches = (oa_ref[...], od.astype(bf16), op.astype(bf16))
    merged = None
    for n in range(N_BRANCH):
        up = _dot(branches[n], wb_ref[n])
        gate = _sigmoid(gate_ref[:, D_MODEL * n:D_MODEL * (n + 1)].astype(f32))
        merged = gate * up if merged is None else merged + gate * up
    y = _dot(merged.astype(bf16), wo_ref[...])
    hn = _layer_norm(DEEPNORM_ALPHA * h_ref[...] + y, g_ref[...], b_ref[...])
    hf_ref[...] = hn
    if routed:
        hb_ref[...] = _pack_bf16_pair(hn[:, :D_MODEL // 2], hn[:, D_MODEL // 2:])
        mi_ref[...], mf_ref[...] = _route_tile(hn, rw_ref[...], carry_ref)
        cnt_ref[...] = jnp.broadcast_to(carry_ref[...], cnt_ref.shape).astype(jnp.int32)
    else:
        hb_ref[...] = hn.astype(bf16)


def _merge(oa, odf, odb, z, pu, gates, h, dn_norm_w, pool_w, pool_scale, w_branch, w_out, ln_g, ln_b, router_w):
    m = h.shape[0]
    bsz, seq, _ = pu.shape
    tm = POOL_TILE
    nt = seq // tm
    assert seq % tm == 0 and seq >= POOL_SPAN
    routed = router_w is not None
    row = lambda w: pl.BlockSpec((tm, w), lambda i: (i, 0))
    nw = jnp.tile(dn_norm_w.astype(f32), DN_HEADS).reshape(1, DN_V_W)
    in_specs = [row(ATTN_W), row(DN_V_W), row(DN_V_W), row(DN_V_W),
                pl.BlockSpec((1, seq, POOL_W), lambda i: (i // nt, 0, 0)),
                row(N_BRANCH * D_MODEL), row(D_MODEL),
                _full((DN_V_W, DN_V_W)), _full((1, DN_V_W)),
                _full((3, N_POOL_GROUPS, POOL_TILE, POOL_SPAN)), _full((N_POOL_GROUPS, POOL_GROUP, POOL_GROUP)),
                _full((1, POOL_W)),
                _full((N_BRANCH, ATTN_W, D_MODEL)), _full((D_MODEL, D_MODEL)),
                _full((1, D_MODEL)), _full((1, D_MODEL))]
    args = [oa, odf, odb, z, pu, gates, h, _head_ones(), nw, _pool_band_tables(), pool_w.astype(bf16),
            pool_scale.reshape(1, POOL_W).astype(f32), w_branch.astype(bf16), w_out.astype(bf16),
            ln_g.reshape(1, D_MODEL).astype(f32), ln_b.reshape(1, D_MODEL).astype(f32)]
    out_specs = [row(D_MODEL)]
    out_shape = [jax.ShapeDtypeStruct((m, D_MODEL), f32)]
    scratch = []
    if routed:
        col = pl.BlockSpec((N_EXPERTS, tm), lambda i: (0, i))
        in_specs.append(_full((N_EXPERTS, D_MODEL)))
        args.append(router_w.astype(f32).T)
        out_specs += [row(PACKED_W), col, col, _full((N_EXPERTS, LANES))]
        out_shape += [jax.ShapeDtypeStruct((m, PACKED_W), jnp.uint32),
                      jax.ShapeDtypeStruct((N_EXPERTS, m), jnp.int32),
                      jax.ShapeDtypeStruct((N_EXPERTS, m), f32),
                      jax.ShapeDtypeStruct((N_EXPERTS, LANES), jnp.int32)]
        scratch.append(pltpu.VMEM((N_EXPERTS, 1), f32))
    else:
        out_specs.append(row(D_MODEL))
        out_shape.append(jax.ShapeDtypeStruct((m, D_MODEL), bf16))
    return pl.pallas_call(
        functools.partial(_merge_kernel, seq=seq, routed=routed),
        grid=(m // tm,),
        in_specs=in_specs,
        out_specs=out_specs,
        out_shape=out_shape,
        scratch_shapes=scratch,
        compiler_params=_cparams(("arbitrary",)),
        name="merge",
    )(*args)


def _swiglu_chunk(x, wg, wu, wd):
    g = _dot(x, wg.astype(bf16))
    u = _dot(x, wu.astype(bf16))
    return _dot((_silu(g) * u).astype(bf16), wd.astype(bf16))


def _ffn_kernel(x_ref, h_ref, wg_ref, wu_ref, wd_ref, g_ref, b_ref, hf_ref, hb_ref, acc_ref):
    acc_ref[...] = _swiglu_chunk(x_ref[...], wg_ref[0], wu_ref[0], wd_ref[0])

    def chunk(j, carry):
        acc_ref[...] += _swiglu_chunk(x_ref[...], wg_ref[j], wu_ref[j], wd_ref[j])
        return carry

    lax.fori_loop(1, wg_ref.shape[0], chunk, 0)
    hn = _layer_norm(DEEPNORM_ALPHA * h_ref[...] + acc_ref[...], g_ref[...], b_ref[...])
    hf_ref[...] = hn
    hb_ref[...] = hn.astype(bf16)


def _ffn(hb, hf, w1, w3, w2, ln_g, ln_b, tm=1024, tf=256):
    m = hb.shape[0]
    tm = min(tm, m)
    ff = w1.shape[-1]
    nj = ff // tf
    row = pl.BlockSpec((tm, D_MODEL), lambda i: (i, 0))
    resident = lambda shape: pl.BlockSpec(shape, lambda i: (0,) * len(shape), pipeline_mode=pl.Buffered(1))
    up_chunks = lambda w: jnp.transpose(w.astype(bf16).reshape(D_MODEL, nj, tf), (1, 0, 2))
    return pl.pallas_call(
        _ffn_kernel,
        grid=(m // tm,),
        in_specs=[row, row, resident((nj, D_MODEL, tf)), resident((nj, D_MODEL, tf)), resident((nj, tf, D_MODEL)),
                  _full((1, D_MODEL)), _full((1, D_MODEL))],
        out_specs=[row, row],
        out_shape=[jax.ShapeDtypeStruct((m, D_MODEL), f32), jax.ShapeDtypeStruct((m, D_MODEL), bf16)],
        scratch_shapes=[pltpu.VMEM((tm, D_MODEL), f32)],
        compiler_params=_cparams(("parallel",)),
        name="ffn",
    )(hb, hf, up_chunks(w1), up_chunks(w3), w2.astype(bf16).reshape(nj, tf, D_MODEL),
      ln_g.reshape(1, D_MODEL).astype(f32), ln_b.reshape(1, D_MODEL).astype(f32))


MOE_TILE = 1024
PACKED_W = D_MODEL // 2
SC_CORES = 2
SC_SUBCORES = 16
SC_WORKERS = SC_CORES * SC_SUBCORES
SC_ROWS = 64


def _gmm_kernel(te_ref, tv_ref, x_ref, wg_ref, wu_ref, wd_ref, o_ref, acc_ref):
    i = pl.program_id(0)
    j = pl.program_id(1)
    valid = tv_ref[i]

    @pl.when(j == 0)
    def _():
        acc_ref[...] = jnp.zeros(acc_ref.shape, f32)

    @pl.when(valid > 0)
    def _():
        rows = lax.broadcasted_iota(jnp.int32, acc_ref.shape, 0)
        x = jnp.where(rows < valid, _unpack_bf16_pair(x_ref[...]), 0.0).astype(bf16)
        acc_ref[...] += _swiglu_chunk(x, wg_ref[0], wu_ref[0], wd_ref[0])

    @pl.when(j == pl.num_programs(1) - 1)
    def _():
        o_ref[...] = _pack_bf16_pair(acc_ref[:, :PACKED_W], acc_ref[:, PACKED_W:])


def _gmm(xs, tile_expert, tile_valid, wg, wu, wd, tm, tf=512):
    n = xs.shape[0]
    ff = wg.shape[-1]
    nj = ff // tf

    def fcol(i, j, te, tv):
        return jnp.where(tv[i] > 0, j, nj - 1)

    grid_spec = pltpu.PrefetchScalarGridSpec(
        num_scalar_prefetch=2,
        grid=(n // tm, nj),
        in_specs=[pl.BlockSpec((tm, PACKED_W), lambda i, j, te, tv: (i, 0)),
                  pl.BlockSpec((1, D_MODEL, tf), lambda i, j, te, tv: (te[i], 0, fcol(i, j, te, tv))),
                  pl.BlockSpec((1, D_MODEL, tf), lambda i, j, te, tv: (te[i], 0, fcol(i, j, te, tv))),
                  pl.BlockSpec((1, tf, D_MODEL), lambda i, j, te, tv: (te[i], fcol(i, j, te, tv), 0))],
        out_specs=pl.BlockSpec((tm, PACKED_W), lambda i, j, te, tv: (i, 0)),
        scratch_shapes=[pltpu.VMEM((tm, D_MODEL), f32)])
    return pl.pallas_call(
        _gmm_kernel,
        grid_spec=grid_spec,
        out_shape=jax.ShapeDtypeStruct((n, PACKED_W), jnp.uint32),
        compiler_params=_cparams(("parallel", "arbitrary")),
        name="moe_gmm",
    )(tile_expert, tile_valid, xs, wg, wu, wd)


def _sc_mesh():
    return plsc.VectorSubcoreMesh(core_axis_name="c", subcore_axis_name="s",
                                  num_cores=SC_CORES, num_subcores=SC_SUBCORES)


def _sc_worker_base(per_worker):
    return (lax.axis_index("s") * SC_CORES + lax.axis_index("c")) * per_worker


def _sc_scratch(d, dtype):
    return [pltpu.VMEM((SC_ROWS,), jnp.int32), pltpu.VMEM((SC_ROWS,), jnp.int32), pltpu.VMEM((2, SC_ROWS, d), dtype),
            pltpu.SemaphoreType.DMA((2,)), pltpu.SemaphoreType.DMA((2,))]


def _sc_scatter_rows(src, idx0, idx1, n_out):
    m, d = src.shape
    per_worker = m // SC_WORKERS
    n_chunks = per_worker // SC_ROWS
    assert m % (SC_WORKERS * SC_ROWS * 2) == 0

    def body(src_hbm, idx0_hbm, idx1_hbm, out_hbm, idx_a, idx_b, rows_v, sem_in, sem_out):
        base = _sc_worker_base(per_worker)
        idx_v = (idx_a, idx_b)

        def load(j, slot):
            off = pl.multiple_of(base + j * SC_ROWS, SC_ROWS)
            return pltpu.make_async_copy(src_hbm.at[pl.ds(off, SC_ROWS)], rows_v.at[slot], sem_in.at[slot])

        def store(j, slot, idx_hbm):
            off = pl.multiple_of(base + j * SC_ROWS, SC_ROWS)
            pltpu.sync_copy(idx_hbm.at[pl.ds(off, SC_ROWS)], idx_v[slot])
            pltpu.async_copy(rows_v.at[slot], out_hbm.at[idx_v[slot]], sem_out.at[slot]).wait()

        load(0, 0).start()

        @pl.loop(0, n_chunks, step=2)
        def _(j):
            for slot in range(2):
                @pl.when(j + slot + 1 < n_chunks)
                def _():
                    load(j + slot + 1, 1 - slot).start()
                load(j + slot, slot).wait()
                store(j + slot, slot, idx0_hbm)
                store(j + slot, slot, idx1_hbm)

    return pl.kernel(
        body, out_type=jax.ShapeDtypeStruct((n_out, d), src.dtype), mesh=_sc_mesh(),
        scratch_types=_sc_scratch(d, src.dtype), name="moe_dispatch",
    )(src, idx0, idx1)


def _sc_gather_rows(table, idx):
    n = idx.shape[0]
    d = table.shape[1]
    per_worker = n // SC_WORKERS
    n_chunks = per_worker // SC_ROWS
    assert n % (SC_WORKERS * SC_ROWS * 2) == 0

    def body(table_hbm, idx_hbm, out_hbm, idx_a, idx_b, rows_v, sem_in, sem_out):
        base = _sc_worker_base(per_worker)
        idx_v = (idx_a, idx_b)

        def gather(j, slot):
            off = pl.multiple_of(base + j * SC_ROWS, SC_ROWS)
            pltpu.sync_copy(idx_hbm.at[pl.ds(off, SC_ROWS)], idx_v[slot])
            return pltpu.make_async_copy(table_hbm.at[idx_v[slot]], rows_v.at[slot], sem_in.at[slot])

        def wait_gather(slot):
            pltpu.make_async_copy(table_hbm.at[idx_v[slot]], rows_v.at[slot], sem_in.at[slot]).wait()

        def write(j, slot):
            off = pl.multiple_of(base + j * SC_ROWS, SC_ROWS)
            return pltpu.make_async_copy(rows_v.at[slot], out_hbm.at[pl.ds(off, SC_ROWS)], sem_out.at[slot])

        gather(0, 0).start()

        @pl.loop(0, n_chunks, step=2)
        def _(j):
            for slot in range(2):
                @pl.when(j + slot + 1 < n_chunks)
                def _():
                    @pl.when(j + slot >= 1)
                    def _():
                        write(j + slot - 1, 1 - slot).wait()
                    gather(j + slot + 1, 1 - slot).start()
                wait_gather(slot)
                write(j + slot, slot).start()

        write(n_chunks - 2, 0).wait()
        write(n_chunks - 1, 1).wait()

    return pl.kernel(
        body, out_type=jax.ShapeDtypeStruct((n, d), table.dtype), mesh=_sc_mesh(),
        scratch_types=_sc_scratch(d, table.dtype), name="moe_collect",
    )(table, idx)


def _combine_kernel(y0_ref, y1_ref, mf_ref, h_ref, g_ref, b_ref, o_ref):
    p = mf_ref[...]
    y = p[:, 0:1] * _unpack_bf16_pair(y0_ref[0]) + p[:, 1:2] * _unpack_bf16_pair(y1_ref[0])
    o_ref[...] = _layer_norm(DEEPNORM_ALPHA * h_ref[...] + y, g_ref[...], b_ref[...])


def _combine(yg, mf, hf, ln_g, ln_b, tm=512):
    m = hf.shape[0]
    tm = min(tm, m)
    return pl.pallas_call(
        _combine_kernel,
        grid=(m // tm,),
        in_specs=[pl.BlockSpec((1, tm, PACKED_W), lambda i: (0, i, 0)),
                  pl.BlockSpec((1, tm, PACKED_W), lambda i: (1, i, 0)),
                  pl.BlockSpec((tm, N_EXPERTS), lambda i: (i, 0)),
                  pl.BlockSpec((tm, D_MODEL), lambda i: (i, 0)),
                  _full((1, D_MODEL)), _full((1, D_MODEL))],
        out_specs=pl.BlockSpec((tm, D_MODEL), lambda i: (i, 0)),
        out_shape=jax.ShapeDtypeStruct((m, D_MODEL), f32),
        compiler_params=_cparams(("parallel",)),
        name="moe_combine",
    )(yg, yg, mf, hf, ln_g.reshape(1, D_MODEL).astype(f32), ln_b.reshape(1, D_MODEL).astype(f32))


def _moe(hf, hp, mi, mf, cnt, wg, wu, wd, ln_g, ln_b):
    m = hf.shape[0]
    tm = MOE_TILE
    counts = cnt[:, 0]
    padded = (counts + tm - 1) // tm * tm
    ends = jnp.cumsum(padded)
    starts = ends - padded
    chosen = mi[0:TOP_K, :, None] == jnp.arange(N_EXPERTS, dtype=jnp.int32)
    pos = jnp.sum(jnp.where(chosen, starts, 0), -1) + mi[TOP_K:2 * TOP_K]
    n_tiles = TOP_K * m // tm + N_EXPERTS
    tile_start = jnp.arange(n_tiles, dtype=jnp.int32) * tm
    tile_expert = jnp.minimum(jnp.sum(tile_start[:, None] >= ends[None, :], -1), N_EXPERTS - 1).astype(jnp.int32)
    tile_valid = jnp.clip(starts[tile_expert] + counts[tile_expert] - tile_start, 0, tm).astype(jnp.int32)
    tile_valid = jnp.where(tile_start < ends[-1], tile_valid, 0)
    xs = _sc_scatter_rows(hp, pos[0], pos[1], n_tiles * tm)
    ys = _gmm(xs, tile_expert, tile_valid, wg, wu, wd, tm)
    yg = _sc_gather_rows(ys, pos.reshape(TOP_K * m))
    return _combine(yg.reshape(TOP_K, m, PACKED_W), mf.T, hf, ln_g, ln_b)


_SPLITS = (ATTN_W, KV_W, KV_W, 3 * DN_QK_W, DN_V_W, 4 * DN_HEADS, POOL_W, N_BRANCH * D_MODEL)
_SPLIT_DTYPES = (bf16, bf16, f32, bf16, f32, bf16, bf16)


def _mixer(hf, hb, bsz, seq, bias_tabs, w_in, sink, conv_w, a_log, dt_bias, dn_norm_w, pool_w, pool_scale,
           w_branch, w_out, ln_g, ln_b, router_w):
    points = np.cumsum(_SPLITS)[:-1].tolist()
    wq, wk, wv, *rest = jnp.split(w_in.astype(bf16), points, axis=-1)
    wkv = jnp.concatenate([wk[:, HEAD_DIM * g:HEAD_DIM * (g + 1)] for g in range(ATTN_KV_HEADS) for _ in range(2)]
                          + [wv], axis=-1)
    aq, akv, qkvn, dz, dab, pu, gates = _inproj(hf if hb is None else hb, [wq, wkv] + rest, _SPLIT_DTYPES,
                                                conv_w, seq, dn=2)
    shp = lambda t: t.reshape(bsz, seq, t.shape[-1])
    oa = _attention(shp(aq), shp(akv), bias_tabs, sink)
    odf, odb = _delta(shp(qkvn), shp(dab), a_log, dt_bias)
    m = bsz * seq
    return _merge(oa.reshape(m, ATTN_W), odf.reshape(m, DN_V_W), odb.reshape(m, DN_V_W), dz, shp(pu), gates, hf,
                  dn_norm_w, pool_w, pool_scale, w_branch, w_out, ln_g, ln_b, router_w)


def kernel(x, w_in, attn_sink, rel_bias, conv_w, a_log, dt_bias, dn_norm_w, pool_w, pool_scale, w_branch, w_out,
           ln1_g, ln1_b, ln2_g, ln2_b, ffn_w1, ffn_w3, ffn_w2, router_w, moe_wg, moe_wu, moe_wd):
    bsz, seq, _ = x.shape
    m = bsz * seq
    bias_tabs = _attn_bias_tables(rel_bias)
    hf = x.reshape(m, D_MODEL).astype(f32)
    hb = None
    for layer in range(DEPTH):
        dense = layer % 2 == 0
        i = layer // 2
        hf, *rest = _mixer(hf, hb, bsz, seq, bias_tabs, w_in[layer], attn_sink[layer], conv_w[layer], a_log[layer],
                           dt_bias[layer], dn_norm_w[layer], pool_w[layer], pool_scale[layer], w_branch[layer],
                           w_out[layer], ln1_g[layer], ln1_b[layer], None if dense else router_w[i])
        if dense:
            hf, hb = _ffn(rest[0], hf, ffn_w1[i], ffn_w3[i], ffn_w2[i], ln2_g[layer], ln2_b[layer])
        else:
            hf = _moe(hf, *rest, moe_wg[i], moe_wu[i], moe_wd[i], ln2_g[layer], ln2_b[layer])
            hb = None
    return hf.reshape(bsz, seq, D_MODEL).astype(x.dtype)
```

```python
import functools

import numpy as np
import jax
import jax.numpy as jnp
from jax import lax
from jax.experimental import pallas as pl
from jax.experimental.pallas import tpu as pltpu
from jax.experimental.pallas import tpu_sc as plsc

f32 = jnp.float32
bf16 = jnp.bfloat16

D_MODEL = 1024
DEPTH = 2
HEAD_DIM = 64
ATTN_Q_HEADS = 8
ATTN_KV_HEADS = 2
ATTN_GROUP = ATTN_Q_HEADS // ATTN_KV_HEADS
WINDOW = 128
BLOCK = 128
SPAN = BLOCK + 2 * WINDOW
N_BUCKETS = 32
MAX_DISTANCE = 128
DN_HEADS = 8
DN_DK = 64
DN_DV = 64
DN_CONV = 5
DN_CHUNK = 64
POOL_WINDOWS = (2, 4, 8, 16)
N_POOL_GROUPS = 4
POOL_GROUP = 128
ATTN_W = ATTN_Q_HEADS * HEAD_DIM
KV_W = ATTN_KV_HEADS * HEAD_DIM
DN_QK_W = DN_HEADS * DN_DK
DN_V_W = DN_HEADS * DN_DV
POOL_W = N_POOL_GROUPS * POOL_GROUP
N_BRANCH = 3
N_EXPERTS = 8
TOP_K = 2
DEEPNORM_ALPHA = (2.0 * DEPTH) ** 0.25
LN_EPS = 1e-5
RMS_EPS = 1e-6
NEG_INF = -1e30

VMEM_LIMIT_BYTES = 56 * 1024 * 1024
LANES = 128
POOL_TILE = 256
POOL_SPAN = 512
MERGE_POOL_TILES = 2


def _cparams(sem):
    return pltpu.CompilerParams(dimension_semantics=sem, vmem_limit_bytes=VMEM_LIMIT_BYTES)


def _full(shape):
    n = len(shape)
    return pl.BlockSpec(shape, lambda *_: (0,) * n)


def _sigmoid(x):
    return 0.5 * jnp.tanh(0.5 * x) + 0.5


def _silu(x):
    return x * _sigmoid(x)


def _layer_norm(x, g, b):
    mu = jnp.mean(x, -1, keepdims=True)
    xc = x - mu
    var = jnp.mean(xc * xc, -1, keepdims=True)
    return xc * lax.rsqrt(var + LN_EPS) * g + b


def _dot(a, b):
    return jnp.dot(a, b, preferred_element_type=f32)


def _dot_nt(a, b):
    return lax.dot_general(a, b, (((1,), (1,)), ((), ())), preferred_element_type=f32)


def _dot_tn(a, b):
    return lax.dot_general(a, b, (((0,), (0,)), ((), ())), preferred_element_type=f32)


def _split3(x):
    hi = x.astype(bf16)
    r = x - hi.astype(f32)
    mid = r.astype(bf16)
    lo = (r - mid.astype(f32)).astype(bf16)
    return hi, mid, lo


def _split2(x):
    hi = x.astype(bf16)
    return hi, (x - hi.astype(f32)).astype(bf16)


def _dot_split2_rhs(x, b_bf):
    hi, lo = _split2(x)
    return _dot(hi, b_bf) + _dot(lo, b_bf)


def _pack_bf16_pair(a, b):
    def rounded(x):
        bits = lax.bitcast_convert_type(x, jnp.uint32)
        return bits + jnp.uint32(0x7FFF) + ((bits >> 16) & jnp.uint32(1))
    return (rounded(b) & jnp.uint32(0xFFFF0000)) | (rounded(a) >> 16)


def _unpack_bf16_pair(p):
    lo = lax.bitcast_convert_type(p << 16, f32)
    hi = lax.bitcast_convert_type(p & jnp.uint32(0xFFFF0000), f32)
    return jnp.concatenate([lo, hi], axis=-1)


INPROJ_TILE = 512
DN_HALO = 16


def _head_ones():
    blk = np.arange(DN_QK_W) // DN_DK
    return jnp.asarray(blk[:, None] == blk[None, :], dtype=bf16)


def _inproj_kernel(x_ref, xp_ref, xn_ref, cw_ref, ones_ref, *refs, nt, dn):
    n = len(refs) // 2
    w_refs, o_refs = refs[:n], refs[n:]
    x = x_ref[...].astype(bf16)
    t = pl.program_id(0) % nt
    x_ext = jnp.concatenate([xp_ref[...].astype(bf16), x, xn_ref[...].astype(bf16)], axis=0)
    rows = x_ext.shape[0]
    row = lax.broadcasted_iota(jnp.int32, (rows, DN_QK_W), 0)
    first = jnp.where(t == 0, DN_HALO, 0)
    end = jnp.where(t == nt - 1, rows - DN_HALO, rows)
    inside = (row >= first) & (row < end)
    u = _dot(x_ext, w_refs[dn][...])
    jobs = [(idx, c0) for idx in range(n) if idx != dn for c0 in range(0, o_refs[idx].shape[1], D_MODEL)]
    jobs.sort(key=lambda job: -min(D_MODEL, o_refs[job[0]].shape[1] - job[1]))
    for sec in range(3):
        for idx, c0 in jobs[sec::3]:
            c1 = min(c0 + D_MODEL, o_refs[idx].shape[1])
            o_refs[idx][:, c0:c1] = _dot(x, w_refs[idx][:, c0:c1]).astype(o_refs[idx].dtype)
        cols = slice(DN_QK_W * sec, DN_QK_W * (sec + 1))
        us = jnp.where(inside, u[:, cols], 0.0)
        z = [us * cw_ref[k:k + 1, cols] for k in range(DN_CONV)]
        after = pltpu.roll(z[3] + pltpu.roll(z[4], rows - 1, axis=0), rows - 1, axis=0)
        before = pltpu.roll(z[1] + pltpu.roll(z[0], 1, axis=0), 1, axis=0)
        y = _silu((z[2] + after + before)[DN_HALO:rows - DN_HALO])
        if sec < 2:
            ss = _dot((y * y).astype(bf16), ones_ref[...])
            y = y * lax.rsqrt(ss + 1e-6) * (DN_DK ** -0.5 if sec == 0 else 1.0)
        o_refs[dn][:, cols] = y


def _inproj(x, ws, dtypes, conv_w, seq, dn):
    m = x.shape[0]
    tm = INPROJ_TILE
    halo_blocks = tm // DN_HALO
    assert seq % tm == 0 and tm % DN_HALO == 0 and DN_CONV == 5
    last = m // DN_HALO - 1
    return pl.pallas_call(
        functools.partial(_inproj_kernel, nt=seq // tm, dn=dn),
        grid=(m // tm,),
        in_specs=[pl.BlockSpec((tm, D_MODEL), lambda i: (i, 0)),
                  pl.BlockSpec((DN_HALO, D_MODEL), lambda i: (jnp.maximum(i * halo_blocks - 1, 0), 0)),
                  pl.BlockSpec((DN_HALO, D_MODEL), lambda i: (jnp.minimum((i + 1) * halo_blocks, last), 0)),
                  _full(conv_w.shape), _full((DN_QK_W, DN_QK_W))]
        + [pl.BlockSpec(w.shape, lambda i: (0, 0), pipeline_mode=pl.Buffered(1)) for w in ws],
        out_specs=[pl.BlockSpec((tm, w.shape[1]), lambda i: (i, 0)) for w in ws],
        out_shape=[jax.ShapeDtypeStruct((m, w.shape[1]), dt) for w, dt in zip(ws, dtypes)],
        compiler_params=_cparams(("parallel",)),
        name="inproj",
    )(x, x, x, conv_w.astype(f32), _head_ones(), *ws)


def _t5_bucket(rel):
    nb = N_BUCKETS // 2
    max_exact = nb // 2
    n = np.abs(rel)
    large = max_exact + (np.log(np.maximum(n, 1) / max_exact) / np.log(MAX_DISTANCE / max_exact)
                         * (nb - max_exact)).astype(np.int32)
    large = np.minimum(large, nb - 1)
    return (np.where(rel > 0, nb, 0) + np.where(n < max_exact, n, large)).astype(np.int32)


ATTN_KV_COLS = 2 * KV_W + KV_W


def _bias_bucket_tables():
    j = np.arange(SPAN)[:, None]
    r = np.arange(BLOCK)[None, :]
    tabs = []
    for shift in (0, WINDOW, 2 * WINDOW):
        rel = j - shift - r
        tabs.append(np.where(np.abs(rel) <= WINDOW, _t5_bucket(rel), N_BUCKETS))
    return jnp.asarray(np.stack(tabs), dtype=jnp.int32)


def _bias_table_kernel(bucket_ref, rb_ref, o_ref):
    bucket = bucket_ref[0]
    for h in range(ATTN_Q_HEADS):
        acc = jnp.full(bucket.shape, NEG_INF, f32)
        for b in range(N_BUCKETS):
            acc = jnp.where(bucket == b, rb_ref[b, h], acc)
        o_ref[0, h] = acc


def _attn_bias_tables(rel_bias):
    return pl.pallas_call(
        _bias_table_kernel,
        grid=(3,),
        in_specs=[pl.BlockSpec((1, SPAN, BLOCK), lambda v: (v, 0, 0)),
                  pl.BlockSpec(memory_space=pltpu.SMEM)],
        out_specs=pl.BlockSpec((1, ATTN_Q_HEADS, SPAN, BLOCK), lambda v: (v, 0, 0, 0)),
        out_shape=jax.ShapeDtypeStruct((3, ATTN_Q_HEADS, SPAN, BLOCK), f32),
        compiler_params=_cparams(("parallel",)),
        name="attn_bias",
    )(_bias_bucket_tables(), rel_bias.astype(f32))


ATTN_STEP_BLOCKS = 4


def _attn_kernel(q_ref, kv_ref, bias_ref, sink_ref, o_ref, *, seq):
    nb = seq // BLOCK
    scale = HEAD_DIM ** -0.5
    pair_w = 2 * HEAD_DIM
    low = lax.broadcasted_iota(jnp.int32, (BLOCK, pair_w), 1) < HEAD_DIM
    blocks = []
    for sb in range(ATTN_STEP_BLOCKS):
        i = pl.program_id(1) * ATTN_STEP_BLOCKS + sb
        ks = pl.multiple_of(jnp.clip(i * BLOCK - WINDOW, 0, seq - SPAN), BLOCK)
        blocks.append(dict(variant=jnp.where(i == 0, 0, jnp.where(i == nb - 1, 2, 1)),
                           q=q_ref[0, BLOCK * sb:BLOCK * (sb + 1), :], kv=kv_ref[0, pl.ds(ks, SPAN), :]))
    pairs = [(blk, g) for blk in blocks for g in range(ATTN_KV_HEADS)]
    for blk in blocks:
        blk["v_t"] = blk["kv"][:, 2 * KV_W:].astype(f32).T.astype(bf16)
        blk["o_t"] = []
    s_t = []
    for blk, g in pairs:
        kk = blk["kv"][:, pair_w * g:pair_w * (g + 1)]
        qm = []
        for hl in range(ATTN_GROUP):
            h = ATTN_GROUP * g + hl
            qp = blk["q"][:, pair_w * (h // 2):pair_w * (h // 2 + 1)]
            qm.append(jnp.where(low if h % 2 == 0 else jnp.logical_not(low), qp, jnp.zeros_like(qp)))
        s_t.append(_dot_nt(kk, jnp.concatenate(qm, axis=0)))
    p_t = []
    for (blk, g), st in zip(pairs, s_t):
        p_n = []
        for hl in range(ATTN_GROUP):
            h = ATTN_GROUP * g + hl
            s = st[:, BLOCK * hl:BLOCK * (hl + 1)] * scale + bias_ref[blk["variant"], h]
            sk = sink_ref[0:1, h:h + 1]
            m = jnp.maximum(jnp.max(s, 0, keepdims=True), sk)
            p = jnp.exp(s - m)
            denom = jnp.sum(p, 0, keepdims=True) + jnp.exp(sk - m)
            p_n.append((p * (1.0 / denom)).astype(bf16))
        p_t.append(jnp.concatenate(p_n, axis=1))
    for (blk, g), pt in zip(pairs, p_t):
        og = _dot(blk["v_t"][HEAD_DIM * g:HEAD_DIM * (g + 1), :], pt)
        blk["o_t"] += [og[:, BLOCK * hl:BLOCK * (hl + 1)] for hl in range(ATTN_GROUP)]
    for sb, blk in enumerate(blocks):
        o = jnp.concatenate(blk["o_t"], axis=0).T
        o_ref[0, BLOCK * sb:BLOCK * (sb + 1), :] = o.astype(o_ref.dtype)


def _attention(aq, akv, bias_tabs, sink):
    b, s, _ = aq.shape
    tq = BLOCK * ATTN_STEP_BLOCKS
    assert s % tq == 0 and s >= SPAN
    return pl.pallas_call(
        functools.partial(_attn_kernel, seq=s),
        grid=(b, s // tq),
        in_specs=[pl.BlockSpec((1, tq, ATTN_W), lambda bi, i: (bi, i, 0)),
                  pl.BlockSpec((1, s, ATTN_KV_COLS), lambda bi, i: (bi, 0, 0)),
                  _full(bias_tabs.shape),
                  _full((1, ATTN_Q_HEADS))],
        out_specs=pl.BlockSpec((1, tq, ATTN_W), lambda bi, i: (bi, i, 0)),
        out_shape=jax.ShapeDtypeStruct((b, s, ATTN_W), bf16),
        compiler_params=_cparams(("parallel", "arbitrary")),
        name="attn",
    )(aq, akv, bias_tabs, sink.reshape(1, ATTN_Q_HEADS).astype(f32))


def _softplus(x):
    return jnp.maximum(x, 0.0) + jnp.log1p(jnp.exp(-jnp.abs(x)))


QUAD = 4
QUAD_W = QUAD * DN_DK
N_QUADS = DN_HEADS // QUAD
DELTA_BLOCK_CHUNKS = 4


def _block_diag(xq):
    lane_head = lax.broadcasted_iota(jnp.int32, xq.shape, 1) // DN_DK
    return jnp.concatenate([jnp.where(lane_head == h, xq, 0.0).astype(bf16) for h in range(QUAD)], axis=0)


def _delta_chunk(q, k, v, gx, bx, d):
    C = DN_CHUNK
    W = DN_QK_W
    sign = 1 - 2 * d
    ii = lax.broadcasted_iota(jnp.int32, (C, W), 0)
    jl = lax.broadcasted_iota(jnp.int32, (C, W), 1) % DN_DK
    order = (ii - jl) * sign
    incl = order >= 0
    strict = order > 0
    rr = lax.broadcasted_iota(jnp.int32, (2 * C, 2 * C), 0)
    cc = lax.broadcasted_iota(jnp.int32, (2 * C, 2 * C), 1)
    tri = jnp.where(((rr % C) - cc) * sign >= 0, 1.0, 0.0)
    lmat = jnp.where(cc < C, tri, jnp.where(rr < C, -1.0, 0.0)).astype(bf16)
    rmat = jnp.concatenate([gx, jnp.where(order <= 0, gx, 0.0)], axis=0)
    dg = _dot(jnp.concatenate([lmat] * 2, axis=1), jnp.concatenate(_split2(rmat), axis=0))
    dmat, gc = dg[:C], dg[C:]
    decay = jnp.where(incl, jnp.exp(jnp.where(incl, dmat, 0.0)), 0.0)
    eg = jnp.exp(gc)
    last = C - 1 if d == 0 else 0
    gl = gc[last:last + 1, :]
    kdec = k * jnp.exp(gl - gc)
    cd = jnp.exp(gl)
    kb = k * bx
    eye = jnp.where(ii == jl, 1.0, 0.0)
    return dict(q=q, k=k, bx=bx, decay=decay, strict=strict, eye=eye, kdec=kdec, cd=cd,
                vb=v * bx, kbe=kb * eg, qdec=q * eg)


DELTA_CHAINS = 2 * DELTA_BLOCK_CHUNKS * N_QUADS


def _delta_kernel(qf_ref, kf_ref, vf_ref, qb_ref, kb_ref, vb_ref, abf_ref, abb_ref, alog_ref, dtb_ref,
                  of_ref, ob_ref, state_ref, u_s, w_s, qk_s, qdec_s, kdec_s, cd_s, *, nblk):
    C, H, NC = DN_CHUNK, DN_HEADS, DELTA_BLOCK_CHUNKS
    t = pl.program_id(0)
    slot_w = t % 2
    slot_r = 1 - slot_w

    @pl.when(t == 0)
    def _():
        for ref in (u_s, w_s, qk_s, qdec_s, kdec_s, cd_s):
            ref[1] = jnp.zeros(ref.shape[1:], ref.dtype)

    @pl.when((t + nblk - 1) % nblk == 0)
    def _():
        state_ref[...] = jnp.zeros(state_ref.shape, f32)

    def expand(x, parts):
        rows = lax.broadcasted_iota(jnp.int32, (H * parts, DN_QK_W), 0) % H
        cols = lax.broadcasted_iota(jnp.int32, (H * parts, DN_QK_W), 1) // DN_DK
        emat = jnp.where(rows == cols, 1.0, 0.0).astype(bf16)
        pieces = [p.astype(f32) for p in _split3(x)[:parts]]
        return _dot(jnp.concatenate(pieces, axis=-1).astype(bf16), emat)

    chains = []

    def stage_prepare():
        dirs = ((0, qf_ref, kf_ref, vf_ref, abf_ref), (1, qb_ref, kb_ref, vb_ref, abb_ref))
        for d, q_ref, k_ref, v_ref, ab_ref in dirs:
            ab = ab_ref[0, 0]
            g = -jnp.exp(alog_ref[d]) * _softplus(ab[:, :H] + dtb_ref[d])
            gx = expand(g, 3)
            bx = expand(_sigmoid(ab[:, H:]), 2)
            for c in range(NC):
                rows = slice(C * c, C * (c + 1))
                pre = _delta_chunk(q_ref[0, rows, :], k_ref[0, rows, :], v_ref[0, rows, :], gx[rows], bx[rows], d)
                for qd in range(N_QUADS):
                    sl = slice(QUAD_W * qd, QUAD_W * (qd + 1))
                    chains.append({n: x[:, sl] for n, x in pre.items()})

    def stage_gram():
        for ch in chains:
            kq = _dot_nt(jnp.concatenate([ch["k"], ch["q"]], axis=0).astype(bf16), _block_diag(ch["k"]))
            a_mat = jnp.where(ch["strict"], kq[:C] * ch["bx"] * ch["decay"], 0.0)
            ch["qk"] = kq[C:] * ch["decay"]
            ch["p"] = -a_mat
            ch["t"] = ch["eye"] - a_mat

    def stage_square():
        for ch in chains:
            ch["p"] = _dot(ch["p"].astype(bf16), _block_diag(ch["p"]))

    def stage_double():
        for ch in chains:
            pt = _dot(jnp.concatenate([ch["p"], ch["t"]], axis=0).astype(bf16), _block_diag(ch["p"]))
            ch["p"], ch["t"] = pt[:C], ch["t"] + pt[C:]

    def stage_last():
        for ch in chains:
            ch["tb"] = (ch["t"] + _dot(ch["t"].astype(bf16), _block_diag(ch["p"]))).astype(bf16)

    def stage_solve():
        for ch in chains:
            ch["u"] = _dot(ch["tb"], _block_diag(ch["vb"]))
            ch["w"] = _dot(ch["tb"], _block_diag(ch["kbe"]))

    row_head = lax.broadcasted_iota(jnp.int32, (QUAD_W, QUAD_W), 0) // DN_DK
    col_head = lax.broadcasted_iota(jnp.int32, (QUAD_W, QUAD_W), 1) // DN_DK
    o_refs = (of_ref, ob_ref)
    live = {}

    def scan_chains(step):
        for d in range(2):
            c = step if d == 0 else NC - 1 - step
            for qd in range(N_QUADS):
                yield d, c, qd, (d * NC + c) * N_QUADS + qd

    def scan_first(step):
        for d, c, qd, n in scan_chains(step):
            st = state_ref[d, qd]
            wq = _dot(jnp.concatenate([w_s[slot_r, n], qdec_s[slot_r, n]], axis=0), st.astype(bf16))
            live[n] = (st, u_s[slot_r, n] - wq[:C], wq[C:])

    def scan_second(step):
        for d, c, qd, n in scan_chains(step):
            st, v_new, qs = live.pop(n)
            o = qs + _dot(qk_s[slot_r, n], _block_diag(v_new))
            o_refs[d][0, C * c:C * (c + 1), QUAD_W * qd:QUAD_W * (qd + 1)] = o
            s_new = _dot_tn(kdec_s[slot_r, n], v_new.astype(bf16))
            state_ref[d, qd] = st * cd_s[slot_r, n, 0:1, :] + jnp.where(row_head == col_head, s_new, 0.0)

    local = [stage_prepare, stage_gram, stage_square] + [stage_double] * 4 + [stage_last, stage_solve]
    scan = [f for step in range(NC) for f in (functools.partial(scan_first, step),
                                              functools.partial(scan_second, step))]
    for k in range(max(len(local), len(scan))):
        if k < len(local):
            local[k]()
        if k < len(scan):
            scan[k]()

    for n, ch in enumerate(chains):
        u_s[slot_w, n] = ch["u"]
        w_s[slot_w, n] = ch["w"].astype(bf16)
        qk_s[slot_w, n] = ch["qk"].astype(bf16)
        qdec_s[slot_w, n] = ch["qdec"].astype(bf16)
        kdec_s[slot_w, n] = ch["kdec"].astype(bf16)
        cd_s[slot_w, n] = jnp.broadcast_to(ch["cd"], (8, QUAD_W))


def _delta(qkvn, dab, a_log, dt_bias):
    b, s, _ = qkvn.shape
    H = DN_HEADS
    tb = DN_CHUNK * DELTA_BLOCK_CHUNKS
    nblk = s // tb
    total = b * nblk
    assert s % tb == 0
    d4 = dab.reshape(b, s, 4, H)
    ab = jnp.stack([jnp.concatenate([d4[:, :, r], d4[:, :, 2 + r]], -1) for r in range(2)])
    alog = a_log.astype(f32).reshape(2, 1, H)
    dtb = dt_bias.astype(f32).reshape(2, 1, H)

    def block(lin, rev):
        i = lin % nblk
        return lin // nblk, (nblk - 1 - i) if rev else i

    def sec(n, rev, lag):
        def index(t):
            bi, i = block(jnp.maximum(t - 1, 0) if lag else jnp.minimum(t, total - 1), rev)
            return bi, i, n
        return pl.BlockSpec((1, tb, DN_QK_W), index)

    def ab_spec(d):
        def index(t):
            bi, i = block(jnp.minimum(t, total - 1), d == 1)
            return d, bi, i, 0
        return pl.BlockSpec((1, 1, tb, 2 * H), index)

    chain_buf = lambda dt: pltpu.VMEM((2, DELTA_CHAINS, DN_CHUNK, QUAD_W), dt)
    return pl.pallas_call(
        functools.partial(_delta_kernel, nblk=nblk),
        grid=(total + 1,),
        in_specs=[sec(0, False, False), sec(1, False, False), sec(2, False, False),
                  sec(0, True, False), sec(1, True, False), sec(2, True, False),
                  ab_spec(0), ab_spec(1), _full((2, 1, H)), _full((2, 1, H))],
        out_specs=[sec(0, False, True), sec(0, True, True)],
        out_shape=[jax.ShapeDtypeStruct((b, s, DN_V_W), f32)] * 2,
        scratch_shapes=[pltpu.VMEM((2, N_QUADS, QUAD_W, QUAD_W), f32),
                        chain_buf(f32), chain_buf(bf16), chain_buf(bf16), chain_buf(bf16), chain_buf(bf16),
                        pltpu.VMEM((2, DELTA_CHAINS, 8, QUAD_W), f32)],
        compiler_params=_cparams(("arbitrary",)),
        name="delta",
    )(qkvn, qkvn, qkvn, qkvn, qkvn, qkvn, ab, ab, alog, dtb)


def _pool_band_tables():
    r = np.arange(POOL_TILE)[:, None]
    j = np.arange(POOL_SPAN)[None, :]
    tabs = np.zeros((3, N_POOL_GROUPS, POOL_TILE, POOL_SPAN), np.float32)
    for vi, shift in enumerate((0, 128, 256)):
        rel = j - shift - r
        for gi, w in enumerate(POOL_WINDOWS):
            tabs[vi, gi] = np.abs(rel) <= w // 2
    return jnp.asarray(tabs, dtype=bf16)


def _pool_tile(u_ref, band_ref, w_ref, scale_ref, t, seq):
    nt = seq // POOL_TILE
    t0 = pl.multiple_of(t * POOL_TILE, POOL_TILE)
    ks = pl.multiple_of(jnp.clip(t0 - 128, 0, seq - POOL_SPAN), 128)
    variant = jnp.where(t == 0, 0, jnp.where(t == nt - 1, 2, 1))
    win = u_ref[0, pl.ds(ks, POOL_SPAN), :]
    own = u_ref[0, pl.ds(t0, POOL_TILE), :].astype(f32)
    pos = t0 + lax.broadcasted_iota(jnp.int32, (POOL_TILE, POOL_GROUP), 0)
    outs = []
    for gi, w in enumerate(POOL_WINDOWS):
        r = w // 2
        sl = slice(POOL_GROUP * gi, POOL_GROUP * (gi + 1))
        wsum = _dot(band_ref[variant, gi], win[:, sl])
        count = (jnp.minimum(pos + r, seq - 1) - jnp.maximum(pos - r, 0) + 1).astype(f32)
        mixed = wsum / count - own[:, sl]
        outs.append(_dot(mixed.astype(bf16), w_ref[gi]))
    return jnp.concatenate(outs, -1) * scale_ref[...]


def _route_tile(x, w_t, carry_ref):
    xh, xl = _split2(x)
    wh, wl = _split2(w_t)
    logits = _dot_nt(wh, xh) + (_dot_nt(wh, xl) + _dot_nt(wl, xh))
    tm = logits.shape[1]
    sub = lax.broadcasted_iota(jnp.int32, logits.shape, 0)
    m1 = jnp.max(logits, 0, keepdims=True)
    i1 = jnp.min(jnp.where(logits == m1, sub, N_EXPERTS), 0, keepdims=True)
    rest = jnp.where(sub == i1, -jnp.inf, logits)
    m2 = jnp.max(rest, 0, keepdims=True)
    i2 = jnp.min(jnp.where(rest == m2, sub, N_EXPERTS), 0, keepdims=True)
    e2 = jnp.exp(m2 - m1)
    p1 = 1.0 / (1.0 + e2)
    p2 = e2 / (1.0 + e2)
    oh1 = jnp.where(sub == i1, 1.0, 0.0)
    oh2 = jnp.where(sub == i2, 1.0, 0.0)
    both = oh1 + oh2
    rr = lax.broadcasted_iota(jnp.int32, (tm, tm), 0)
    cc = lax.broadcasted_iota(jnp.int32, (tm, tm), 1)
    earlier = jnp.where(rr < cc, 1.0, 0.0).astype(bf16)
    before = _dot(both.astype(bf16), earlier) + carry_ref[...]
    r1 = jnp.sum(oh1 * before, 0, keepdims=True).astype(jnp.int32)
    r2 = jnp.sum(oh2 * before, 0, keepdims=True).astype(jnp.int32)
    carry_ref[...] += jnp.sum(both, 1, keepdims=True)
    mi = jnp.where(sub == 0, i1, jnp.where(sub == 1, i2, jnp.where(sub == 2, r1, jnp.where(sub == 3, r2, 0))))
    mf = jnp.where(sub == 0, p1, jnp.where(sub == 1, p2, 0.0))
    return mi, mf


def _merge_kernel(oa_ref, odf_ref, odb_ref, z_ref, pu_ref, gate_ref, h_ref, ones_ref, nw_ref, band_ref, pw_ref,
                  ps_ref, wb_ref, wo_ref, g_ref, b_ref, *refs, seq, routed):
    if routed:
        rw_ref, hf_ref, hb_ref, mi_ref, mf_ref, cnt_ref, carry_ref = refs

        @pl.when(pl.program_id(0) == 0)
        def _():
            carry_ref[...] = jnp.zeros(carry_ref.shape, f32)
    else:
        hf_ref, hb_ref = refs
    od = odf_ref[...] + odb_ref[...]
    ms = _dot_split2_rhs(od * od, ones_ref[...]) * (1.0 / DN_DV)
    od = od * lax.rsqrt(ms + RMS_EPS) * nw_ref[...] * _silu(z_ref[...].astype(f32))
    first = (pl.program_id(0) * MERGE_POOL_TILES) % (seq // POOL_TILE)
    op = jnp.concatenate([_pool_tile(pu_ref, band_ref, pw_ref, ps_ref, first + k, seq)
                          for k in range(MERGE_POOL_TILES)], axis=0)
    branches = (oa_ref[...], od.astype(bf16), op.astype(bf16))
    merged = None
    for n in range(N_BRANCH):
        up = _dot(branches[n], wb_ref[n])
        gate = _sigmoid(gate_ref[:, D_MODEL * n:D_MODEL * (n + 1)].astype(f32))
        merged = gate * up if merged is None else merged + gate * up
    y = _dot(merged.astype(bf16), wo_ref[...])
    hn = _layer_norm(DEEPNORM_ALPHA * h_ref[...] + y, g_ref[...], b_ref[...])
    hf_ref[...] = hn
    if routed:
        hb_ref[...] = _pack_bf16_pair(hn[:, :D_MODEL // 2], hn[:, D_MODEL // 2:])
        mi_ref[...], mf_ref[...] = _route_tile(hn, rw_ref[...], carry_ref)
        cnt_ref[...] = jnp.broadcast_to(carry_ref[...], cnt_ref.shape).astype(jnp.int32)
    else:
        hb_ref[...] = hn.astype(bf16)


def _merge(oa, odf, odb, z, pu, gates, h, dn_norm_w, pool_w, pool_scale, w_branch, w_out, ln_g, ln_b, router_w):
    m = h.shape[0]
    bsz, seq, _ = pu.shape
    tm = POOL_TILE * MERGE_POOL_TILES
    nt = seq // tm
    assert seq % tm == 0 and seq >= POOL_SPAN
    routed = router_w is not None
    row = lambda w: pl.BlockSpec((tm, w), lambda i: (i, 0))
    nw = jnp.tile(dn_norm_w.astype(f32), DN_HEADS).reshape(1, DN_V_W)
    in_specs = [row(ATTN_W), row(DN_V_W), row(DN_V_W), row(DN_V_W),
                pl.BlockSpec((1, seq, POOL_W), lambda i: (i // nt, 0, 0)),
                row(N_BRANCH * D_MODEL), row(D_MODEL),
                _full((DN_V_W, DN_V_W)), _full((1, DN_V_W)),
                _full((3, N_POOL_GROUPS, POOL_TILE, POOL_SPAN)), _full((N_POOL_GROUPS, POOL_GROUP, POOL_GROUP)),
                _full((1, POOL_W)),
                _full((N_BRANCH, ATTN_W, D_MODEL)), _full((D_MODEL, D_MODEL)),
                _full((1, D_MODEL)), _full((1, D_MODEL))]
    args = [oa, odf, odb, z, pu, gates, h, _head_ones(), nw, _pool_band_tables(), pool_w.astype(bf16),
            pool_scale.reshape(1, POOL_W).astype(f32), w_branch.astype(bf16), w_out.astype(bf16),
            ln_g.reshape(1, D_MODEL).astype(f32), ln_b.reshape(1, D_MODEL).astype(f32)]
    out_specs = [row(D_MODEL)]
    out_shape = [jax.ShapeDtypeStruct((m, D_MODEL), f32)]
    scratch = []
    if routed:
        col = pl.BlockSpec((N_EXPERTS, tm), lambda i: (0, i))
        in_specs.append(_full((N_EXPERTS, D_MODEL)))
        args.append(router_w.astype(f32).T)
        out_specs += [row(PACKED_W), col, col, _full((N_EXPERTS, LANES))]
        out_shape += [jax.ShapeDtypeStruct((m, PACKED_W), jnp.uint32),
                      jax.ShapeDtypeStruct((N_EXPERTS, m), jnp.int32),
                      jax.ShapeDtypeStruct((N_EXPERTS, m), f32),
                      jax.ShapeDtypeStruct((N_EXPERTS, LANES), jnp.int32)]
        scratch.append(pltpu.VMEM((N_EXPERTS, 1), f32))
    else:
        out_specs.append(row(D_MODEL))
        out_shape.append(jax.ShapeDtypeStruct((m, D_MODEL), bf16))
    return pl.pallas_call(
        functools.partial(_merge_kernel, seq=seq, routed=routed),
        grid=(m // tm,),
        in_specs=in_specs,
        out_specs=out_specs,
        out_shape=out_shape,
        scratch_shapes=scratch,
        compiler_params=_cparams(("arbitrary",)),
        name="merge",
    )(*args)


def _swiglu_chunk(x, wg, wu, wd):
    g = _dot(x, wg.astype(bf16))
    u = _dot(x, wu.astype(bf16))
    return _dot((_silu(g) * u).astype(bf16), wd.astype(bf16))


def _ffn_kernel(x_ref, h_ref, wg_ref, wu_ref, wd_ref, g_ref, b_ref, hf_ref, hb_ref, acc_ref):
    acc_ref[...] = _swiglu_chunk(x_ref[...], wg_ref[0], wu_ref[0], wd_ref[0])

    def chunk(j, carry):
        acc_ref[...] += _swiglu_chunk(x_ref[...], wg_ref[j], wu_ref[j], wd_ref[j])
        return carry

    lax.fori_loop(1, wg_ref.shape[0], chunk, 0)
    hn = _layer_norm(DEEPNORM_ALPHA * h_ref[...] + acc_ref[...], g_ref[...], b_ref[...])
    hf_ref[...] = hn
    hb_ref[...] = hn.astype(bf16)


def _ffn(hb, hf, w1, w3, w2, ln_g, ln_b, tm=1024, tf=256):
    m = hb.shape[0]
    tm = min(tm, m)
    ff = w1.shape[-1]
    nj = ff // tf
    row = pl.BlockSpec((tm, D_MODEL), lambda i: (i, 0))
    resident = lambda shape: pl.BlockSpec(shape, lambda i: (0,) * len(shape), pipeline_mode=pl.Buffered(1))
    up_chunks = lambda w: jnp.transpose(w.astype(bf16).reshape(D_MODEL, nj, tf), (1, 0, 2))
    return pl.pallas_call(
        _ffn_kernel,
        grid=(m // tm,),
        in_specs=[row, row, resident((nj, D_MODEL, tf)), resident((nj, D_MODEL, tf)), resident((nj, tf, D_MODEL)),
                  _full((1, D_MODEL)), _full((1, D_MODEL))],
        out_specs=[row, row],
        out_shape=[jax.ShapeDtypeStruct((m, D_MODEL), f32), jax.ShapeDtypeStruct((m, D_MODEL), bf16)],
        scratch_shapes=[pltpu.VMEM((tm, D_MODEL), f32)],
        compiler_params=_cparams(("parallel",)),
        name="ffn",
    )(hb, hf, up_chunks(w1), up_chunks(w3), w2.astype(bf16).reshape(nj, tf, D_MODEL),
      ln_g.reshape(1, D_MODEL).astype(f32), ln_b.reshape(1, D_MODEL).astype(f32))


MOE_TILE = 1024
PACKED_W = D_MODEL // 2
SC_CORES = 2
SC_SUBCORES = 16
SC_WORKERS = SC_CORES * SC_SUBCORES
SC_ROWS = 64


def _gmm_kernel(te_ref, tv_ref, x_ref, wg_ref, wu_ref, wd_ref, o_ref, acc_ref):
    i = pl.program_id(0)
    j = pl.program_id(1)
    valid = tv_ref[i]

    @pl.when(j == 0)
    def _():
        acc_ref[...] = jnp.zeros(acc_ref.shape, f32)

    @pl.when(valid > 0)
    def _():
        rows = lax.broadcasted_iota(jnp.int32, acc_ref.shape, 0)
        x = jnp.where(rows < valid, _unpack_bf16_pair(x_ref[...]), 0.0).astype(bf16)
        acc_ref[...] += _swiglu_chunk(x, wg_ref[0], wu_ref[0], wd_ref[0])

    @pl.when(j == pl.num_programs(1) - 1)
    def _():
        o_ref[...] = _pack_bf16_pair(acc_ref[:, :PACKED_W], acc_ref[:, PACKED_W:])


def _gmm(xs, tile_expert, tile_valid, wg, wu, wd, tm, tf=512):
    n = xs.shape[0]
    ff = wg.shape[-1]
    nj = ff // tf

    def fcol(i, j, te, tv):
        return jnp.where(tv[i] > 0, j, nj - 1)

    grid_spec = pltpu.PrefetchScalarGridSpec(
        num_scalar_prefetch=2,
        grid=(n // tm, nj),
        in_specs=[pl.BlockSpec((tm, PACKED_W), lambda i, j, te, tv: (i, 0)),
                  pl.BlockSpec((1, D_MODEL, tf), lambda i, j, te, tv: (te[i], 0, fcol(i, j, te, tv))),
                  pl.BlockSpec((1, D_MODEL, tf), lambda i, j, te, tv: (te[i], 0, fcol(i, j, te, tv))),
                  pl.BlockSpec((1, tf, D_MODEL), lambda i, j, te, tv: (te[i], fcol(i, j, te, tv), 0))],
        out_specs=pl.BlockSpec((tm, PACKED_W), lambda i, j, te, tv: (i, 0)),
        scratch_shapes=[pltpu.VMEM((tm, D_MODEL), f32)])
    return pl.pallas_call(
        _gmm_kernel,
        grid_spec=grid_spec,
        out_shape=jax.ShapeDtypeStruct((n, PACKED_W), jnp.uint32),
        compiler_params=_cparams(("parallel", "arbitrary")),
        name="moe_gmm",
    )(tile_expert, tile_valid, xs, wg, wu, wd)


def _sc_mesh():
    return plsc.VectorSubcoreMesh(core_axis_name="c", subcore_axis_name="s",
                                  num_cores=SC_CORES, num_subcores=SC_SUBCORES)


def _sc_worker_base(per_worker):
    return (lax.axis_index("s") * SC_CORES + lax.axis_index("c")) * per_worker


def _sc_scratch(d, dtype):
    return [pltpu.VMEM((SC_ROWS,), jnp.int32), pltpu.VMEM((SC_ROWS,), jnp.int32), pltpu.VMEM((2, SC_ROWS, d), dtype),
            pltpu.SemaphoreType.DMA((2,)), pltpu.SemaphoreType.DMA((2,))]


def _sc_scatter_rows(src, idx0, idx1, n_out):
    m, d = src.shape
    per_worker = m // SC_WORKERS
    n_chunks = per_worker // SC_ROWS
    assert m % (SC_WORKERS * SC_ROWS * 2) == 0

    def body(src_hbm, idx0_hbm, idx1_hbm, out_hbm, idx_a, idx_b, rows_v, sem_in, sem_out):
        base = _sc_worker_base(per_worker)
        idx_v = (idx_a, idx_b)

        def load(j, slot):
            off = pl.multiple_of(base + j * SC_ROWS, SC_ROWS)
            return pltpu.make_async_copy(src_hbm.at[pl.ds(off, SC_ROWS)], rows_v.at[slot], sem_in.at[slot])

        def store(j, slot, idx_hbm):
            off = pl.multiple_of(base + j * SC_ROWS, SC_ROWS)
            pltpu.sync_copy(idx_hbm.at[pl.ds(off, SC_ROWS)], idx_v[slot])
            pltpu.async_copy(rows_v.at[slot], out_hbm.at[idx_v[slot]], sem_out.at[slot]).wait()

        load(0, 0).start()

        @pl.loop(0, n_chunks, step=2)
        def _(j):
            for slot in range(2):
                @pl.when(j + slot + 1 < n_chunks)
                def _():
                    load(j + slot + 1, 1 - slot).start()
                load(j + slot, slot).wait()
                store(j + slot, slot, idx0_hbm)
                store(j + slot, slot, idx1_hbm)

    return pl.kernel(
        body, out_type=jax.ShapeDtypeStruct((n_out, d), src.dtype), mesh=_sc_mesh(),
        scratch_types=_sc_scratch(d, src.dtype), name="moe_dispatch",
    )(src, idx0, idx1)


def _sc_gather_rows(table, idx):
    n = idx.shape[0]
    d = table.shape[1]
    per_worker = n // SC_WORKERS
    n_chunks = per_worker // SC_ROWS
    assert n % (SC_WORKERS * SC_ROWS * 2) == 0

    def body(table_hbm, idx_hbm, out_hbm, idx_a, idx_b, rows_v, sem_in, sem_out):
        base = _sc_worker_base(per_worker)
        idx_v = (idx_a, idx_b)

        def gather(j, slot):
            off = pl.multiple_of(base + j * SC_ROWS, SC_ROWS)
            pltpu.sync_copy(idx_hbm.at[pl.ds(off, SC_ROWS)], idx_v[slot])
            return pltpu.make_async_copy(table_hbm.at[idx_v[slot]], rows_v.at[slot], sem_in.at[slot])

        def wait_gather(slot):
            pltpu.make_async_copy(table_hbm.at[idx_v[slot]], rows_v.at[slot], sem_in.at[slot]).wait()

        def write(j, slot):
            off = pl.multiple_of(base + j * SC_ROWS, SC_ROWS)
            return pltpu.make_async_copy(rows_v.at[slot], out_hbm.at[pl.ds(off, SC_ROWS)], sem_out.at[slot])

        gather(0, 0).start()

        @pl.loop(0, n_chunks, step=2)
        def _(j):
            for slot in range(2):
                @pl.when(j + slot + 1 < n_chunks)
                def _():
                    @pl.when(j + slot >= 1)
                    def _():
                        write(j + slot - 1, 1 - slot).wait()
                    gather(j + slot + 1, 1 - slot).start()
                wait_gather(slot)
                write(j + slot, slot).start()

        write(n_chunks - 2, 0).wait()
        write(n_chunks - 1, 1).wait()

    return pl.kernel(
        body, out_type=jax.ShapeDtypeStruct((n, d), table.dtype), mesh=_sc_mesh(),
        scratch_types=_sc_scratch(d, table.dtype), name="moe_collect",
    )(table, idx)


def _combine_kernel(y0_ref, y1_ref, mf_ref, h_ref, g_ref, b_ref, o_ref):
    p = mf_ref[...]
    y = p[:, 0:1] * _unpack_bf16_pair(y0_ref[0]) + p[:, 1:2] * _unpack_bf16_pair(y1_ref[0])
    o_ref[...] = _layer_norm(DEEPNORM_ALPHA * h_ref[...] + y, g_ref[...], b_ref[...])


def _combine(yg, mf, hf, ln_g, ln_b, tm=512):
    m = hf.shape[0]
    tm = min(tm, m)
    return pl.pallas_call(
        _combine_kernel,
        grid=(m // tm,),
        in_specs=[pl.BlockSpec((1, tm, PACKED_W), lambda i: (0, i, 0)),
                  pl.BlockSpec((1, tm, PACKED_W), lambda i: (1, i, 0)),
                  pl.BlockSpec((tm, N_EXPERTS), lambda i: (i, 0)),
                  pl.BlockSpec((tm, D_MODEL), lambda i: (i, 0)),
                  _full((1, D_MODEL)), _full((1, D_MODEL))],
        out_specs=pl.BlockSpec((tm, D_MODEL), lambda i: (i, 0)),
        out_shape=jax.ShapeDtypeStruct((m, D_MODEL), f32),
        compiler_params=_cparams(("parallel",)),
        name="moe_combine",
    )(yg, yg, mf, hf, ln_g.reshape(1, D_MODEL).astype(f32), ln_b.reshape(1, D_MODEL).astype(f32))


def _moe(hf, hp, mi, mf, cnt, wg, wu, wd, ln_g, ln_b):
    m = hf.shape[0]
    tm = MOE_TILE
    counts = cnt[:, 0]
    padded = (counts + tm - 1) // tm * tm
    ends = jnp.cumsum(padded)
    starts = ends - padded
    chosen = mi[0:TOP_K, :, None] == jnp.arange(N_EXPERTS, dtype=jnp.int32)
    pos = jnp.sum(jnp.where(chosen, starts, 0), -1) + mi[TOP_K:2 * TOP_K]
    n_tiles = TOP_K * m // tm + N_EXPERTS
    tile_start = jnp.arange(n_tiles, dtype=jnp.int32) * tm
    tile_expert = jnp.minimum(jnp.sum(tile_start[:, None] >= ends[None, :], -1), N_EXPERTS - 1).astype(jnp.int32)
    tile_valid = jnp.clip(starts[tile_expert] + counts[tile_expert] - tile_start, 0, tm).astype(jnp.int32)
    tile_valid = jnp.where(tile_start < ends[-1], tile_valid, 0)
    xs = _sc_scatter_rows(hp, pos[0], pos[1], n_tiles * tm)
    ys = _gmm(xs, tile_expert, tile_valid, wg, wu, wd, tm)
    yg = _sc_gather_rows(ys, pos.reshape(TOP_K * m))
    return _combine(yg.reshape(TOP_K, m, PACKED_W), mf.T, hf, ln_g, ln_b)


_SPLITS = (ATTN_W, KV_W, KV_W, 3 * DN_QK_W, DN_V_W, 4 * DN_HEADS, POOL_W, N_BRANCH * D_MODEL)
_SPLIT_DTYPES = (bf16, bf16, f32, bf16, f32, bf16, bf16)


def _mixer(hf, hb, bsz, seq, bias_tabs, w_in, sink, conv_w, a_log, dt_bias, dn_norm_w, pool_w, pool_scale,
           w_branch, w_out, ln_g, ln_b, router_w):
    points = np.cumsum(_SPLITS)[:-1].tolist()
    wq, wk, wv, *rest = jnp.split(w_in.astype(bf16), points, axis=-1)
    wkv = jnp.concatenate([wk[:, HEAD_DIM * g:HEAD_DIM * (g + 1)] for g in range(ATTN_KV_HEADS) for _ in range(2)]
                          + [wv], axis=-1)
    aq, akv, qkvn, dz, dab, pu, gates = _inproj(hf if hb is None else hb, [wq, wkv] + rest, _SPLIT_DTYPES,
                                                conv_w, seq, dn=2)
    shp = lambda t: t.reshape(bsz, seq, t.shape[-1])
    oa = _attention(shp(aq), shp(akv), bias_tabs, sink)
    odf, odb = _delta(shp(qkvn), shp(dab), a_log, dt_bias)
    m = bsz * seq
    return _merge(oa.reshape(m, ATTN_W), odf.reshape(m, DN_V_W), odb.reshape(m, DN_V_W), dz, shp(pu), gates, hf,
                  dn_norm_w, pool_w, pool_scale, w_branch, w_out, ln_g, ln_b, router_w)


def kernel(x, w_in, attn_sink, rel_bias, conv_w, a_log, dt_bias, dn_norm_w, pool_w, pool_scale, w_branch, w_out,
           ln1_g, ln1_b, ln2_g, ln2_b, ffn_w1, ffn_w3, ffn_w2, router_w, moe_wg, moe_wu, moe_wd):
    bsz, seq, _ = x.shape
    m = bsz * seq
    bias_tabs = _attn_bias_tables(rel_bias)
    hf = x.reshape(m, D_MODEL).astype(f32)
    hb = None
    for layer in range(DEPTH):
        dense = layer % 2 == 0
        i = layer // 2
        hf, *rest = _mixer(hf, hb, bsz, seq, bias_tabs, w_in[layer], attn_sink[layer], conv_w[layer], a_log[layer],
                           dt_bias[layer], dn_norm_w[layer], pool_w[layer], pool_scale[layer], w_branch[layer],
                           w_out[layer], ln1_g[layer], ln1_b[layer], None if dense else router_w[i])
        if dense:
            hf, hb = _ffn(rest[0], hf, ffn_w1[i], ffn_w3[i], ffn_w2[i], ln2_g[layer], ln2_b[layer])
        else:
            hf = _moe(hf, *rest, moe_wg[i], moe_wu[i], moe_wd[i], ln2_g[layer], ln2_b[layer])
            hb = None
    return hf.reshape(bsz, seq, D_MODEL).astype(x.dtype)
```

```python
import functools

import numpy as np
import jax
import jax.numpy as jnp
from jax import lax
from jax.experimental import pallas as pl
from jax.experimental.pallas import tpu as pltpu
from jax.experimental.pallas import tpu_sc as plsc

f32 = jnp.float32
bf16 = jnp.bfloat16

D_MODEL = 1024
DEPTH = 2
HEAD_DIM = 64
ATTN_Q_HEADS = 8
ATTN_KV_HEADS = 2
ATTN_GROUP = ATTN_Q_HEADS // ATTN_KV_HEADS
WINDOW = 128
BLOCK = 128
SPAN = BLOCK + 2 * WINDOW
N_BUCKETS = 32
MAX_DISTANCE = 128
DN_HEADS = 8
DN_DK = 64
DN_DV = 64
DN_CONV = 5
DN_CHUNK = 64
POOL_WINDOWS = (2, 4, 8, 16)
N_POOL_GROUPS = 4
POOL_GROUP = 128
ATTN_W = ATTN_Q_HEADS * HEAD_DIM
KV_W = ATTN_KV_HEADS * HEAD_DIM
DN_QK_W = DN_HEADS * DN_DK
DN_V_W = DN_HEADS * DN_DV
POOL_W = N_POOL_GROUPS * POOL_GROUP
N_BRANCH = 3
N_EXPERTS = 8
TOP_K = 2
DEEPNORM_ALPHA = (2.0 * DEPTH) ** 0.25
LN_EPS = 1e-5
RMS_EPS = 1e-6
NEG_INF = -1e30

VMEM_LIMIT_BYTES = 56 * 1024 * 1024
LANES = 128
POOL_TILE = 256
POOL_SPAN = 512
MERGE_POOL_TILES = 2


def _cparams(sem):
    return pltpu.CompilerParams(dimension_semantics=sem, vmem_limit_bytes=VMEM_LIMIT_BYTES)


def _full(shape):
    n = len(shape)
    return pl.BlockSpec(shape, lambda *_: (0,) * n)


def _sigmoid(x):
    return 0.5 * jnp.tanh(0.5 * x) + 0.5


def _silu(x):
    return x * _sigmoid(x)


def _layer_norm(x, g, b):
    mu = jnp.mean(x, -1, keepdims=True)
    xc = x - mu
    var = jnp.mean(xc * xc, -1, keepdims=True)
    return xc * lax.rsqrt(var + LN_EPS) * g + b


def _dot(a, b):
    return jnp.dot(a, b, preferred_element_type=f32)


def _dot_nt(a, b):
    return lax.dot_general(a, b, (((1,), (1,)), ((), ())), preferred_element_type=f32)


def _dot_tn(a, b):
    return lax.dot_general(a, b, (((0,), (0,)), ((), ())), preferred_element_type=f32)


def _split3(x):
    hi = x.astype(bf16)
    r = x - hi.astype(f32)
    mid = r.astype(bf16)
    lo = (r - mid.astype(f32)).astype(bf16)
    return hi, mid, lo


def _split2(x):
    hi = x.astype(bf16)
    return hi, (x - hi.astype(f32)).astype(bf16)


def _dot_split2_rhs(x, b_bf):
    hi, lo = _split2(x)
    return _dot(hi, b_bf) + _dot(lo, b_bf)


def _pack_bf16_pair(a, b):
    def rounded(x):
        bits = lax.bitcast_convert_type(x, jnp.uint32)
        return bits + jnp.uint32(0x7FFF) + ((bits >> 16) & jnp.uint32(1))
    return (rounded(b) & jnp.uint32(0xFFFF0000)) | (rounded(a) >> 16)


def _unpack_bf16_pair(p):
    lo = lax.bitcast_convert_type(p << 16, f32)
    hi = lax.bitcast_convert_type(p & jnp.uint32(0xFFFF0000), f32)
    return jnp.concatenate([lo, hi], axis=-1)


INPROJ_TILE = 512
DN_HALO = 16


def _head_ones():
    blk = np.arange(DN_QK_W) // DN_DK
    return jnp.asarray(blk[:, None] == blk[None, :], dtype=bf16)


def _inproj_kernel(x_ref, xp_ref, xn_ref, cw_ref, ones_ref, *refs, nt, dn):
    n = len(refs) // 2
    w_refs, o_refs = refs[:n], refs[n:]
    x = x_ref[...].astype(bf16)
    t = pl.program_id(0) % nt
    x_ext = jnp.concatenate([xp_ref[...].astype(bf16), x, xn_ref[...].astype(bf16)], axis=0)
    rows = x_ext.shape[0]
    row = lax.broadcasted_iota(jnp.int32, (rows, DN_QK_W), 0)
    first = jnp.where(t == 0, DN_HALO, 0)
    end = jnp.where(t == nt - 1, rows - DN_HALO, rows)
    inside = (row >= first) & (row < end)
    u = _dot(x_ext, w_refs[dn][...])
    jobs = [(idx, c0) for idx in range(n) if idx != dn for c0 in range(0, o_refs[idx].shape[1], D_MODEL)]
    jobs.sort(key=lambda job: -min(D_MODEL, o_refs[job[0]].shape[1] - job[1]))
    for sec in range(3):
        for idx, c0 in jobs[sec::3]:
            c1 = min(c0 + D_MODEL, o_refs[idx].shape[1])
            o_refs[idx][:, c0:c1] = _dot(x, w_refs[idx][:, c0:c1]).astype(o_refs[idx].dtype)
        cols = slice(DN_QK_W * sec, DN_QK_W * (sec + 1))
        us = jnp.where(inside, u[:, cols], 0.0)
        z = [us * cw_ref[k:k + 1, cols] for k in range(DN_CONV)]
        after = pltpu.roll(z[3] + pltpu.roll(z[4], rows - 1, axis=0), rows - 1, axis=0)
        before = pltpu.roll(z[1] + pltpu.roll(z[0], 1, axis=0), 1, axis=0)
        y = _silu((z[2] + after + before)[DN_HALO:rows - DN_HALO])
        if sec < 2:
            ss = _dot((y * y).astype(bf16), ones_ref[...])
            y = y * lax.rsqrt(ss + 1e-6) * (DN_DK ** -0.5 if sec == 0 else 1.0)
        o_refs[dn][:, cols] = y


def _inproj(x, ws, dtypes, conv_w, seq, dn):
    m = x.shape[0]
    tm = INPROJ_TILE
    halo_blocks = tm // DN_HALO
    assert seq % tm == 0 and tm % DN_HALO == 0 and DN_CONV == 5
    last = m // DN_HALO - 1
    return pl.pallas_call(
        functools.partial(_inproj_kernel, nt=seq // tm, dn=dn),
        grid=(m // tm,),
        in_specs=[pl.BlockSpec((tm, D_MODEL), lambda i: (i, 0)),
                  pl.BlockSpec((DN_HALO, D_MODEL), lambda i: (jnp.maximum(i * halo_blocks - 1, 0), 0)),
                  pl.BlockSpec((DN_HALO, D_MODEL), lambda i: (jnp.minimum((i + 1) * halo_blocks, last), 0)),
                  _full(conv_w.shape), _full((DN_QK_W, DN_QK_W))]
        + [pl.BlockSpec(w.shape, lambda i: (0, 0), pipeline_mode=pl.Buffered(1)) for w in ws],
        out_specs=[pl.BlockSpec((tm, w.shape[1]), lambda i: (i, 0)) for w in ws],
        out_shape=[jax.ShapeDtypeStruct((m, w.shape[1]), dt) for w, dt in zip(ws, dtypes)],
        compiler_params=_cparams(("parallel",)),
        name="inproj",
    )(x, x, x, conv_w.astype(f32), _head_ones(), *ws)


def _t5_bucket(rel):
    nb = N_BUCKETS // 2
    max_exact = nb // 2
    n = np.abs(rel)
    large = max_exact + (np.log(np.maximum(n, 1) / max_exact) / np.log(MAX_DISTANCE / max_exact)
                         * (nb - max_exact)).astype(np.int32)
    large = np.minimum(large, nb - 1)
    return (np.where(rel > 0, nb, 0) + np.where(n < max_exact, n, large)).astype(np.int32)


ATTN_KV_COLS = 2 * KV_W + KV_W


def _bias_bucket_tables():
    j = np.arange(SPAN)[:, None]
    r = np.arange(BLOCK)[None, :]
    tabs = []
    for shift in (0, WINDOW, 2 * WINDOW):
        rel = j - shift - r
        tabs.append(np.where(np.abs(rel) <= WINDOW, _t5_bucket(rel), N_BUCKETS))
    return jnp.asarray(np.stack(tabs), dtype=jnp.int32)


def _bias_table_kernel(bucket_ref, rb_ref, o_ref):
    bucket = bucket_ref[0]
    for h in range(ATTN_Q_HEADS):
        acc = jnp.full(bucket.shape, NEG_INF, f32)
        for b in range(N_BUCKETS):
            acc = jnp.where(bucket == b, rb_ref[b, h], acc)
        o_ref[0, h] = acc


def _attn_bias_tables(rel_bias):
    return pl.pallas_call(
        _bias_table_kernel,
        grid=(3,),
        in_specs=[pl.BlockSpec((1, SPAN, BLOCK), lambda v: (v, 0, 0)),
                  pl.BlockSpec(memory_space=pltpu.SMEM)],
        out_specs=pl.BlockSpec((1, ATTN_Q_HEADS, SPAN, BLOCK), lambda v: (v, 0, 0, 0)),
        out_shape=jax.ShapeDtypeStruct((3, ATTN_Q_HEADS, SPAN, BLOCK), f32),
        compiler_params=_cparams(("parallel",)),
        name="attn_bias",
    )(_bias_bucket_tables(), rel_bias.astype(f32))


ATTN_STEP_BLOCKS = 8


def _attn_kernel(q_ref, kv_ref, bias_ref, sink_ref, o_ref, *, seq):
    nb = seq // BLOCK
    scale = HEAD_DIM ** -0.5
    pair_w = 2 * HEAD_DIM
    low = lax.broadcasted_iota(jnp.int32, (BLOCK, pair_w), 1) < HEAD_DIM
    blocks = []
    for sb in range(ATTN_STEP_BLOCKS):
        i = pl.program_id(1) * ATTN_STEP_BLOCKS + sb
        ks = pl.multiple_of(jnp.clip(i * BLOCK - WINDOW, 0, seq - SPAN), BLOCK)
        blocks.append(dict(variant=jnp.where(i == 0, 0, jnp.where(i == nb - 1, 2, 1)),
                           q=q_ref[0, BLOCK * sb:BLOCK * (sb + 1), :], kv=kv_ref[0, pl.ds(ks, SPAN), :]))
    pairs = [(blk, g) for blk in blocks for g in range(ATTN_KV_HEADS)]
    for blk in blocks:
        blk["v_t"] = blk["kv"][:, 2 * KV_W:].astype(f32).T.astype(bf16)
        blk["o_t"] = []
    s_t = []
    for blk, g in pairs:
        kk = blk["kv"][:, pair_w * g:pair_w * (g + 1)]
        qm = []
        for hl in range(ATTN_GROUP):
            h = ATTN_GROUP * g + hl
            qp = blk["q"][:, pair_w * (h // 2):pair_w * (h // 2 + 1)]
            qm.append(jnp.where(low if h % 2 == 0 else jnp.logical_not(low), qp, jnp.zeros_like(qp)))
        s_t.append(_dot_nt(kk, jnp.concatenate(qm, axis=0)))
    p_t = []
    for (blk, g), st in zip(pairs, s_t):
        p_n = []
        for hl in range(ATTN_GROUP):
            h = ATTN_GROUP * g + hl
            s = st[:, BLOCK * hl:BLOCK * (hl + 1)] * scale + bias_ref[blk["variant"], h]
            sk = sink_ref[0:1, h:h + 1]
            m = jnp.maximum(jnp.max(s, 0, keepdims=True), sk)
            p = jnp.exp(s - m)
            denom = jnp.sum(p, 0, keepdims=True) + jnp.exp(sk - m)
            p_n.append((p * (1.0 / denom)).astype(bf16))
        p_t.append(jnp.concatenate(p_n, axis=1))
    for (blk, g), pt in zip(pairs, p_t):
        og = _dot(blk["v_t"][HEAD_DIM * g:HEAD_DIM * (g + 1), :], pt)
        blk["o_t"] += [og[:, BLOCK * hl:BLOCK * (hl + 1)] for hl in range(ATTN_GROUP)]
    for sb, blk in enumerate(blocks):
        o = jnp.concatenate(blk["o_t"], axis=0).T
        o_ref[0, BLOCK * sb:BLOCK * (sb + 1), :] = o.astype(o_ref.dtype)


def _attention(aq, akv, bias_tabs, sink):
    b, s, _ = aq.shape
    tq = BLOCK * ATTN_STEP_BLOCKS
    assert s % tq == 0 and s >= SPAN
    return pl.pallas_call(
        functools.partial(_attn_kernel, seq=s),
        grid=(b, s // tq),
        in_specs=[pl.BlockSpec((1, tq, ATTN_W), lambda bi, i: (bi, i, 0)),
                  pl.BlockSpec((1, s, ATTN_KV_COLS), lambda bi, i: (bi, 0, 0)),
                  _full(bias_tabs.shape),
                  _full((1, ATTN_Q_HEADS))],
        out_specs=pl.BlockSpec((1, tq, ATTN_W), lambda bi, i: (bi, i, 0)),
        out_shape=jax.ShapeDtypeStruct((b, s, ATTN_W), bf16),
        compiler_params=_cparams(("parallel", "arbitrary")),
        name="attn",
    )(aq, akv, bias_tabs, sink.reshape(1, ATTN_Q_HEADS).astype(f32))


def _softplus(x):
    return jnp.maximum(x, 0.0) + jnp.log1p(jnp.exp(-jnp.abs(x)))


QUAD = 4
QUAD_W = QUAD * DN_DK
N_QUADS = DN_HEADS // QUAD
DELTA_BLOCK_CHUNKS = 4


def _block_diag(xq):
    lane_head = lax.broadcasted_iota(jnp.int32, xq.shape, 1) // DN_DK
    return jnp.concatenate([jnp.where(lane_head == h, xq, 0.0).astype(bf16) for h in range(QUAD)], axis=0)


def _delta_chunk(q, k, v, gx, bx, d):
    C = DN_CHUNK
    W = DN_QK_W
    sign = 1 - 2 * d
    ii = lax.broadcasted_iota(jnp.int32, (C, W), 0)
    jl = lax.broadcasted_iota(jnp.int32, (C, W), 1) % DN_DK
    order = (ii - jl) * sign
    incl = order >= 0
    strict = order > 0
    rr = lax.broadcasted_iota(jnp.int32, (2 * C, 2 * C), 0)
    cc = lax.broadcasted_iota(jnp.int32, (2 * C, 2 * C), 1)
    tri = jnp.where(((rr % C) - cc) * sign >= 0, 1.0, 0.0)
    lmat = jnp.where(cc < C, tri, jnp.where(rr < C, -1.0, 0.0)).astype(bf16)
    rmat = jnp.concatenate([gx, jnp.where(order <= 0, gx, 0.0)], axis=0)
    dg = _dot(jnp.concatenate([lmat] * 2, axis=1), jnp.concatenate(_split2(rmat), axis=0))
    dmat, gc = dg[:C], dg[C:]
    decay = jnp.where(incl, jnp.exp(jnp.where(incl, dmat, 0.0)), 0.0)
    eg = jnp.exp(gc)
    last = C - 1 if d == 0 else 0
    gl = gc[last:last + 1, :]
    kdec = k * jnp.exp(gl - gc)
    cd = jnp.exp(gl)
    kb = k * bx
    eye = jnp.where(ii == jl, 1.0, 0.0)
    return dict(q=q, k=k, bx=bx, decay=decay, strict=strict, eye=eye, kdec=kdec, cd=cd,
                vb=v * bx, kbe=kb * eg, qdec=q * eg)


DELTA_CHAINS = 2 * DELTA_BLOCK_CHUNKS * N_QUADS


def _delta_kernel(qf_ref, kf_ref, vf_ref, qb_ref, kb_ref, vb_ref, abf_ref, abb_ref, alog_ref, dtb_ref,
                  of_ref, ob_ref, state_ref, u_s, w_s, qk_s, qdec_s, kdec_s, cd_s, *, nblk):
    C, H, NC = DN_CHUNK, DN_HEADS, DELTA_BLOCK_CHUNKS
    t = pl.program_id(0)
    slot_w = t % 2
    slot_r = 1 - slot_w

    @pl.when(t == 0)
    def _():
        for ref in (u_s, w_s, qk_s, qdec_s, kdec_s, cd_s):
            ref[1] = jnp.zeros(ref.shape[1:], ref.dtype)

    @pl.when((t + nblk - 1) % nblk == 0)
    def _():
        state_ref[...] = jnp.zeros(state_ref.shape, f32)

    def expand(x, parts):
        rows = lax.broadcasted_iota(jnp.int32, (H * parts, DN_QK_W), 0) % H
        cols = lax.broadcasted_iota(jnp.int32, (H * parts, DN_QK_W), 1) // DN_DK
        emat = jnp.where(rows == cols, 1.0, 0.0).astype(bf16)
        pieces = [p.astype(f32) for p in _split3(x)[:parts]]
        return _dot(jnp.concatenate(pieces, axis=-1).astype(bf16), emat)

    chains = []

    def stage_prepare():
        dirs = ((0, qf_ref, kf_ref, vf_ref, abf_ref), (1, qb_ref, kb_ref, vb_ref, abb_ref))
        for d, q_ref, k_ref, v_ref, ab_ref in dirs:
            ab = ab_ref[0, 0]
            g = -jnp.exp(alog_ref[d]) * _softplus(ab[:, :H] + dtb_ref[d])
            gx = expand(g, 3)
            bx = expand(_sigmoid(ab[:, H:]), 2)
            for c in range(NC):
                rows = slice(C * c, C * (c + 1))
                pre = _delta_chunk(q_ref[0, rows, :], k_ref[0, rows, :], v_ref[0, rows, :], gx[rows], bx[rows], d)
                for qd in range(N_QUADS):
                    sl = slice(QUAD_W * qd, QUAD_W * (qd + 1))
                    chains.append({n: x[:, sl] for n, x in pre.items()})

    def stage_gram():
        for ch in chains:
            kq = _dot_nt(jnp.concatenate([ch["k"], ch["q"]], axis=0).astype(bf16), _block_diag(ch["k"]))
            a_mat = jnp.where(ch["strict"], kq[:C] * ch["bx"] * ch["decay"], 0.0)
            ch["qk"] = kq[C:] * ch["decay"]
            ch["p"] = -a_mat
            ch["t"] = ch["eye"] - a_mat

    def stage_square():
        for ch in chains:
            ch["p"] = _dot(ch["p"].astype(bf16), _block_diag(ch["p"]))

    def stage_double():
        for ch in chains:
            pt = _dot(jnp.concatenate([ch["p"], ch["t"]], axis=0).astype(bf16), _block_diag(ch["p"]))
            ch["p"], ch["t"] = pt[:C], ch["t"] + pt[C:]

    def stage_last():
        for ch in chains:
            ch["tb"] = (ch["t"] + _dot(ch["t"].astype(bf16), _block_diag(ch["p"]))).astype(bf16)

    def stage_solve():
        for ch in chains:
            ch["u"] = _dot(ch["tb"], _block_diag(ch["vb"]))
            ch["w"] = _dot(ch["tb"], _block_diag(ch["kbe"]))

    row_head = lax.broadcasted_iota(jnp.int32, (QUAD_W, QUAD_W), 0) // DN_DK
    col_head = lax.broadcasted_iota(jnp.int32, (QUAD_W, QUAD_W), 1) // DN_DK
    o_refs = (of_ref, ob_ref)
    live = {}

    def scan_chains(step):
        for d in range(2):
            c = step if d == 0 else NC - 1 - step
            for qd in range(N_QUADS):
                yield d, c, qd, (d * NC + c) * N_QUADS + qd

    def scan_first(step):
        for d, c, qd, n in scan_chains(step):
            st = state_ref[d, qd]
            wq = _dot(jnp.concatenate([w_s[slot_r, n], qdec_s[slot_r, n]], axis=0), st.astype(bf16))
            live[n] = (st, u_s[slot_r, n] - wq[:C], wq[C:])

    def scan_second(step):
        for d, c, qd, n in scan_chains(step):
            st, v_new, qs = live.pop(n)
            o = qs + _dot(qk_s[slot_r, n], _block_diag(v_new))
            o_refs[d][0, C * c:C * (c + 1), QUAD_W * qd:QUAD_W * (qd + 1)] = o
            s_new = _dot_tn(kdec_s[slot_r, n], v_new.astype(bf16))
            state_ref[d, qd] = st * cd_s[slot_r, n, 0:1, :] + jnp.where(row_head == col_head, s_new, 0.0)

    local = [stage_prepare, stage_gram, stage_square] + [stage_double] * 4 + [stage_last, stage_solve]
    scan = [f for step in range(NC) for f in (functools.partial(scan_first, step),
                                              functools.partial(scan_second, step))]
    for k in range(max(len(local), len(scan))):
        if k < len(local):
            local[k]()
        if k < len(scan):
            scan[k]()

    for n, ch in enumerate(chains):
        u_s[slot_w, n] = ch["u"]
        w_s[slot_w, n] = ch["w"].astype(bf16)
        qk_s[slot_w, n] = ch["qk"].astype(bf16)
        qdec_s[slot_w, n] = ch["qdec"].astype(bf16)
        kdec_s[slot_w, n] = ch["kdec"].astype(bf16)
        cd_s[slot_w, n] = jnp.broadcast_to(ch["cd"], (8, QUAD_W))


def _delta(qkvn, dab, a_log, dt_bias):
    b, s, _ = qkvn.shape
    H = DN_HEADS
    tb = DN_CHUNK * DELTA_BLOCK_CHUNKS
    nblk = s // tb
    total = b * nblk
    assert s % tb == 0
    d4 = dab.reshape(b, s, 4, H)
    ab = jnp.stack([jnp.concatenate([d4[:, :, r], d4[:, :, 2 + r]], -1) for r in range(2)])
    alog = a_log.astype(f32).reshape(2, 1, H)
    dtb = dt_bias.astype(f32).reshape(2, 1, H)

    def block(lin, rev):
        i = lin % nblk
        return lin // nblk, (nblk - 1 - i) if rev else i

    def sec(n, rev, lag):
        def index(t):
            bi, i = block(jnp.maximum(t - 1, 0) if lag else jnp.minimum(t, total - 1), rev)
            return bi, i, n
        return pl.BlockSpec((1, tb, DN_QK_W), index)

    def ab_spec(d):
        def index(t):
            bi, i = block(jnp.minimum(t, total - 1), d == 1)
            return d, bi, i, 0
        return pl.BlockSpec((1, 1, tb, 2 * H), index)

    chain_buf = lambda dt: pltpu.VMEM((2, DELTA_CHAINS, DN_CHUNK, QUAD_W), dt)
    return pl.pallas_call(
        functools.partial(_delta_kernel, nblk=nblk),
        grid=(total + 1,),
        in_specs=[sec(0, False, False), sec(1, False, False), sec(2, False, False),
                  sec(0, True, False), sec(1, True, False), sec(2, True, False),
                  ab_spec(0), ab_spec(1), _full((2, 1, H)), _full((2, 1, H))],
        out_specs=[sec(0, False, True), sec(0, True, True)],
        out_shape=[jax.ShapeDtypeStruct((b, s, DN_V_W), f32)] * 2,
        scratch_shapes=[pltpu.VMEM((2, N_QUADS, QUAD_W, QUAD_W), f32),
                        chain_buf(f32), chain_buf(bf16), chain_buf(bf16), chain_buf(bf16), chain_buf(bf16),
                        pltpu.VMEM((2, DELTA_CHAINS, 8, QUAD_W), f32)],
        compiler_params=_cparams(("arbitrary",)),
        name="delta",
    )(qkvn, qkvn, qkvn, qkvn, qkvn, qkvn, ab, ab, alog, dtb)


def _pool_band_tables():
    r = np.arange(POOL_TILE)[:, None]
    j = np.arange(POOL_SPAN)[None, :]
    tabs = np.zeros((3, N_POOL_GROUPS, POOL_TILE, POOL_SPAN), np.float32)
    for vi, shift in enumerate((0, 128, 256)):
        rel = j - shift - r
        for gi, w in enumerate(POOL_WINDOWS):
            tabs[vi, gi] = np.abs(rel) <= w // 2
    return jnp.asarray(tabs, dtype=bf16)


def _pool_tile(u_ref, band_ref, w_ref, scale_ref, t, seq):
    nt = seq // POOL_TILE
    t0 = pl.multiple_of(t * POOL_TILE, POOL_TILE)
    ks = pl.multiple_of(jnp.clip(t0 - 128, 0, seq - POOL_SPAN), 128)
    variant = jnp.where(t == 0, 0, jnp.where(t == nt - 1, 2, 1))
    win = u_ref[0, pl.ds(ks, POOL_SPAN), :]
    own = u_ref[0, pl.ds(t0, POOL_TILE), :].astype(f32)
    pos = t0 + lax.broadcasted_iota(jnp.int32, (POOL_TILE, POOL_GROUP), 0)
    outs = []
    for gi, w in enumerate(POOL_WINDOWS):
        r = w // 2
        sl = slice(POOL_GROUP * gi, POOL_GROUP * (gi + 1))
        wsum = _dot(band_ref[variant, gi], win[:, sl])
        count = (jnp.minimum(pos + r, seq - 1) - jnp.maximum(pos - r, 0) + 1).astype(f32)
        mixed = wsum / count - own[:, sl]
        outs.append(_dot(mixed.astype(bf16), w_ref[gi]))
    return jnp.concatenate(outs, -1) * scale_ref[...]


def _route_tile(x, w_t, carry_ref):
    xh, xl = _split2(x)
    wh, wl = _split2(w_t)
    logits = _dot_nt(wh, xh) + (_dot_nt(wh, xl) + _dot_nt(wl, xh))
    tm = logits.shape[1]
    sub = lax.broadcasted_iota(jnp.int32, logits.shape, 0)
    m1 = jnp.max(logits, 0, keepdims=True)
    i1 = jnp.min(jnp.where(logits == m1, sub, N_EXPERTS), 0, keepdims=True)
    rest = jnp.where(sub == i1, -jnp.inf, logits)
    m2 = jnp.max(rest, 0, keepdims=True)
    i2 = jnp.min(jnp.where(rest == m2, sub, N_EXPERTS), 0, keepdims=True)
    e2 = jnp.exp(m2 - m1)
    p1 = 1.0 / (1.0 + e2)
    p2 = e2 / (1.0 + e2)
    oh1 = jnp.where(sub == i1, 1.0, 0.0)
    oh2 = jnp.where(sub == i2, 1.0, 0.0)
    both = oh1 + oh2
    rr = lax.broadcasted_iota(jnp.int32, (tm, tm), 0)
    cc = lax.broadcasted_iota(jnp.int32, (tm, tm), 1)
    earlier = jnp.where(rr < cc, 1.0, 0.0).astype(bf16)
    before = _dot(both.astype(bf16), earlier) + carry_ref[...]
    r1 = jnp.sum(oh1 * before, 0, keepdims=True).astype(jnp.int32)
    r2 = jnp.sum(oh2 * before, 0, keepdims=True).astype(jnp.int32)
    carry_ref[...] += jnp.sum(both, 1, keepdims=True)
    mi = jnp.where(sub == 0, i1, jnp.where(sub == 1, i2, jnp.where(sub == 2, r1, jnp.where(sub == 3, r2, 0))))
    mf = jnp.where(sub == 0, p1, jnp.where(sub == 1, p2, 0.0))
    return mi, mf


def _merge_kernel(oa_ref, odf_ref, odb_ref, z_ref, pu_ref, gate_ref, h_ref, ones_ref, nw_ref, band_ref, pw_ref,
                  ps_ref, wb_ref, wo_ref, g_ref, b_ref, *refs, seq, routed):
    if routed:
        rw_ref, hf_ref, hb_ref, mi_ref, mf_ref, cnt_ref, carry_ref = refs

        @pl.when(pl.program_id(0) == 0)
        def _():
            carry_ref[...] = jnp.zeros(carry_ref.shape, f32)
    else:
        hf_ref, hb_ref = refs
    od = odf_ref[...] + odb_ref[...]
    ms = _dot_split2_rhs(od * od, ones_ref[...]) * (1.0 / DN_DV)
    od = od * lax.rsqrt(ms + RMS_EPS) * nw_ref[...] * _silu(z_ref[...].astype(f32))
    first = (pl.program_id(0) * MERGE_POOL_TILES) % (seq // POOL_TILE)
    op = jnp.concatenate([_pool_tile(pu_ref, band_ref, pw_ref, ps_ref, first + k, seq)
                          for k in range(MERGE_POOL_TILES)], axis=0)
    branches = (oa_ref[...], od.astype(bf16), op.astype(bf16))
    merged = None
    for n in range(N_BRANCH):
        up = _dot(branches[n], wb_ref[n])
        gate = _sigmoid(gate_ref[:, D_MODEL * n:D_MODEL * (n + 1)].astype(f32))
        merged = gate * up if merged is None else merged + gate * up
    y = _dot(merged.astype(bf16), wo_ref[...])
    hn = _layer_norm(DEEPNORM_ALPHA * h_ref[...] + y, g_ref[...], b_ref[...])
    hf_ref[...] = hn
    if routed:
        hb_ref[...] = _pack_bf16_pair(hn[:, :D_MODEL // 2], hn[:, D_MODEL // 2:])
        mi_ref[...], mf_ref[...] = _route_tile(hn, rw_ref[...], carry_ref)
        cnt_ref[...] = jnp.broadcast_to(carry_ref[...], cnt_ref.shape).astype(jnp.int32)
    else:
        hb_ref[...] = hn.astype(bf16)


def _merge(oa, odf, odb, z, pu, gates, h, dn_norm_w, pool_w, pool_scale, w_branch, w_out, ln_g, ln_b, router_w):
    m = h.shape[0]
    bsz, seq, _ = pu.shape
    tm = POOL_TILE * MERGE_POOL_TILES
    nt = seq // tm
    assert seq % tm == 0 and seq >= POOL_SPAN
    routed = router_w is not None
    row = lambda w: pl.BlockSpec((tm, w), lambda i: (i, 0))
    nw = jnp.tile(dn_norm_w.astype(f32), DN_HEADS).reshape(1, DN_V_W)
    in_specs = [row(ATTN_W), row(DN_V_W), row(DN_V_W), row(DN_V_W),
                pl.BlockSpec((1, seq, POOL_W), lambda i: (i // nt, 0, 0)),
                row(N_BRANCH * D_MODEL), row(D_MODEL),
                _full((DN_V_W, DN_V_W)), _full((1, DN_V_W)),
                _full((3, N_POOL_GROUPS, POOL_TILE, POOL_SPAN)), _full((N_POOL_GROUPS, POOL_GROUP, POOL_GROUP)),
                _full((1, POOL_W)),
                _full((N_BRANCH, ATTN_W, D_MODEL)), _full((D_MODEL, D_MODEL)),
                _full((1, D_MODEL)), _full((1, D_MODEL))]
    args = [oa, odf, odb, z, pu, gates, h, _head_ones(), nw, _pool_band_tables(), pool_w.astype(bf16),
            pool_scale.reshape(1, POOL_W).astype(f32), w_branch.astype(bf16), w_out.astype(bf16),
            ln_g.reshape(1, D_MODEL).astype(f32), ln_b.reshape(1, D_MODEL).astype(f32)]
    out_specs = [row(D_MODEL)]
    out_shape = [jax.ShapeDtypeStruct((m, D_MODEL), f32)]
    scratch = []
    if routed:
        col = pl.BlockSpec((N_EXPERTS, tm), lambda i: (0, i))
        in_specs.append(_full((N_EXPERTS, D_MODEL)))
        args.append(router_w.astype(f32).T)
        out_specs += [row(PACKED_W), col, col, _full((N_EXPERTS, LANES))]
        out_shape += [jax.ShapeDtypeStruct((m, PACKED_W), jnp.uint32),
                      jax.ShapeDtypeStruct((N_EXPERTS, m), jnp.int32),
                      jax.ShapeDtypeStruct((N_EXPERTS, m), f32),
                      jax.ShapeDtypeStruct((N_EXPERTS, LANES), jnp.int32)]
        scratch.append(pltpu.VMEM((N_EXPERTS, 1), f32))
    else:
        out_specs.append(row(D_MODEL))
        out_shape.append(jax.ShapeDtypeStruct((m, D_MODEL), bf16))
    return pl.pallas_call(
        functools.partial(_merge_kernel, seq=seq, routed=routed),
        grid=(m // tm,),
        in_specs=in_specs,
        out_specs=out_specs,
        out_shape=out_shape,
        scratch_shapes=scratch,
        compiler_params=_cparams(("arbitrary",)),
        name="merge",
    )(*args)


def _swiglu_chunk(x, wg, wu, wd):
    g = _dot(x, wg.astype(bf16))
    u = _dot(x, wu.astype(bf16))
    return _dot((_silu(g) * u).astype(bf16), wd.astype(bf16))


def _ffn_kernel(x_ref, h_ref, wg_ref, wu_ref, wd_ref, g_ref, b_ref, hf_ref, hb_ref, acc_ref):
    acc_ref[...] = _swiglu_chunk(x_ref[...], wg_ref[0], wu_ref[0], wd_ref[0])

    def chunk(j, carry):
        acc_ref[...] += _swiglu_chunk(x_ref[...], wg_ref[j], wu_ref[j], wd_ref[j])
        return carry

    lax.fori_loop(1, wg_ref.shape[0], chunk, 0)
    hn = _layer_norm(DEEPNORM_ALPHA * h_ref[...] + acc_ref[...], g_ref[...], b_ref[...])
    hf_ref[...] = hn
    hb_ref[...] = hn.astype(bf16)


def _ffn(hb, hf, w1, w3, w2, ln_g, ln_b, tm=1024, tf=256):
    m = hb.shape[0]
    tm = min(tm, m)
    ff = w1.shape[-1]
    nj = ff // tf
    row = pl.BlockSpec((tm, D_MODEL), lambda i: (i, 0))
    resident = lambda shape: pl.BlockSpec(shape, lambda i: (0,) * len(shape), pipeline_mode=pl.Buffered(1))
    up_chunks = lambda w: jnp.transpose(w.astype(bf16).reshape(D_MODEL, nj, tf), (1, 0, 2))
    return pl.pallas_call(
        _ffn_kernel,
        grid=(m // tm,),
        in_specs=[row, row, resident((nj, D_MODEL, tf)), resident((nj, D_MODEL, tf)), resident((nj, tf, D_MODEL)),
                  _full((1, D_MODEL)), _full((1, D_MODEL))],
        out_specs=[row, row],
        out_shape=[jax.ShapeDtypeStruct((m, D_MODEL), f32), jax.ShapeDtypeStruct((m, D_MODEL), bf16)],
        scratch_shapes=[pltpu.VMEM((tm, D_MODEL), f32)],
        compiler_params=_cparams(("parallel",)),
        name="ffn",
    )(hb, hf, up_chunks(w1), up_chunks(w3), w2.astype(bf16).reshape(nj, tf, D_MODEL),
      ln_g.reshape(1, D_MODEL).astype(f32), ln_b.reshape(1, D_MODEL).astype(f32))


MOE_TILE = 1024
PACKED_W = D_MODEL // 2
SC_CORES = 2
SC_SUBCORES = 16
SC_WORKERS = SC_CORES * SC_SUBCORES
SC_ROWS = 64


def _gmm_kernel(te_ref, tv_ref, x_ref, wg_ref, wu_ref, wd_ref, o_ref, acc_ref):
    i = pl.program_id(0)
    j = pl.program_id(1)
    valid = tv_ref[i]

    @pl.when(j == 0)
    def _():
        acc_ref[...] = jnp.zeros(acc_ref.shape, f32)

    @pl.when(valid > 0)
    def _():
        rows = lax.broadcasted_iota(jnp.int32, acc_ref.shape, 0)
        x = jnp.where(rows < valid, _unpack_bf16_pair(x_ref[...]), 0.0).astype(bf16)
        acc_ref[...] += _swiglu_chunk(x, wg_ref[0], wu_ref[0], wd_ref[0])

    @pl.when(j == pl.num_programs(1) - 1)
    def _():
        o_ref[...] = _pack_bf16_pair(acc_ref[:, :PACKED_W], acc_ref[:, PACKED_W:])


def _gmm(xs, tile_expert, tile_valid, wg, wu, wd, tm, tf=512):
    n = xs.shape[0]
    ff = wg.shape[-1]
    nj = ff // tf

    def fcol(i, j, te, tv):
        return jnp.where(tv[i] > 0, j, nj - 1)

    grid_spec = pltpu.PrefetchScalarGridSpec(
        num_scalar_prefetch=2,
        grid=(n // tm, nj),
        in_specs=[pl.BlockSpec((tm, PACKED_W), lambda i, j, te, tv: (i, 0)),
                  pl.BlockSpec((1, D_MODEL, tf), lambda i, j, te, tv: (te[i], 0, fcol(i, j, te, tv))),
                  pl.BlockSpec((1, D_MODEL, tf), lambda i, j, te, tv: (te[i], 0, fcol(i, j, te, tv))),
                  pl.BlockSpec((1, tf, D_MODEL), lambda i, j, te, tv: (te[i], fcol(i, j, te, tv), 0))],
        out_specs=pl.BlockSpec((tm, PACKED_W), lambda i, j, te, tv: (i, 0)),
        scratch_shapes=[pltpu.VMEM((tm, D_MODEL), f32)])
    return pl.pallas_call(
        _gmm_kernel,
        grid_spec=grid_spec,
        out_shape=jax.ShapeDtypeStruct((n, PACKED_W), jnp.uint32),
        compiler_params=_cparams(("parallel", "arbitrary")),
        name="moe_gmm",
    )(tile_expert, tile_valid, xs, wg, wu, wd)


def _sc_mesh():
    return plsc.VectorSubcoreMesh(core_axis_name="c", subcore_axis_name="s",
                                  num_cores=SC_CORES, num_subcores=SC_SUBCORES)


def _sc_worker_base(per_worker):
    return (lax.axis_index("s") * SC_CORES + lax.axis_index("c")) * per_worker


def _sc_scratch(d, dtype):
    return [pltpu.VMEM((SC_ROWS,), jnp.int32), pltpu.VMEM((SC_ROWS,), jnp.int32), pltpu.VMEM((2, SC_ROWS, d), dtype),
            pltpu.SemaphoreType.DMA((2,)), pltpu.SemaphoreType.DMA((2,))]


def _sc_scatter_rows(src, idx0, idx1, n_out):
    m, d = src.shape
    per_worker = m // SC_WORKERS
    n_chunks = per_worker // SC_ROWS
    assert m % (SC_WORKERS * SC_ROWS * 2) == 0

    def body(src_hbm, idx0_hbm, idx1_hbm, out_hbm, idx_a, idx_b, rows_v, sem_in, sem_out):
        base = _sc_worker_base(per_worker)
        idx_v = (idx_a, idx_b)

        def load(j, slot):
            off = pl.multiple_of(base + j * SC_ROWS, SC_ROWS)
            return pltpu.make_async_copy(src_hbm.at[pl.ds(off, SC_ROWS)], rows_v.at[slot], sem_in.at[slot])

        def store(j, slot, idx_hbm):
            off = pl.multiple_of(base + j * SC_ROWS, SC_ROWS)
            pltpu.sync_copy(idx_hbm.at[pl.ds(off, SC_ROWS)], idx_v[slot])
            pltpu.async_copy(rows_v.at[slot], out_hbm.at[idx_v[slot]], sem_out.at[slot]).wait()

        load(0, 0).start()

        @pl.loop(0, n_chunks, step=2)
        def _(j):
            for slot in range(2):
                @pl.when(j + slot + 1 < n_chunks)
                def _():
                    load(j + slot + 1, 1 - slot).start()
                load(j + slot, slot).wait()
                store(j + slot, slot, idx0_hbm)
                store(j + slot, slot, idx1_hbm)

    return pl.kernel(
        body, out_type=jax.ShapeDtypeStruct((n_out, d), src.dtype), mesh=_sc_mesh(),
        scratch_types=_sc_scratch(d, src.dtype), name="moe_dispatch",
    )(src, idx0, idx1)


def _sc_gather_rows(table, idx):
    n = idx.shape[0]
    d = table.shape[1]
    per_worker = n // SC_WORKERS
    n_chunks = per_worker // SC_ROWS
    assert n % (SC_WORKERS * SC_ROWS * 2) == 0

    def body(table_hbm, idx_hbm, out_hbm, idx_a, idx_b, rows_v, sem_in, sem_out):
        base = _sc_worker_base(per_worker)
        idx_v = (idx_a, idx_b)

        def gather(j, slot):
            off = pl.multiple_of(base + j * SC_ROWS, SC_ROWS)
            pltpu.sync_copy(idx_hbm.at[pl.ds(off, SC_ROWS)], idx_v[slot])
            return pltpu.make_async_copy(table_hbm.at[idx_v[slot]], rows_v.at[slot], sem_in.at[slot])

        def wait_gather(slot):
            pltpu.make_async_copy(table_hbm.at[idx_v[slot]], rows_v.at[slot], sem_in.at[slot]).wait()

        def write(j, slot):
            off = pl.multiple_of(base + j * SC_ROWS, SC_ROWS)
            return pltpu.make_async_copy(rows_v.at[slot], out_hbm.at[pl.ds(off, SC_ROWS)], sem_out.at[slot])

        gather(0, 0).start()

        @pl.loop(0, n_chunks, step=2)
        def _(j):
            for slot in range(2):
                @pl.when(j + slot + 1 < n_chunks)
                def _():
                    @pl.when(j + slot >= 1)
                    def _():
                        write(j + slot - 1, 1 - slot).wait()
                    gather(j + slot + 1, 1 - slot).start()
                wait_gather(slot)
                write(j + slot, slot).start()

        write(n_chunks - 2, 0).wait()
        write(n_chunks - 1, 1).wait()

    return pl.kernel(
        body, out_type=jax.ShapeDtypeStruct((n, d), table.dtype), mesh=_sc_mesh(),
        scratch_types=_sc_scratch(d, table.dtype), name="moe_collect",
    )(table, idx)


def _combine_kernel(y0_ref, y1_ref, mf_ref, h_ref, g_ref, b_ref, o_ref):
    p = mf_ref[...]
    y = p[:, 0:1] * _unpack_bf16_pair(y0_ref[0]) + p[:, 1:2] * _unpack_bf16_pair(y1_ref[0])
    o_ref[...] = _layer_norm(DEEPNORM_ALPHA * h_ref[...] + y, g_ref[...], b_ref[...])


def _combine(yg, mf, hf, ln_g, ln_b, tm=1024):
    m = hf.shape[0]
    tm = min(tm, m)
    return pl.pallas_call(
        _combine_kernel,
        grid=(m // tm,),
        in_specs=[pl.BlockSpec((1, tm, PACKED_W), lambda i: (0, i, 0)),
                  pl.BlockSpec((1, tm, PACKED_W), lambda i: (1, i, 0)),
                  pl.BlockSpec((tm, N_EXPERTS), lambda i: (i, 0)),
                  pl.BlockSpec((tm, D_MODEL), lambda i: (i, 0)),
                  _full((1, D_MODEL)), _full((1, D_MODEL))],
        out_specs=pl.BlockSpec((tm, D_MODEL), lambda i: (i, 0)),
        out_shape=jax.ShapeDtypeStruct((m, D_MODEL), f32),
        compiler_params=_cparams(("parallel",)),
        name="moe_combine",
    )(yg, yg, mf, hf, ln_g.reshape(1, D_MODEL).astype(f32), ln_b.reshape(1, D_MODEL).astype(f32))


def _moe(hf, hp, mi, mf, cnt, wg, wu, wd, ln_g, ln_b):
    m = hf.shape[0]
    tm = MOE_TILE
    counts = cnt[:, 0]
    padded = (counts + tm - 1) // tm * tm
    ends = jnp.cumsum(padded)
    starts = ends - padded
    chosen = mi[0:TOP_K, :, None] == jnp.arange(N_EXPERTS, dtype=jnp.int32)
    pos = jnp.sum(jnp.where(chosen, starts, 0), -1) + mi[TOP_K:2 * TOP_K]
    n_tiles = TOP_K * m // tm + N_EXPERTS
    tile_start = jnp.arange(n_tiles, dtype=jnp.int32) * tm
    tile_expert = jnp.minimum(jnp.sum(tile_start[:, None] >= ends[None, :], -1), N_EXPERTS - 1).astype(jnp.int32)
    tile_valid = jnp.clip(starts[tile_expert] + counts[tile_expert] - tile_start, 0, tm).astype(jnp.int32)
    tile_valid = jnp.where(tile_start < ends[-1], tile_valid, 0)
    xs = _sc_scatter_rows(hp, pos[0], pos[1], n_tiles * tm)
    ys = _gmm(xs, tile_expert, tile_valid, wg, wu, wd, tm)
    yg = _sc_gather_rows(ys, pos.reshape(TOP_K * m))
    return _combine(yg.reshape(TOP_K, m, PACKED_W), mf.T, hf, ln_g, ln_b)


_SPLITS = (ATTN_W, KV_W, KV_W, 3 * DN_QK_W, DN_V_W, 4 * DN_HEADS, POOL_W, N_BRANCH * D_MODEL)
_SPLIT_DTYPES = (bf16, bf16, f32, bf16, f32, bf16, bf16)


def _mixer(hf, hb, bsz, seq, bias_tabs, w_in, sink, conv_w, a_log, dt_bias, dn_norm_w, pool_w, pool_scale,
           w_branch, w_out, ln_g, ln_b, router_w):
    points = np.cumsum(_SPLITS)[:-1].tolist()
    wq, wk, wv, *rest = jnp.split(w_in.astype(bf16), points, axis=-1)
    wkv = jnp.concatenate([wk[:, HEAD_DIM * g:HEAD_DIM * (g + 1)] for g in range(ATTN_KV_HEADS) for _ in range(2)]
                          + [wv], axis=-1)
    aq, akv, qkvn, dz, dab, pu, gates = _inproj(hf if hb is None else hb, [wq, wkv] + rest, _SPLIT_DTYPES,
                                                conv_w, seq, dn=2)
    shp = lambda t: t.reshape(bsz, seq, t.shape[-1])
    oa = _attention(shp(aq), shp(akv), bias_tabs, sink)
    odf, odb = _delta(shp(qkvn), shp(dab), a_log, dt_bias)
    m = bsz * seq
    return _merge(oa.reshape(m, ATTN_W), odf.reshape(m, DN_V_W), odb.reshape(m, DN_V_W), dz, shp(pu), gates, hf,
                  dn_norm_w, pool_w, pool_scale, w_branch, w_out, ln_g, ln_b, router_w)


def kernel(x, w_in, attn_sink, rel_bias, conv_w, a_log, dt_bias, dn_norm_w, pool_w, pool_scale, w_branch, w_out,
           ln1_g, ln1_b, ln2_g, ln2_b, ffn_w1, ffn_w3, ffn_w2, router_w, moe_wg, moe_wu, moe_wd):
    bsz, seq, _ = x.shape
    m = bsz * seq
    bias_tabs = _attn_bias_tables(rel_bias)
    hf = x.reshape(m, D_MODEL).astype(f32)
    hb = None
    for layer in range(DEPTH):
        dense = layer % 2 == 0
        i = layer // 2
        hf, *rest = _mixer(hf, hb, bsz, seq, bias_tabs, w_in[layer], attn_sink[layer], conv_w[layer], a_log[layer],
                           dt_bias[layer], dn_norm_w[layer], pool_w[layer], pool_scale[layer], w_branch[layer],
                           w_out[layer], ln1_g[layer], ln1_b[layer], None if dense else router_w[i])
        if dense:
            hf, hb = _ffn(rest[0], hf, ffn_w1[i], ffn_w3[i], ffn_w2[i], ln2_g[layer], ln2_b[layer])
        else:
            hf = _moe(hf, *rest, moe_wg[i], moe_wu[i], moe_wd[i], ln2_g[layer], ln2_b[layer])
            hb = None
    return hf.reshape(bsz, seq, D_MODEL).astype(x.dtype)
```
